```python
import math
import jax, jax.numpy as jnp
from jax import lax
import numpy as np

D_MODEL = 2048
BATCH = 4
SEQ = 8192
DEPTH = 4

GRID_W = 64
CTX_LEN = 256
N_MLA_HEADS = 8
Q_RANK = 512
KV_RANK = 256
QK_NOPE = 128
QK_ROPE = 64
V_HEAD = 128
QK_HEAD = QK_NOPE + QK_ROPE
ATTN_SCALE = QK_HEAD ** -0.5
MLA_IN = Q_RANK + KV_RANK + QK_ROPE
MLA_OUT = N_MLA_HEADS * V_HEAD
Q_BLOCK = 128
ROPE_AXIS = QK_ROPE // 2
ROPE_FREQS = ROPE_AXIS // 2
ROPE_THETA = 10000.0
POOL_WIDTH = D_MODEL // 2
POOL_WINDOWS = (2, 4, 8, 16)
POOL_GROUPS = len(POOL_WINDOWS)
POOL_GROUP = POOL_WIDTH // POOL_GROUPS
IN_WIDTH = MLA_IN + POOL_WIDTH
EVEN_MIX_WIDTH = MLA_OUT + POOL_WIDTH
FOURIER_GROUPS = 4
FOURIER_GROUP = D_MODEL // FOURIER_GROUPS
D_FF = 4 * D_MODEL
EPS = 1e-6
N_EVEN = (DEPTH + 1) // 2
N_ODD = DEPTH // 2

kernel_name = 'hybrid_mla_pool_fourier_dit'


def rmsnorm(x, g):
    xf = x.astype(jnp.float32)
    y = xf * lax.rsqrt(jnp.mean(xf * xf, axis=-1, keepdims=True) + EPS)
    return (y * g.astype(jnp.float32)).astype(x.dtype)


def modulate(h, shift, scale):
    return h * (1 + scale) + shift


def axial_rope_tables(n):
    rows = n // GRID_W
    r = jnp.broadcast_to(jnp.arange(rows, dtype=jnp.float32)[:, None], (rows, GRID_W)).reshape(n)
    col = jnp.broadcast_to(jnp.arange(GRID_W, dtype=jnp.float32)[None, :], (rows, GRID_W)).reshape(n)
    inv = ROPE_THETA ** (-2.0 * jnp.arange(ROPE_FREQS, dtype=jnp.float32) / ROPE_AXIS)
    ang = jnp.stack([r[:, None] * inv, col[:, None] * inv], axis=1)
    ang = jnp.broadcast_to(ang[:, :, None, :], (n, 2, 2, ROPE_FREQS)).reshape(n, QK_ROPE)
    return jnp.cos(ang), jnp.sin(ang)


def rotate_half_axial(x):
    xr = x.reshape(x.shape[:-1] + (2, 2, ROPE_FREQS))
    return jnp.concatenate([-xr[..., 1:, :], xr[..., :1, :]], axis=-2).reshape(x.shape)


def apply_rope(x, cos, sin):
    return x * cos + rotate_half_axial(x) * sin


def mla_queries(p, q_norm, w_uq):
    b, n, _ = p.shape
    q = (rmsnorm(p[..., :Q_RANK], q_norm) @ w_uq).reshape(b, n, N_MLA_HEADS, QK_HEAD)
    return q[..., :QK_NOPE], q[..., QK_NOPE:]


def mla_keys(p, kv_norm, w_ukv):
    b, n, _ = p.shape
    kv_lat = p[..., Q_RANK:Q_RANK + KV_RANK]
    k_rope = p[..., Q_RANK + KV_RANK:MLA_IN]
    kv = (rmsnorm(kv_lat, kv_norm) @ w_ukv).reshape(b, n, N_MLA_HEADS, QK_NOPE + V_HEAD)
    return kv[..., :QK_NOPE], k_rope, kv[..., QK_NOPE:]


def attend(qn, qr, kn, kr, v):
    s = jnp.einsum('bqhd,bkhd->bhqk', qn, kn) + jnp.einsum('bqhr,bkr->bhqk', qr, kr)
    p = jax.nn.softmax(s.astype(jnp.float32) * ATTN_SCALE, axis=-1).astype(v.dtype)
    return jnp.einsum('bhqk,bkhd->bqhd', p, v)


def latent_attention(qn, qr, kn, kr, v):
    b, n, h, _ = qn.shape
    nb = n // Q_BLOCK

    def to_blocks(t):
        return jnp.moveaxis(t.reshape((b, nb, Q_BLOCK) + t.shape[2:]), 1, 0)

    out = lax.map(lambda qs: attend(qs[0], qs[1], kn, kr, v), (to_blocks(qn), to_blocks(qr)))
    return jnp.moveaxis(out, 0, 1).reshape(b, n, h * V_HEAD)


def multiscale_pool(u, w_pool, pool_scale):
    b, n, _ = u.shape
    ug = u.reshape(b, n, POOL_GROUPS, POOL_GROUP).astype(jnp.float32)
    cs = jnp.concatenate([jnp.zeros((b, 1, POOL_GROUPS, POOL_GROUP), jnp.float32),
                          jnp.cumsum(ug, axis=1)], axis=1)
    t = jnp.arange(n)
    means = []
    for gi, w in enumerate(POOL_WINDOWS):
        lo = jnp.clip(t - w // 2, 0, n)
        hi = jnp.clip(t + w // 2, 0, n)
        cg = cs[:, :, gi]
        means.append((cg[:, hi] - cg[:, lo]) / (hi - lo).astype(jnp.float32)[None, :, None])
    pooled = (jnp.stack(means, axis=2) - ug).astype(u.dtype)
    y = jnp.einsum('bngc,gcd->bngd', pooled, w_pool)
    return y.reshape(b, n, POOL_WIDTH) * pool_scale


def even_mixer(hx, hc, w_in, q_norm, w_uq, kv_norm, w_ukv, w_pool, pool_scale, w_out, cos, sin, ctx_out):
    px = hx @ w_in
    pc = hc @ w_in
    qn, qr = mla_queries(px, q_norm, w_uq)
    kn, kr, v = mla_keys(px, kv_norm, w_ukv)
    ckn, ckr, cv = mla_keys(pc, kv_norm, w_ukv)
    qr = apply_rope(qr, cos[:, None, :], sin[:, None, :])
    kr = apply_rope(kr, cos, sin)
    kn_all = jnp.concatenate([ckn, kn], axis=1)
    kr_all = jnp.concatenate([ckr, kr], axis=1)
    v_all = jnp.concatenate([cv, v], axis=1)
    attn_x = latent_attention(qn, qr, kn_all, kr_all, v_all)
    pool_x = multiscale_pool(px[..., MLA_IN:], w_pool, pool_scale)
    yx = jnp.concatenate([attn_x, pool_x], axis=-1) @ w_out
    if not ctx_out:
        return yx, None
    b, l, _ = hc.shape
    cqn, cqr = mla_queries(pc, q_norm, w_uq)
    attn_c = attend(cqn, cqr, ckn, ckr, cv).reshape(b, l, MLA_OUT)
    pool_c = multiscale_pool(pc[..., MLA_IN:], w_pool, pool_scale)
    yc = jnp.concatenate([attn_c, pool_c], axis=-1) @ w_out
    return yx, yc


def fourier_mixer(h, w_out):
    b, n, _ = h.shape
    hg = h.astype(jnp.float32).reshape(b, n, FOURIER_GROUPS, FOURIER_GROUP)
    f = jnp.fft.fftn(hg, axes=(1, 3), norm='ortho').real
    return f.reshape(b, n, D_MODEL).astype(h.dtype) @ w_out


def sq_relu_mlp(h, w1, w2):
    return jnp.square(jax.nn.relu(h @ w1)) @ w2


def setup_inputs(seed: int = 0) -> dict:
    key = jax.random.key(seed)
    ks = jax.random.split(key, 20)

    def nrm(k, shape, scale):
        return jax.random.normal(k, shape, jnp.float32) * scale

    return {
        'x': nrm(ks[0], (BATCH, SEQ, D_MODEL), 1.0),
        'c': nrm(ks[1], (BATCH, D_MODEL), 1.0),
        'ctx': nrm(ks[2], (BATCH, CTX_LEN, D_MODEL), 1.0),
        'c_ctx': nrm(ks[3], (D_MODEL,), 1.0),
        'w_mod': nrm(ks[4], (DEPTH, D_MODEL, 6 * D_MODEL), 0.5 * D_MODEL ** -0.5),
        'b_mod': nrm(ks[5], (DEPTH, 6 * D_MODEL), 0.02),
        'norm1': 1.0 + nrm(ks[6], (DEPTH, D_MODEL), 0.05),
        'norm2': 1.0 + nrm(ks[7], (DEPTH, D_MODEL), 0.05),
        'w_in': nrm(ks[8], (N_EVEN, D_MODEL, IN_WIDTH), D_MODEL ** -0.5),
        'q_norm': 1.0 + nrm(ks[9], (N_EVEN, Q_RANK), 0.05),
        'w_uq': nrm(ks[10], (N_EVEN, Q_RANK, N_MLA_HEADS * QK_HEAD), Q_RANK ** -0.5),
        'kv_norm': 1.0 + nrm(ks[11], (N_EVEN, KV_RANK), 0.05),
        'w_ukv': nrm(ks[12], (N_EVEN, KV_RANK, N_MLA_HEADS * (QK_NOPE + V_HEAD)), KV_RANK ** -0.5),
        'w_pool': nrm(ks[13], (N_EVEN, POOL_GROUPS, POOL_GROUP, POOL_GROUP), POOL_GROUP ** -0.5),
        'pool_scale': 1.0 + nrm(ks[14], (N_EVEN, POOL_WIDTH), 0.1),
        'w_out_even': nrm(ks[15], (N_EVEN, EVEN_MIX_WIDTH, D_MODEL), EVEN_MIX_WIDTH ** -0.5),
        'w_out_odd': nrm(ks[16], (N_ODD, D_MODEL, D_MODEL), D_MODEL ** -0.5),
        'w_mlp1': nrm(ks[17], (DEPTH, D_MODEL, D_FF), D_MODEL ** -0.5),
        'w_mlp2': nrm(ks[18], (DEPTH, D_FF, D_MODEL), D_FF ** -0.5),
        'final_norm': 1.0 + nrm(ks[19], (D_MODEL,), 0.05),
    }


def reference(x, c, ctx, c_ctx, w_mod, b_mod, norm1, norm2, w_in, q_norm, w_uq, kv_norm, w_ukv,
              w_pool, pool_scale, w_out_even, w_out_odd, w_mlp1, w_mlp2, final_norm):
    n = x.shape[1]
    cos, sin = axial_rope_tables(n)
    cos = cos.astype(x.dtype)
    sin = sin.astype(x.dtype)
    for l in range(DEPTH):
        last = l == DEPTH - 1
        even = l % 2 == 0
        i = l // 2
        mod_x = (jax.nn.silu(c) @ w_mod[l] + b_mod[l])[:, None, :]
        sh1, sc1, g1, sh2, sc2, g2 = jnp.split(mod_x, 6, axis=-1)
        hx = modulate(rmsnorm(x, norm1[l]), sh1, sc1)
        need_ctx = (not last) or even
        if need_ctx:
            mod_c = jax.nn.silu(c_ctx) @ w_mod[l] + b_mod[l]
            csh1, csc1, cg1, csh2, csc2, cg2 = jnp.split(mod_c, 6, axis=-1)
            hc = modulate(rmsnorm(ctx, norm1[l]), csh1, csc1)
        if even:
            yx, yc = even_mixer(hx, hc, w_in[i], q_norm[i], w_uq[i], kv_norm[i], w_ukv[i],
                                w_pool[i], pool_scale[i], w_out_even[i], cos, sin, not last)
        else:
            yx = fourier_mixer(hx, w_out_odd[i])
            yc = None if last else fourier_mixer(hc, w_out_odd[i])
        x = x + g1 * yx
        x = x + g2 * sq_relu_mlp(modulate(rmsnorm(x, norm2[l]), sh2, sc2), w_mlp1[l], w_mlp2[l])
        if not last:
            ctx = ctx + cg1 * yc
            ctx = ctx + cg2 * sq_relu_mlp(modulate(rmsnorm(ctx, norm2[l]), csh2, csc2),
                                          w_mlp1[l], w_mlp2[l])
    return rmsnorm(x, final_norm)
```

```python
import functools
import math

import jax
import jax.numpy as jnp
import numpy as np
from jax import lax
from jax.experimental import pallas as pl
from jax.experimental.pallas import tpu as pltpu

D_MODEL = 2048
DEPTH = 4
GRID_W = 64
N_HEADS = 8
Q_RANK = 512
KV_RANK = 256
QK_NOPE = 128
QK_ROPE = 64
V_HEAD = 128
QK_HEAD = QK_NOPE + QK_ROPE
ATTN_SCALE = QK_HEAD ** -0.5
MLA_OUT = N_HEADS * V_HEAD
ROPE_AXIS = QK_ROPE // 2
ROPE_FREQS = ROPE_AXIS // 2
ROPE_THETA = 10000.0
POOL_WIDTH = D_MODEL // 2
POOL_WINDOWS = (2, 4, 8, 16)
POOL_GROUP = POOL_WIDTH // len(POOL_WINDOWS)
POOL_HALO = 8
FOURIER_GROUPS = 4
FOURIER_GROUP = D_MODEL // FOURIER_GROUPS
D_FF = 4 * D_MODEL
EPS = 1e-6

P_POOL = 0
P_QLAT = POOL_WIDTH
P_KV = POOL_WIDTH + Q_RANK
P_WIDTH = POOL_WIDTH + Q_RANK + KV_RANK + 2 * QK_ROPE
KV_BLOCK = KV_RANK + 2 * QK_ROPE

DFT_N1 = 64
DFT_N2 = 128
DFT_ROWS = 16

V7X_VMEM_LIMIT = 56 * 1024 * 1024

F32 = jnp.float32
BF16 = jnp.bfloat16


def _params(*sem):
    return pltpu.CompilerParams(dimension_semantics=sem, vmem_limit_bytes=V7X_VMEM_LIMIT)


def _resident(shape):
    nd = len(shape)
    return pl.BlockSpec(shape, lambda *_: (0,) * nd, pipeline_mode=pl.Buffered(1))


def _rms(x, g):
    return x * lax.rsqrt(jnp.mean(x * x, axis=-1, keepdims=True) + EPS) * g


def _mod_kernel(c_ref, w_ref, b_ref, o_ref):
    c = c_ref[...]
    s = c * (1.0 / (1.0 + jnp.exp(-c)))
    o_ref[0] = jnp.dot(s, w_ref[0], preferred_element_type=F32, precision=lax.Precision.HIGHEST) + b_ref[0]


def _modulation(cc, w_mod, b_mod):
    rows = cc.shape[0]
    tn = 1536
    n6 = 6 * D_MODEL
    return pl.pallas_call(
        _mod_kernel,
        grid=(DEPTH, n6 // tn),
        in_specs=[
            pl.BlockSpec((rows, D_MODEL), lambda l, j: (0, 0)),
            pl.BlockSpec((1, D_MODEL, tn), lambda l, j: (l, 0, j)),
            pl.BlockSpec((1, 1, tn), lambda l, j: (l, 0, j)),
        ],
        out_specs=pl.BlockSpec((1, rows, tn), lambda l, j: (l, 0, j)),
        out_shape=jax.ShapeDtypeStruct((DEPTH, rows, n6), F32),
        compiler_params=_params("parallel", "parallel"),
        name="modulation",
    )(cc, w_mod, b_mod.reshape(DEPTH, 1, n6))


def _normmod_kernel(x_ref, g_ref, sh_ref, sc_ref, o_ref):
    y = _rms(x_ref[0], g_ref[...])
    o_ref[0] = (y * (1.0 + sc_ref[0]) + sh_ref[0]).astype(o_ref.dtype)


def _normmod(x, g, sh, sc, out_dtype=BF16):
    b, n, d = x.shape
    tm = min(512, n)
    vec = pl.BlockSpec((1, 1, d), lambda bi, i: (bi, 0, 0))
    return pl.pallas_call(
        _normmod_kernel,
        grid=(b, n // tm),
        in_specs=[pl.BlockSpec((1, tm, d), lambda bi, i: (bi, i, 0)),
                  pl.BlockSpec((1, d), lambda bi, i: (0, 0)), vec, vec],
        out_specs=pl.BlockSpec((1, tm, d), lambda bi, i: (bi, i, 0)),
        out_shape=jax.ShapeDtypeStruct((b, n, d), out_dtype),
        compiler_params=_params("parallel", "parallel"),
        name="normmod",
    )(x, g.reshape(1, d), sh, sc)


def _final_norm_kernel(x_ref, g_ref, o_ref):
    o_ref[0] = _rms(x_ref[0], g_ref[...])


def _final_norm(x, g):
    b, n, d = x.shape
    tm = 512
    return pl.pallas_call(
        _final_norm_kernel,
        grid=(b, n // tm),
        in_specs=[pl.BlockSpec((1, tm, d), lambda bi, i: (bi, i, 0)),
                  pl.BlockSpec((1, d), lambda bi, i: (0, 0))],
        out_specs=pl.BlockSpec((1, tm, d), lambda bi, i: (bi, i, 0)),
        out_shape=jax.ShapeDtypeStruct((b, n, d), F32),
        compiler_params=_params("parallel", "parallel"),
        name="final_norm",
    )(x, g.reshape(1, d))


def _rowmm_kernel(*refs, n_in, has_res):
    a_refs, w_refs = refs[:n_in], refs[n_in:2 * n_in]
    o_ref = refs[-1]
    acc = None
    for a, w in zip(a_refs, w_refs):
        d = jnp.dot(a[0], w[...], preferred_element_type=F32)
        acc = d if acc is None else acc + d
    if has_res:
        res_ref, gate_ref = refs[2 * n_in], refs[2 * n_in + 1]
        acc = res_ref[0] + gate_ref[0] * acc
    o_ref[0] = acc.astype(o_ref.dtype)


def _rowmm(a_list, w_list, out_dtype, res=None, gate=None, tm=512):
    b, n, _ = a_list[0].shape
    nout = w_list[0].shape[1]
    tm = min(tm, n)
    in_specs = [pl.BlockSpec((1, tm, a.shape[2]), lambda bi, i: (bi, i, 0)) for a in a_list]
    in_specs += [_resident(w.shape) for w in w_list]
    args = list(a_list) + list(w_list)
    if res is not None:
        in_specs += [pl.BlockSpec((1, tm, nout), lambda bi, i: (bi, i, 0)),
                     pl.BlockSpec((1, 1, nout), lambda bi, i: (bi, 0, 0))]
        args += [res, gate]
    return pl.pallas_call(
        functools.partial(_rowmm_kernel, n_in=len(a_list), has_res=res is not None),
        grid=(b, n // tm),
        in_specs=in_specs,
        out_specs=pl.BlockSpec((1, tm, nout), lambda bi, i: (bi, i, 0)),
        out_shape=jax.ShapeDtypeStruct((b, n, nout), out_dtype),
        compiler_params=_params("parallel", "parallel"),
        name="rowmm",
    )(*args)


def _rope_cols(t, c_ref, s_ref, rope):
    if not rope:
        return t
    return t * c_ref[...] + pltpu.roll(t, QK_ROPE, 1) * s_ref[...]


def _qproj_kernel(p_ref, g_ref, w_ref, c_ref, s_ref, o_ref, *, rope):
    z = _rms(p_ref[0], g_ref[...]).astype(BF16)
    for h in range(N_HEADS):
        acc = jnp.dot(z, w_ref[h], preferred_element_type=F32)
        o_ref[0, h, :, :QK_NOPE] = (acc[:, :QK_NOPE] * ATTN_SCALE).astype(o_ref.dtype)
        r = _rope_cols(acc[:, QK_NOPE:], c_ref, s_ref, rope)
        o_ref[0, h, :, QK_NOPE:] = (r[:, :QK_ROPE] * ATTN_SCALE).astype(o_ref.dtype)


def _qproj(p, q_norm, wq, cos2, sin2, rope):
    b, n, _ = p.shape
    tm = min(512, n)
    tab = pl.BlockSpec((tm, 2 * QK_ROPE), lambda bi, i: (i, 0))
    return pl.pallas_call(
        functools.partial(_qproj_kernel, rope=rope),
        grid=(b, n // tm),
        in_specs=[pl.BlockSpec((1, tm, Q_RANK), lambda bi, i: (bi, i, P_QLAT // Q_RANK)),
                  pl.BlockSpec((1, Q_RANK), lambda bi, i: (0, 0)),
                  _resident(wq.shape), tab, tab],
        out_specs=pl.BlockSpec((1, N_HEADS, tm, QK_HEAD), lambda bi, i: (bi, 0, i, 0)),
        out_shape=jax.ShapeDtypeStruct((b, N_HEADS, n, QK_HEAD), BF16),
        compiler_params=_params("parallel", "parallel"),
        name="qproj",
    )(p, q_norm.reshape(1, Q_RANK), wq, cos2, sin2)


def _kvproj_kernel(p_ref, g_ref, w_ref, c_ref, s_ref, *rest, rope, aliased):
    k_ref, v_ref = rest[-2], rest[-1]
    blk = p_ref[0]
    z = _rms(blk[:, :KV_RANK], g_ref[...]).astype(BF16)
    kr = _rope_cols(blk[:, KV_RANK:], c_ref, s_ref, rope)[:, :QK_ROPE].astype(k_ref.dtype)
    for h in range(N_HEADS):
        acc = jnp.dot(z, w_ref[h], preferred_element_type=F32)
        k_ref[0, h, :, :QK_NOPE] = acc[:, :QK_NOPE].astype(k_ref.dtype)
        k_ref[0, h, :, QK_NOPE:] = kr
        v_ref[0, h] = acc[:, QK_NOPE:].astype(v_ref.dtype)


def _kvproj(p, kv_norm, wkv, cos2, sin2, rope, n_keys, row_block0, k_all=None, v_all=None):
    b, n, _ = p.shape
    tm = min(512, n)
    tab = pl.BlockSpec((tm, 2 * QK_ROPE), lambda bi, i: (i, 0))
    in_specs = [pl.BlockSpec((1, tm, KV_BLOCK), lambda bi, i: (bi, i, P_KV // KV_BLOCK)),
                pl.BlockSpec((1, KV_RANK), lambda bi, i: (0, 0)),
                _resident(wkv.shape), tab, tab]
    args = [p, kv_norm.reshape(1, KV_RANK), wkv, cos2, sin2]
    aliases = {}
    if k_all is not None:
        in_specs += [pl.BlockSpec(memory_space=pl.ANY), pl.BlockSpec(memory_space=pl.ANY)]
        args += [k_all, v_all]
        aliases = {5: 0, 6: 1}
    return pl.pallas_call(
        functools.partial(_kvproj_kernel, rope=rope, aliased=k_all is not None),
        grid=(b, n // tm),
        in_specs=in_specs,
        out_specs=[pl.BlockSpec((1, N_HEADS, tm, QK_HEAD), lambda bi, i: (bi, 0, row_block0 + i, 0)),
                   pl.BlockSpec((1, N_HEADS, tm, V_HEAD), lambda bi, i: (bi, 0, row_block0 + i, 0))],
        out_shape=[jax.ShapeDtypeStruct((b, N_HEADS, n_keys, QK_HEAD), BF16),
                   jax.ShapeDtypeStruct((b, N_HEADS, n_keys, V_HEAD), BF16)],
        input_output_aliases=aliases,
        compiler_params=_params("parallel", "parallel"),
        name="kvproj",
    )(*args)


def _attn_kernel(q_ref, k_ref, v_ref, o_ref, m_sc, l_sc, acc_sc):
    j = pl.program_id(3)

    @pl.when(j == 0)
    def _():
        m_sc[...] = jnp.full(m_sc.shape, -jnp.inf, F32)
        l_sc[...] = jnp.zeros(l_sc.shape, F32)
        acc_sc[...] = jnp.zeros(acc_sc.shape, F32)

    s = lax.dot_general(q_ref[0, 0], k_ref[0, 0], (((1,), (1,)), ((), ())), preferred_element_type=F32)
    m_prev = m_sc[...]
    m_new = jnp.maximum(m_prev, jnp.max(s, axis=-1, keepdims=True))
    alpha = jnp.exp(m_prev - m_new)
    p = jnp.exp(s - m_new)
    l_sc[...] = alpha * l_sc[...] + jnp.sum(p, axis=-1, keepdims=True)
    acc_sc[...] = alpha * acc_sc[...] + jnp.dot(p.astype(v_ref.dtype), v_ref[0, 0], preferred_element_type=F32)
    m_sc[...] = m_new

    @pl.when(j == pl.num_programs(3) - 1)
    def _():
        o_ref[0] = (acc_sc[...] / l_sc[...]).astype(o_ref.dtype)


def _attention(q, k_all, v_all, tk, kv_block0, n_kv_blocks):
    b, h, n, _ = q.shape
    tq = min(512, n)
    return pl.pallas_call(
        _attn_kernel,
        grid=(b, h, n // tq, n_kv_blocks),
        in_specs=[pl.BlockSpec((1, 1, tq, QK_HEAD), lambda bi, hi, i, j: (bi, hi, i, 0)),
                  pl.BlockSpec((1, 1, tk, QK_HEAD), lambda bi, hi, i, j: (bi, hi, kv_block0 + j, 0)),
                  pl.BlockSpec((1, 1, tk, V_HEAD), lambda bi, hi, i, j: (bi, hi, kv_block0 + j, 0))],
        out_specs=pl.BlockSpec((1, tq, V_HEAD), lambda bi, hi, i, j: (bi, i, hi)),
        out_shape=jax.ShapeDtypeStruct((b, n, h * V_HEAD), BF16),
        scratch_shapes=[pltpu.VMEM((tq, 1), F32), pltpu.VMEM((tq, 1), F32), pltpu.VMEM((tq, V_HEAD), F32)],
        compiler_params=_params("parallel", "parallel", "parallel", "arbitrary"),
        name="attention",
    )(q, k_all, v_all)


def _pool_kernel(u_ref, prev_ref, next_ref, w_ref, s_ref, o_ref, ext_sc, *, n, tm):
    i = pl.program_id(1)
    keep_prev = jnp.where(i > 0, 1.0, 0.0)
    keep_next = jnp.where(i < pl.num_programs(1) - 1, 1.0, 0.0)
    ext_sc[0:POOL_HALO] = prev_ref[0] * keep_prev
    ext_sc[POOL_HALO:POOL_HALO + tm] = u_ref[0]
    ext_sc[POOL_HALO + tm:] = next_ref[0] * keep_next
    t = i * tm + lax.broadcasted_iota(jnp.int32, (tm, 1), 0)
    for g, win in enumerate(POOL_WINDOWS):
        half = win // 2
        cols = slice(g * POOL_GROUP, (g + 1) * POOL_GROUP)
        tot = ext_sc[POOL_HALO - half:POOL_HALO - half + tm, cols]
        for d in range(1 - half, half):
            tot = tot + ext_sc[POOL_HALO + d:POOL_HALO + d + tm, cols]
        cnt = (jnp.minimum(t + half, n) - jnp.maximum(t - half, 0)).astype(F32)
        pooled = (tot / cnt - ext_sc[POOL_HALO:POOL_HALO + tm, cols]).astype(BF16)
        y = jnp.dot(pooled, w_ref[g], preferred_element_type=F32) * s_ref[:, cols]
        o_ref[0, :, cols] = y.astype(o_ref.dtype)


def _pool(p, w_pool, pool_scale):
    b, n, _ = p.shape
    tm = min(512, n)
    hb = tm // POOL_HALO
    last_hb = n // POOL_HALO - 1
    return pl.pallas_call(
        functools.partial(_pool_kernel, n=n, tm=tm),
        grid=(b, n // tm),
        in_specs=[pl.BlockSpec((1, tm, POOL_WIDTH), lambda bi, i: (bi, i, P_POOL)),
                  pl.BlockSpec((1, POOL_HALO, POOL_WIDTH), lambda bi, i: (bi, jnp.maximum(i * hb - 1, 0), P_POOL)),
                  pl.BlockSpec((1, POOL_HALO, POOL_WIDTH),
                               lambda bi, i: (bi, jnp.minimum((i + 1) * hb, last_hb), P_POOL)),
                  _resident(w_pool.shape),
                  pl.BlockSpec((1, POOL_WIDTH), lambda bi, i: (0, 0))],
        out_specs=pl.BlockSpec((1, tm, POOL_WIDTH), lambda bi, i: (bi, i, 0)),
        out_shape=jax.ShapeDtypeStruct((b, n, POOL_WIDTH), BF16),
        scratch_shapes=[pltpu.VMEM((tm + 2 * POOL_HALO, POOL_WIDTH), F32)],
        compiler_params=_params("parallel", "parallel"),
        name="pool",
    )(p, p, p, w_pool, pool_scale.reshape(1, POOL_WIDTH))


def _dft1_kernel(x_ref, a_ref, o_ref):
    tc = x_ref.shape[-1]
    x = x_ref[0].reshape(DFT_N1 * DFT_ROWS, tc)
    y = jnp.dot(a_ref[...], x, preferred_element_type=F32)
    o_ref[0] = y.astype(o_ref.dtype).reshape(2, DFT_N1, DFT_ROWS, tc)


def _dft_stage1(h, a_kron):
    b, n, d = h.shape
    tc = 1024
    h4 = h.reshape(b, DFT_N1, DFT_N2, d)
    return pl.pallas_call(
        _dft1_kernel,
        grid=(b, DFT_N2 // DFT_ROWS, d // tc),
        in_specs=[pl.BlockSpec((1, DFT_N1, DFT_ROWS, tc), lambda bi, r, c: (bi, 0, r, c)),
                  _resident(a_kron.shape)],
        out_specs=pl.BlockSpec((1, 2, DFT_N1, DFT_ROWS, tc), lambda bi, r, c: (bi, 0, 0, r, c)),
        out_shape=jax.ShapeDtypeStruct((b, 2, DFT_N1, DFT_N2, d), BF16),
        compiler_params=_params("parallel", "parallel", "parallel"),
        name="dft_stage1",
    )(h4, a_kron)


def _dft2_kernel(y_ref, m_ref, cc_ref, sc_ref, o_ref, *, rows_out, seq_len):
    y = y_ref[...].reshape(-1, D_MODEL)
    pq = jnp.dot(m_ref[0], y, preferred_element_type=F32).astype(BF16)
    ortho = 1.0 / math.sqrt(seq_len * FOURIER_GROUP)
    for g in range(FOURIER_GROUPS):
        cols = slice(g * FOURIER_GROUP, (g + 1) * FOURIER_GROUP)
        f = (jnp.dot(pq[:rows_out, cols], cc_ref[...], preferred_element_type=F32)
             - jnp.dot(pq[rows_out:, cols], sc_ref[...], preferred_element_type=F32))
        o_ref[0, :, cols] = (f * ortho).astype(o_ref.dtype)


def _dft_stage2(y, m, cc, sc):
    b = y.shape[0]
    out = pl.pallas_call(
        functools.partial(_dft2_kernel, rows_out=DFT_N2, seq_len=DFT_N1 * DFT_N2),
        grid=(b, DFT_N1),
        in_specs=[pl.BlockSpec((1, 2, 1, DFT_N2, D_MODEL), lambda bi, k: (bi, 0, k, 0, 0)),
                  pl.BlockSpec((1, 2 * DFT_N2, 2 * DFT_N2), lambda bi, k: (k, 0, 0)),
                  _resident(cc.shape), _resident(sc.shape)],
        out_specs=pl.BlockSpec((1, DFT_N2, D_MODEL), lambda bi, k: (bi, 0, k)),
        out_shape=jax.ShapeDtypeStruct((b, DFT_N2, DFT_N1 * D_MODEL), BF16),
        compiler_params=_params("parallel", "parallel"),
        name="dft_stage2",
    )(y, m, cc, sc)
    return out.reshape(b, DFT_N1 * DFT_N2, D_MODEL)


def _dft_small(h, m, cc, sc):
    b, n, d = h.shape
    return pl.pallas_call(
        functools.partial(_dft2_kernel, rows_out=n, seq_len=n),
        grid=(b,),
        in_specs=[pl.BlockSpec((1, n, d), lambda bi: (bi, 0, 0)),
                  _resident(m.shape), _resident(cc.shape), _resident(sc.shape)],
        out_specs=pl.BlockSpec((1, n, d), lambda bi: (bi, 0, 0)),
        out_shape=jax.ShapeDtypeStruct((b, n, d), BF16),
        compiler_params=_params("parallel"),
        name="dft_small",
    )(h, m, cc, sc)


def _cos_sin(num, den):
    ang = (num % den).astype(F32) * (2.0 * math.pi / den)
    return jnp.cos(ang), jnp.sin(ang)


def _dft_tables(n_ctx):
    i1 = jnp.arange(DFT_N1, dtype=jnp.int32)
    c1, s1 = _cos_sin(i1[:, None] * i1[None, :], DFT_N1)
    a = jnp.concatenate([c1, -s1], axis=0)
    eye = jnp.eye(DFT_ROWS, dtype=F32)
    a_kron = jnp.einsum('pt,jk->pjtk', a, eye).reshape(2 * DFT_N1 * DFT_ROWS, DFT_N1 * DFT_ROWS)
    n = DFT_N1 * DFT_N2
    t2 = jnp.arange(DFT_N2, dtype=jnp.int32)
    k = i1[:, None, None] + DFT_N1 * t2[None, :, None]
    ck, sk = _cos_sin(k * t2[None, None, :], n)
    m = jnp.concatenate([jnp.concatenate([ck, sk], axis=2),
                         jnp.concatenate([sk, -ck], axis=2)], axis=1)
    ic = jnp.arange(FOURIER_GROUP, dtype=jnp.int32)
    cc, sc = _cos_sin(ic[:, None] * ic[None, :], FOURIER_GROUP)
    il = jnp.arange(n_ctx, dtype=jnp.int32)
    cl, sl = _cos_sin(il[:, None] * il[None, :], n_ctx)
    m_ctx = jnp.concatenate([cl, sl], axis=0)[None]

    return dict(a_kron=a_kron.astype(BF16), m=m.astype(BF16), m_ctx=m_ctx.astype(BF16),
                cc=cc.astype(BF16), sc=sc.astype(BF16))


def _mlp_kernel(x_ref, g_ref, sh_ref, sc_ref, gate_ref, w1_ref, w2_ref, o_ref, h_sc, acc_sc):
    j = pl.program_id(2)

    @pl.when(j == 0)
    def _():
        y = _rms(x_ref[0], g_ref[...])
        h_sc[...] = (y * (1.0 + sc_ref[0]) + sh_ref[0]).astype(h_sc.dtype)
        acc_sc[...] = jnp.zeros(acc_sc.shape, F32)

    a = jnp.maximum(jnp.dot(h_sc[...], w1_ref[...], preferred_element_type=F32), 0.0)
    acc_sc[...] += jnp.dot((a * a).astype(BF16), w2_ref[...], preferred_element_type=F32)

    @pl.when(j == pl.num_programs(2) - 1)
    def _():
        o_ref[0] = x_ref[0] + gate_ref[0] * acc_sc[...]


def _mlp(x, g, sh, sc, gate, w1, w2):
    b, n, d = x.shape
    tm = min(512, n)
    tf = 512
    vec = pl.BlockSpec((1, 1, d), lambda bi, i, j: (bi, 0, 0))
    return pl.pallas_call(
        _mlp_kernel,
        grid=(b, n // tm, D_FF // tf),
        in_specs=[pl.BlockSpec((1, tm, d), lambda bi, i, j: (bi, i, 0)),
                  pl.BlockSpec((1, d), lambda bi, i, j: (0, 0)), vec, vec, vec,
                  pl.BlockSpec((d, tf), lambda bi, i, j: (0, j)),
                  pl.BlockSpec((tf, d), lambda bi, i, j: (j, 0))],
        out_specs=pl.BlockSpec((1, tm, d), lambda bi, i, j: (bi, i, 0)),
        out_shape=jax.ShapeDtypeStruct((b, n, d), F32),
        scratch_shapes=[pltpu.VMEM((tm, d), BF16), pltpu.VMEM((tm, d), F32)],
        compiler_params=_params("parallel", "parallel", "arbitrary"),
        name="mlp",
    )(x, g.reshape(1, d), sh, sc, gate, w1, w2)


def _rotate_half_axial(x):
    xr = x.reshape(x.shape[:-1] + (2, 2, ROPE_FREQS))
    return jnp.concatenate([-xr[..., 1:, :], xr[..., :1, :]], axis=-2).reshape(x.shape)


def _rope_tables(n):
    rows = n // GRID_W
    r = jnp.broadcast_to(jnp.arange(rows, dtype=F32)[:, None], (rows, GRID_W)).reshape(n)
    col = jnp.broadcast_to(jnp.arange(GRID_W, dtype=F32)[None, :], (rows, GRID_W)).reshape(n)
    inv = ROPE_THETA ** (-2.0 * jnp.arange(ROPE_FREQS, dtype=F32) / ROPE_AXIS)
    ang = jnp.stack([r[:, None] * inv, col[:, None] * inv], axis=1)
    ang = jnp.broadcast_to(ang[:, :, None, :], (n, 2, 2, ROPE_FREQS)).reshape(n, QK_ROPE)
    zeros = jnp.zeros((n, QK_ROPE), F32)
    return (jnp.concatenate([jnp.cos(ang), zeros], axis=1), jnp.concatenate([jnp.sin(ang), zeros], axis=1))


def _even_weights(w_in, w_uq, w_ukv):
    w_kr = w_in[:, Q_RANK + KV_RANK:Q_RANK + KV_RANK + QK_ROPE]
    w_in2 = jnp.concatenate([w_in[:, Q_RANK + KV_RANK + QK_ROPE:], w_in[:, :Q_RANK],
                             w_in[:, Q_RANK:Q_RANK + KV_RANK], w_kr, _rotate_half_axial(w_kr)], axis=1)
    wq = w_uq.reshape(Q_RANK, N_HEADS, QK_HEAD)
    wq = jnp.concatenate([wq, _rotate_half_axial(wq[..., QK_NOPE:])], axis=-1)
    wq = jnp.transpose(wq, (1, 0, 2))
    wkv = jnp.transpose(w_ukv.reshape(KV_RANK, N_HEADS, QK_NOPE + V_HEAD), (1, 0, 2))
    return w_in2.astype(BF16), wq.astype(BF16), wkv.astype(BF16)


def kernel(x, c, ctx, c_ctx, w_mod, b_mod, norm1, norm2, w_in, q_norm, w_uq, kv_norm, w_ukv, w_pool,
           pool_scale, w_out_even, w_out_odd, w_mlp1, w_mlp2, final_norm):
    b, n, d = x.shape
    n_ctx = ctx.shape[1]
    assert n == DFT_N1 * DFT_N2 and d == D_MODEL and n % GRID_W == 0
    n_keys = n + n_ctx
    tk = 768
    assert n_keys % tk == 0 and n % 512 == 0 and n_ctx == 256

    cos2, sin2 = _rope_tables(n)
    dft = _dft_tables(n_ctx)
    w1 = w_mlp1.astype(BF16)
    w2 = w_mlp2.astype(BF16)

    cc = jnp.concatenate([c, c_ctx[None], jnp.zeros((8 - b - 1, d), F32)], axis=0)
    mods = _modulation(cc, w_mod, b_mod)

    updates_ctx = [False] * DEPTH
    for l in reversed(range(DEPTH - 1)):
        updates_ctx[l] = (l + 1) % 2 == 0 or updates_ctx[l + 1]

    for l in range(DEPTH):
        even = l % 2 == 0
        i = l // 2
        mx = [mods[l, :b, k * d:(k + 1) * d].reshape(b, 1, d) for k in range(6)]
        mc = [jnp.broadcast_to(mods[l, b, k * d:(k + 1) * d].reshape(1, 1, d), (b, 1, d)) for k in range(6)]
        ctx_update = updates_ctx[l]

        hx = _normmod(x, norm1[l], mx[0], mx[1])
        if even or ctx_update:
            hc = _normmod(ctx, norm1[l], mc[0], mc[1])
        if even:
            w_in2, wq, wkv = _even_weights(w_in[i], w_uq[i], w_ukv[i])
            w_out = w_out_even[i].astype(BF16)
            wp = w_pool[i].astype(BF16)
            px = _rowmm([hx], [w_in2], F32)
            pc = _rowmm([hc], [w_in2], F32)
            q = _qproj(px, q_norm[i], wq, cos2, sin2, rope=True)
            k_all, v_all = _kvproj(px, kv_norm[i], wkv, cos2, sin2, True, n_keys, 0)
            k_all, v_all = _kvproj(pc, kv_norm[i], wkv, cos2, sin2, False, n_keys, n // n_ctx, k_all, v_all)
            attn_x = _attention(q, k_all, v_all, tk, 0, n_keys // tk)
            pool_x = _pool(px, wp, pool_scale[i])
            x = _rowmm([attn_x, pool_x], [w_out[:MLA_OUT], w_out[MLA_OUT:]], F32, res=x, gate=mx[2])
            if ctx_update:
                cq = _qproj(pc, q_norm[i], wq, cos2, sin2, rope=False)
                attn_c = _attention(cq, k_all, v_all, n_ctx, n // n_ctx, 1)
                pool_c = _pool(pc, wp, pool_scale[i])
                ctx = _rowmm([attn_c, pool_c], [w_out[:MLA_OUT], w_out[MLA_OUT:]], F32, res=ctx, gate=mc[2])
        else:
            w_out = w_out_odd[i].astype(BF16)
            fx = _dft_stage2(_dft_stage1(hx, dft['a_kron']), dft['m'], dft['cc'], dft['sc'])
            x = _rowmm([fx], [w_out], F32, res=x, gate=mx[2])
            if ctx_update:
                fc = _dft_small(hc, dft['m_ctx'], dft['cc'], dft['sc'])
                ctx = _rowmm([fc], [w_out], F32, res=ctx, gate=mc[2])
        x = _mlp(x, norm2[l], mx[3], mx[4], mx[5], w1[l], w2[l])
        if ctx_update:
            ctx = _mlp(ctx, norm2[l], mc[3], mc[4], mc[5], w1[l], w2[l])
    return _final_norm(x, final_norm)
```

```python
import functools
import math

import jax
import jax.numpy as jnp
import numpy as np
from jax import lax
from jax.experimental import pallas as pl
from jax.experimental.pallas import tpu as pltpu

D_MODEL = 2048
DEPTH = 4
GRID_W = 64
N_HEADS = 8
Q_RANK = 512
KV_RANK = 256
QK_NOPE = 128
QK_ROPE = 64
V_HEAD = 128
QK_HEAD = QK_NOPE + QK_ROPE
ATTN_SCALE = QK_HEAD ** -0.5
Q_SCALE = ATTN_SCALE * math.log2(math.e)
MLA_OUT = N_HEADS * V_HEAD
ROPE_AXIS = QK_ROPE // 2
ROPE_FREQS = ROPE_AXIS // 2
ROPE_THETA = 10000.0
POOL_WIDTH = D_MODEL // 2
POOL_WINDOWS = (2, 4, 8, 16)
POOL_GROUP = POOL_WIDTH // len(POOL_WINDOWS)
POOL_HALO = 8
FOURIER_GROUPS = 4
FOURIER_GROUP = D_MODEL // FOURIER_GROUPS
D_FF = 4 * D_MODEL
EPS = 1e-6

P_POOL = 0
P_QLAT = POOL_WIDTH
P_KV = POOL_WIDTH + Q_RANK
P_WIDTH = POOL_WIDTH + Q_RANK + KV_RANK + 2 * QK_ROPE
KV_BLOCK = KV_RANK + 2 * QK_ROPE

DFT_N1 = 64
DFT_N2 = 128
DFT_ROWS = 16

V7X_VMEM_LIMIT = 56 * 1024 * 1024

F32 = jnp.float32
BF16 = jnp.bfloat16


def _params(*sem):
    return pltpu.CompilerParams(dimension_semantics=sem, vmem_limit_bytes=V7X_VMEM_LIMIT)


def _resident(shape):
    nd = len(shape)
    return pl.BlockSpec(shape, lambda *_: (0,) * nd, pipeline_mode=pl.Buffered(1))


def _rms(x, g):
    return x * lax.rsqrt(jnp.mean(x * x, axis=-1, keepdims=True) + EPS) * g


def _mod_kernel(c_ref, w_ref, b_ref, o_ref):
    c = c_ref[...]
    s = c * (1.0 / (1.0 + jnp.exp(-c)))
    o_ref[0] = jnp.dot(s, w_ref[0], preferred_element_type=F32, precision=lax.Precision.HIGHEST) + b_ref[0]


def _modulation(cc, w_mod, b_mod):
    rows = cc.shape[0]
    tn = 1536
    n6 = 6 * D_MODEL
    return pl.pallas_call(
        _mod_kernel,
        grid=(DEPTH, n6 // tn),
        in_specs=[
            pl.BlockSpec((rows, D_MODEL), lambda l, j: (0, 0)),
            pl.BlockSpec((1, D_MODEL, tn), lambda l, j: (l, 0, j)),
            pl.BlockSpec((1, 1, tn), lambda l, j: (l, 0, j)),
        ],
        out_specs=pl.BlockSpec((1, rows, tn), lambda l, j: (l, 0, j)),
        out_shape=jax.ShapeDtypeStruct((DEPTH, rows, n6), F32),
        compiler_params=_params("parallel", "parallel"),
        name="modulation",
    )(cc, w_mod, b_mod.reshape(DEPTH, 1, n6))


def _normmod_kernel(x_ref, g_ref, sh_ref, sc_ref, o_ref):
    y = _rms(x_ref[0], g_ref[...])
    o_ref[0] = (y * (1.0 + sc_ref[0]) + sh_ref[0]).astype(o_ref.dtype)


def _normmod(x, g, sh, sc, out_dtype=BF16):
    b, n, d = x.shape
    tm = min(512, n)
    vec = pl.BlockSpec((1, 1, d), lambda bi, i: (bi, 0, 0))
    return pl.pallas_call(
        _normmod_kernel,
        grid=(b, n // tm),
        in_specs=[pl.BlockSpec((1, tm, d), lambda bi, i: (bi, i, 0)),
                  pl.BlockSpec((1, d), lambda bi, i: (0, 0)), vec, vec],
        out_specs=pl.BlockSpec((1, tm, d), lambda bi, i: (bi, i, 0)),
        out_shape=jax.ShapeDtypeStruct((b, n, d), out_dtype),
        compiler_params=_params("parallel", "parallel"),
        name="normmod",
    )(x, g.reshape(1, d), sh, sc)


def _final_norm_kernel(x_ref, g_ref, o_ref):
    o_ref[0] = _rms(x_ref[0], g_ref[...])


def _final_norm(x, g):
    b, n, d = x.shape
    tm = 512
    return pl.pallas_call(
        _final_norm_kernel,
        grid=(b, n // tm),
        in_specs=[pl.BlockSpec((1, tm, d), lambda bi, i: (bi, i, 0)),
                  pl.BlockSpec((1, d), lambda bi, i: (0, 0))],
        out_specs=pl.BlockSpec((1, tm, d), lambda bi, i: (bi, i, 0)),
        out_shape=jax.ShapeDtypeStruct((b, n, d), F32),
        compiler_params=_params("parallel", "parallel"),
        name="final_norm",
    )(x, g.reshape(1, d))


def _rowmm_kernel(*refs, n_in, has_res):
    a_refs, w_refs = refs[:n_in], refs[n_in:2 * n_in]
    o_ref = refs[-1]
    acc = None
    for a, w in zip(a_refs, w_refs):
        d = jnp.dot(a[0], w[...], preferred_element_type=F32)
        acc = d if acc is None else acc + d
    if has_res:
        res_ref, gate_ref = refs[2 * n_in], refs[2 * n_in + 1]
        acc = res_ref[0] + gate_ref[0] * acc
    o_ref[0] = acc.astype(o_ref.dtype)


def _rowmm(a_list, w_list, out_dtype, res=None, gate=None, tm=512):
    b, n, _ = a_list[0].shape
    nout = w_list[0].shape[1]
    tm = min(tm, n)
    in_specs = [pl.BlockSpec((1, tm, a.shape[2]), lambda bi, i: (bi, i, 0)) for a in a_list]
    in_specs += [_resident(w.shape) for w in w_list]
    args = list(a_list) + list(w_list)
    if res is not None:
        in_specs += [pl.BlockSpec((1, tm, nout), lambda bi, i: (bi, i, 0)),
                     pl.BlockSpec((1, 1, nout), lambda bi, i: (bi, 0, 0))]
        args += [res, gate]
    return pl.pallas_call(
        functools.partial(_rowmm_kernel, n_in=len(a_list), has_res=res is not None),
        grid=(b, n // tm),
        in_specs=in_specs,
        out_specs=pl.BlockSpec((1, tm, nout), lambda bi, i: (bi, i, 0)),
        out_shape=jax.ShapeDtypeStruct((b, n, nout), out_dtype),
        compiler_params=_params("parallel", "parallel"),
        name="rowmm",
    )(*args)


def _rope_cols(t, c_ref, s_ref, rope):
    if not rope:
        return t
    return t * c_ref[...] + pltpu.roll(t, QK_ROPE, 1) * s_ref[...]


def _qproj_kernel(p_ref, g_ref, w_ref, c_ref, s_ref, o_ref, *, rope):
    z = _rms(p_ref[0], g_ref[...]).astype(BF16)
    for h in range(N_HEADS):
        acc = jnp.dot(z, w_ref[h], preferred_element_type=F32)
        o_ref[0, h, :, :QK_NOPE] = (acc[:, :QK_NOPE] * Q_SCALE).astype(o_ref.dtype)
        r = _rope_cols(acc[:, QK_NOPE:], c_ref, s_ref, rope)
        o_ref[0, h, :, QK_NOPE:] = (r[:, :QK_ROPE] * Q_SCALE).astype(o_ref.dtype)


def _qproj(p, q_norm, wq, cos2, sin2, rope):
    b, n, _ = p.shape
    tm = min(512, n)
    tab = pl.BlockSpec((tm, 2 * QK_ROPE), lambda bi, i: (i, 0))
    return pl.pallas_call(
        functools.partial(_qproj_kernel, rope=rope),
        grid=(b, n // tm),
        in_specs=[pl.BlockSpec((1, tm, Q_RANK), lambda bi, i: (bi, i, P_QLAT // Q_RANK)),
                  pl.BlockSpec((1, Q_RANK), lambda bi, i: (0, 0)),
                  _resident(wq.shape), tab, tab],
        out_specs=pl.BlockSpec((1, N_HEADS, tm, QK_HEAD), lambda bi, i: (bi, 0, i, 0)),
        out_shape=jax.ShapeDtypeStruct((b, N_HEADS, n, QK_HEAD), BF16),
        compiler_params=_params("parallel", "parallel"),
        name="qproj",
    )(p, q_norm.reshape(1, Q_RANK), wq, cos2, sin2)


def _kvproj_kernel(p_ref, g_ref, wk_ref, wvt_ref, c_ref, s_ref, *rest, rope):
    k_ref, vt_ref = rest[-2], rest[-1]
    blk = p_ref[0]
    z = _rms(blk[:, :KV_RANK], g_ref[...]).astype(BF16)
    kr = _rope_cols(blk[:, KV_RANK:], c_ref, s_ref, rope)[:, :QK_ROPE].astype(k_ref.dtype)
    for h in range(N_HEADS):
        k_ref[0, h, :, :QK_NOPE] = jnp.dot(z, wk_ref[h], preferred_element_type=F32).astype(k_ref.dtype)
        k_ref[0, h, :, QK_NOPE:] = kr
        vt = lax.dot_general(wvt_ref[h], z, (((1,), (1,)), ((), ())), preferred_element_type=F32)
        vt_ref[0, h] = vt.astype(vt_ref.dtype)


def _kvproj(p, kv_norm, wk, wvt, cos2, sin2, rope, n_keys, row_block0, k_all=None, vt_all=None):
    b, n, _ = p.shape
    tm = min(512, n)
    tab = pl.BlockSpec((tm, 2 * QK_ROPE), lambda bi, i: (i, 0))
    in_specs = [pl.BlockSpec((1, tm, KV_BLOCK), lambda bi, i: (bi, i, P_KV // KV_BLOCK)),
                pl.BlockSpec((1, KV_RANK), lambda bi, i: (0, 0)),
                _resident(wk.shape), _resident(wvt.shape), tab, tab]
    args = [p, kv_norm.reshape(1, KV_RANK), wk, wvt, cos2, sin2]
    aliases = {}
    if k_all is not None:
        aliases = {len(args): 0, len(args) + 1: 1}
        in_specs += [pl.BlockSpec(memory_space=pl.ANY), pl.BlockSpec(memory_space=pl.ANY)]
        args += [k_all, vt_all]
    return pl.pallas_call(
        functools.partial(_kvproj_kernel, rope=rope),
        grid=(b, n // tm),
        in_specs=in_specs,
        out_specs=[pl.BlockSpec((1, N_HEADS, tm, QK_HEAD), lambda bi, i: (bi, 0, row_block0 + i, 0)),
                   pl.BlockSpec((1, N_HEADS, V_HEAD, tm), lambda bi, i: (bi, 0, 0, row_block0 + i))],
        out_shape=[jax.ShapeDtypeStruct((b, N_HEADS, n_keys, QK_HEAD), BF16),
                   jax.ShapeDtypeStruct((b, N_HEADS, V_HEAD, n_keys), BF16)],
        input_output_aliases=aliases,
        compiler_params=_params("parallel", "parallel"),
        name="kvproj",
    )(*args)


def _attn_kernel(q_ref, k_ref, vt_ref, o_ref, *, tk):
    q = q_ref[0, 0]
    tq = q.shape[0]
    m = jnp.full((1, tq), -jnp.inf, F32)
    l = jnp.zeros((1, tq), F32)
    acc = jnp.zeros((V_HEAD, tq), F32)
    for j in range(k_ref.shape[2] // tk):
        keys = slice(j * tk, (j + 1) * tk)
        st = lax.dot_general(k_ref[0, 0, keys, :], q, (((1,), (1,)), ((), ())), preferred_element_type=F32)
        m_new = jnp.maximum(m, jnp.max(st, axis=0, keepdims=True))
        alpha = jnp.exp2(m - m_new)
        p = jnp.exp2(st - m_new)
        l = alpha * l + jnp.sum(p, axis=0, keepdims=True)
        acc = alpha * acc + jnp.dot(vt_ref[0, 0, :, keys], p.astype(BF16), preferred_element_type=F32)
        m = m_new
    o_ref[0] = jnp.transpose(acc / l).astype(o_ref.dtype)


def _attention(q, k_all, vt_all, kv_len, kv_block0, tk):
    b, h, n, _ = q.shape
    tq = min(512, n)
    return pl.pallas_call(
        functools.partial(_attn_kernel, tk=tk),
        grid=(b, h, n // tq),
        in_specs=[pl.BlockSpec((1, 1, tq, QK_HEAD), lambda bi, hi, i: (bi, hi, i, 0)),
                  pl.BlockSpec((1, 1, kv_len, QK_HEAD), lambda bi, hi, i: (bi, hi, kv_block0, 0)),
                  pl.BlockSpec((1, 1, V_HEAD, kv_len), lambda bi, hi, i: (bi, hi, 0, kv_block0))],
        out_specs=pl.BlockSpec((1, tq, V_HEAD), lambda bi, hi, i: (bi, i, hi)),
        out_shape=jax.ShapeDtypeStruct((b, n, h * V_HEAD), BF16),
        compiler_params=_params("parallel", "parallel", "parallel"),
        name="attention",
    )(q, k_all, vt_all)


def _pool_kernel(u_ref, prev_ref, next_ref, w_ref, s_ref, o_ref, ext_sc, *, n, tm):
    i = pl.program_id(1)
    keep_prev = jnp.where(i > 0, 1.0, 0.0)
    keep_next = jnp.where(i < pl.num_programs(1) - 1, 1.0, 0.0)
    ext_sc[0:POOL_HALO] = prev_ref[0] * keep_prev
    ext_sc[POOL_HALO:POOL_HALO + tm] = u_ref[0]
    ext_sc[POOL_HALO + tm:] = next_ref[0] * keep_next
    t = i * tm + lax.broadcasted_iota(jnp.int32, (tm, 1), 0)
    for g, win in enumerate(POOL_WINDOWS):
        half = win // 2
        cols = slice(g * POOL_GROUP, (g + 1) * POOL_GROUP)
        tot = ext_sc[POOL_HALO - half:POOL_HALO - half + tm, cols]
        for d in range(1 - half, half):
            tot = tot + ext_sc[POOL_HALO + d:POOL_HALO + d + tm, cols]
        cnt = (jnp.minimum(t + half, n) - jnp.maximum(t - half, 0)).astype(F32)
        pooled = (tot / cnt - ext_sc[POOL_HALO:POOL_HALO + tm, cols]).astype(BF16)
        y = jnp.dot(pooled, w_ref[g], preferred_element_type=F32) * s_ref[:, cols]
        o_ref[0, :, cols] = y.astype(o_ref.dtype)


def _pool(p, w_pool, pool_scale):
    b, n, _ = p.shape
    tm = min(512, n)
    hb = tm // POOL_HALO
    last_hb = n // POOL_HALO - 1
    return pl.pallas_call(
        functools.partial(_pool_kernel, n=n, tm=tm),
        grid=(b, n // tm),
        in_specs=[pl.BlockSpec((1, tm, POOL_WIDTH), lambda bi, i: (bi, i, P_POOL)),
                  pl.BlockSpec((1, POOL_HALO, POOL_WIDTH), lambda bi, i: (bi, jnp.maximum(i * hb - 1, 0), P_POOL)),
                  pl.BlockSpec((1, POOL_HALO, POOL_WIDTH),
                               lambda bi, i: (bi, jnp.minimum((i + 1) * hb, last_hb), P_POOL)),
                  _resident(w_pool.shape),
                  pl.BlockSpec((1, POOL_WIDTH), lambda bi, i: (0, 0))],
        out_specs=pl.BlockSpec((1, tm, POOL_WIDTH), lambda bi, i: (bi, i, 0)),
        out_shape=jax.ShapeDtypeStruct((b, n, POOL_WIDTH), BF16),
        scratch_shapes=[pltpu.VMEM((tm + 2 * POOL_HALO, POOL_WIDTH), F32)],
        compiler_params=_params("parallel", "parallel"),
        name="pool",
    )(p, p, p, w_pool, pool_scale.reshape(1, POOL_WIDTH))


def _dft1_kernel(x_ref, a_ref, o_ref):
    tc = x_ref.shape[-1]
    x = x_ref[0].reshape(DFT_N1 * DFT_ROWS, tc)
    y = jnp.dot(a_ref[...], x, preferred_element_type=F32)
    o_ref[0] = y.astype(o_ref.dtype).reshape(2, DFT_N1, DFT_ROWS, tc)


def _dft_stage1(h, a_kron):
    b, n, d = h.shape
    tc = 1024
    h4 = h.reshape(b, DFT_N1, DFT_N2, d)
    return pl.pallas_call(
        _dft1_kernel,
        grid=(b, DFT_N2 // DFT_ROWS, d // tc),
        in_specs=[pl.BlockSpec((1, DFT_N1, DFT_ROWS, tc), lambda bi, r, c: (bi, 0, r, c)),
                  _resident(a_kron.shape)],
        out_specs=pl.BlockSpec((1, 2, DFT_N1, DFT_ROWS, tc), lambda bi, r, c: (bi, 0, 0, r, c)),
        out_shape=jax.ShapeDtypeStruct((b, 2, DFT_N1, DFT_N2, d), BF16),
        compiler_params=_params("parallel", "parallel", "parallel"),
        name="dft_stage1",
    )(h4, a_kron)


def _dft2_kernel(y_ref, m_ref, cc_ref, sc_ref, o_ref, *, rows_out, seq_len):
    y = y_ref[...].reshape(-1, D_MODEL)
    pq = jnp.dot(m_ref[0], y, preferred_element_type=F32).astype(BF16)
    ortho = 1.0 / math.sqrt(seq_len * FOURIER_GROUP)
    for g in range(FOURIER_GROUPS):
        cols = slice(g * FOURIER_GROUP, (g + 1) * FOURIER_GROUP)
        f = (jnp.dot(pq[:rows_out, cols], cc_ref[...], preferred_element_type=F32)
             - jnp.dot(pq[rows_out:, cols], sc_ref[...], preferred_element_type=F32))
        o_ref[0, :, cols] = (f * ortho).astype(o_ref.dtype)


def _dft_stage2(y, m, cc, sc):
    b = y.shape[0]
    out = pl.pallas_call(
        functools.partial(_dft2_kernel, rows_out=DFT_N2, seq_len=DFT_N1 * DFT_N2),
        grid=(b, DFT_N1),
        in_specs=[pl.BlockSpec((1, 2, 1, DFT_N2, D_MODEL), lambda bi, k: (bi, 0, k, 0, 0)),
                  pl.BlockSpec((1, 2 * DFT_N2, 2 * DFT_N2), lambda bi, k: (k, 0, 0)),
                  _resident(cc.shape), _resident(sc.shape)],
        out_specs=pl.BlockSpec((1, DFT_N2, D_MODEL), lambda bi, k: (bi, 0, k)),
        out_shape=jax.ShapeDtypeStruct((b, DFT_N2, DFT_N1 * D_MODEL), BF16),
        compiler_params=_params("parallel", "parallel"),
        name="dft_stage2",
    )(y, m, cc, sc)
    return out.reshape(b, DFT_N1 * DFT_N2, D_MODEL)


def _dft_small(h, m, cc, sc):
    b, n, d = h.shape
    return pl.pallas_call(
        functools.partial(_dft2_kernel, rows_out=n, seq_len=n),
        grid=(b,),
        in_specs=[pl.BlockSpec((1, n, d), lambda bi: (bi, 0, 0)),
                  _resident(m.shape), _resident(cc.shape), _resident(sc.shape)],
        out_specs=pl.BlockSpec((1, n, d), lambda bi: (bi, 0, 0)),
        out_shape=jax.ShapeDtypeStruct((b, n, d), BF16),
        compiler_params=_params("parallel"),
        name="dft_small",
    )(h, m, cc, sc)


def _cos_sin(num, den):
    ang = (num % den).astype(F32) * (2.0 * math.pi / den)
    return jnp.cos(ang), jnp.sin(ang)


def _dft_tables(n_ctx):
    i1 = jnp.arange(DFT_N1, dtype=jnp.int32)
    c1, s1 = _cos_sin(i1[:, None] * i1[None, :], DFT_N1)
    a = jnp.concatenate([c1, -s1], axis=0)
    eye = jnp.eye(DFT_ROWS, dtype=F32)
    a_kron = jnp.einsum('pt,jk->pjtk', a, eye).reshape(2 * DFT_N1 * DFT_ROWS, DFT_N1 * DFT_ROWS)
    n = DFT_N1 * DFT_N2
    t2 = jnp.arange(DFT_N2, dtype=jnp.int32)
    k = i1[:, None, None] + DFT_N1 * t2[None, :, None]
    ck, sk = _cos_sin(k * t2[None, None, :], n)
    m = jnp.concatenate([jnp.concatenate([ck, sk], axis=2),
                         jnp.concatenate([sk, -ck], axis=2)], axis=1)
    ic = jnp.arange(FOURIER_GROUP, dtype=jnp.int32)
    cc, sc = _cos_sin(ic[:, None] * ic[None, :], FOURIER_GROUP)
    il = jnp.arange(n_ctx, dtype=jnp.int32)
    cl, sl = _cos_sin(il[:, None] * il[None, :], n_ctx)
    m_ctx = jnp.concatenate([cl, sl], axis=0)[None]

    return dict(a_kron=a_kron.astype(BF16), m=m.astype(BF16), m_ctx=m_ctx.astype(BF16),
                cc=cc.astype(BF16), sc=sc.astype(BF16))


def _mlp_kernel(x_ref, g_ref, sh_ref, sc_ref, gate_ref, w1_ref, w2_ref, o_ref, h_sc, acc_sc):
    j = pl.program_id(2)

    @pl.when(j == 0)
    def _():
        y = _rms(x_ref[0], g_ref[...])
        h_sc[...] = (y * (1.0 + sc_ref[0]) + sh_ref[0]).astype(h_sc.dtype)
        acc_sc[...] = jnp.zeros(acc_sc.shape, F32)

    a = jnp.maximum(jnp.dot(h_sc[...], w1_ref[...], preferred_element_type=F32), 0.0)
    acc_sc[...] += jnp.dot((a * a).astype(BF16), w2_ref[...], preferred_element_type=F32)

    @pl.when(j == pl.num_programs(2) - 1)
    def _():
        o_ref[0] = x_ref[0] + gate_ref[0] * acc_sc[...]


def _mlp(x, g, sh, sc, gate, w1, w2):
    b, n, d = x.shape
    tm = min(512, n)
    tf = 512
    vec = pl.BlockSpec((1, 1, d), lambda bi, i, j: (bi, 0, 0))
    return pl.pallas_call(
        _mlp_kernel,
        grid=(b, n // tm, D_FF // tf),
        in_specs=[pl.BlockSpec((1, tm, d), lambda bi, i, j: (bi, i, 0)),
                  pl.BlockSpec((1, d), lambda bi, i, j: (0, 0)), vec, vec, vec,
                  pl.BlockSpec((d, tf), lambda bi, i, j: (0, j)),
                  pl.BlockSpec((tf, d), lambda bi, i, j: (j, 0))],
        out_specs=pl.BlockSpec((1, tm, d), lambda bi, i, j: (bi, i, 0)),
        out_shape=jax.ShapeDtypeStruct((b, n, d), F32),
        scratch_shapes=[pltpu.VMEM((tm, d), BF16), pltpu.VMEM((tm, d), F32)],
        compiler_params=_params("parallel", "parallel", "arbitrary"),
        name="mlp",
    )(x, g.reshape(1, d), sh, sc, gate, w1, w2)


def _rotate_half_axial(x):
    xr = x.reshape(x.shape[:-1] + (2, 2, ROPE_FREQS))
    return jnp.concatenate([-xr[..., 1:, :], xr[..., :1, :]], axis=-2).reshape(x.shape)


def _rope_tables(n):
    rows = n // GRID_W
    r = jnp.broadcast_to(jnp.arange(rows, dtype=F32)[:, None], (rows, GRID_W)).reshape(n)
    col = jnp.broadcast_to(jnp.arange(GRID_W, dtype=F32)[None, :], (rows, GRID_W)).reshape(n)
    inv = ROPE_THETA ** (-2.0 * jnp.arange(ROPE_FREQS, dtype=F32) / ROPE_AXIS)
    ang = jnp.stack([r[:, None] * inv, col[:, None] * inv], axis=1)
    ang = jnp.broadcast_to(ang[:, :, None, :], (n, 2, 2, ROPE_FREQS)).reshape(n, QK_ROPE)
    zeros = jnp.zeros((n, QK_ROPE), F32)
    return (jnp.concatenate([jnp.cos(ang), zeros], axis=1), jnp.concatenate([jnp.sin(ang), zeros], axis=1))


def _even_weights(w_in, w_uq, w_ukv):
    w_kr = w_in[:, Q_RANK + KV_RANK:Q_RANK + KV_RANK + QK_ROPE]
    w_in2 = jnp.concatenate([w_in[:, Q_RANK + KV_RANK + QK_ROPE:], w_in[:, :Q_RANK],
                             w_in[:, Q_RANK:Q_RANK + KV_RANK], w_kr, _rotate_half_axial(w_kr)], axis=1)
    wq = w_uq.reshape(Q_RANK, N_HEADS, QK_HEAD)
    wq = jnp.concatenate([wq, _rotate_half_axial(wq[..., QK_NOPE:])], axis=-1)
    wq = jnp.transpose(wq, (1, 0, 2))
    wkv = w_ukv.reshape(KV_RANK, N_HEADS, QK_NOPE + V_HEAD)
    wk = jnp.transpose(wkv[..., :QK_NOPE], (1, 0, 2))
    wvt = jnp.transpose(wkv[..., QK_NOPE:], (1, 2, 0))
    return w_in2.astype(BF16), wq.astype(BF16), wk.astype(BF16), wvt.astype(BF16)


def kernel(x, c, ctx, c_ctx, w_mod, b_mod, norm1, norm2, w_in, q_norm, w_uq, kv_norm, w_ukv, w_pool,
           pool_scale, w_out_even, w_out_odd, w_mlp1, w_mlp2, final_norm):
    b, n, d = x.shape
    n_ctx = ctx.shape[1]
    assert n == DFT_N1 * DFT_N2 and d == D_MODEL and n % GRID_W == 0
    n_keys = n + n_ctx
    tk = 768
    assert n_keys % tk == 0 and n % 512 == 0 and n_ctx == 256

    cos2, sin2 = _rope_tables(n)
    dft = _dft_tables(n_ctx)
    w1 = w_mlp1.astype(BF16)
    w2 = w_mlp2.astype(BF16)

    cc = jnp.concatenate([c, c_ctx[None], jnp.zeros((8 - b - 1, d), F32)], axis=0)
    mods = _modulation(cc, w_mod, b_mod)

    updates_ctx = [False] * DEPTH
    for l in reversed(range(DEPTH - 1)):
        updates_ctx[l] = (l + 1) % 2 == 0 or updates_ctx[l + 1]

    for l in range(DEPTH):
        even = l % 2 == 0
        i = l // 2
        mx = [mods[l, :b, k * d:(k + 1) * d].reshape(b, 1, d) for k in range(6)]
        mc = [jnp.broadcast_to(mods[l, b, k * d:(k + 1) * d].reshape(1, 1, d), (b, 1, d)) for k in range(6)]
        ctx_update = updates_ctx[l]

        hx = _normmod(x, norm1[l], mx[0], mx[1])
        if even or ctx_update:
            hc = _normmod(ctx, norm1[l], mc[0], mc[1])
        if even:
            w_in2, wq, wk, wvt = _even_weights(w_in[i], w_uq[i], w_ukv[i])
            w_out = w_out_even[i].astype(BF16)
            wp = w_pool[i].astype(BF16)
            px = _rowmm([hx], [w_in2], F32)
            pc = _rowmm([hc], [w_in2], F32)
            q = _qproj(px, q_norm[i], wq, cos2, sin2, rope=True)
            k_all, vt_all = _kvproj(px, kv_norm[i], wk, wvt, cos2, sin2, True, n_keys, 0)
            k_all, vt_all = _kvproj(pc, kv_norm[i], wk, wvt, cos2, sin2, False, n_keys, n // n_ctx, k_all, vt_all)
            attn_x = _attention(q, k_all, vt_all, n_keys, 0, tk)
            pool_x = _pool(px, wp, pool_scale[i])
            x = _rowmm([attn_x, pool_x], [w_out[:MLA_OUT], w_out[MLA_OUT:]], F32, res=x, gate=mx[2])
            if ctx_update:
                cq = _qproj(pc, q_norm[i], wq, cos2, sin2, rope=False)
                attn_c = _attention(cq, k_all, vt_all, n_ctx, n // n_ctx, n_ctx)
                pool_c = _pool(pc, wp, pool_scale[i])
                ctx = _rowmm([attn_c, pool_c], [w_out[:MLA_OUT], w_out[MLA_OUT:]], F32, res=ctx, gate=mc[2])
        else:
            w_out = w_out_odd[i].astype(BF16)
            fx = _dft_stage2(_dft_stage1(hx, dft['a_kron']), dft['m'], dft['cc'], dft['sc'])
            x = _rowmm([fx], [w_out], F32, res=x, gate=mx[2])
            if ctx_update:
                fc = _dft_small(hc, dft['m_ctx'], dft['cc'], dft['sc'])
                ctx = _rowmm([fc], [w_out], F32, res=ctx, gate=mc[2])
        x = _mlp(x, norm2[l], mx[3], mx[4], mx[5], w1[l], w2[l])
        if ctx_update:
            ctx = _mlp(ctx, norm2[l], mc[3], mc[4], mc[5], w1[l], w2[l])
    return _final_norm(x, final_norm)
```

```python
import functools
import math

import jax
import jax.numpy as jnp
import numpy as np
from jax import lax
from jax.experimental import pallas as pl
from jax.experimental.pallas import tpu as pltpu

D_MODEL = 2048
DEPTH = 4
GRID_W = 64
N_HEADS = 8
Q_RANK = 512
KV_RANK = 256
QK_NOPE = 128
QK_ROPE = 64
V_HEAD = 128
QK_HEAD = QK_NOPE + QK_ROPE
ATTN_SCALE = QK_HEAD ** -0.5
Q_SCALE = ATTN_SCALE * math.log2(math.e)
MLA_OUT = N_HEADS * V_HEAD
ROPE_AXIS = QK_ROPE // 2
ROPE_FREQS = ROPE_AXIS // 2
ROPE_THETA = 10000.0
POOL_WIDTH = D_MODEL // 2
POOL_WINDOWS = (2, 4, 8, 16)
POOL_GROUP = POOL_WIDTH // len(POOL_WINDOWS)
POOL_HALO = 8
FOURIER_GROUPS = 4
FOURIER_GROUP = D_MODEL // FOURIER_GROUPS
D_FF = 4 * D_MODEL
EPS = 1e-6

P_POOL = 0
P_QLAT = POOL_WIDTH
P_KV = POOL_WIDTH + Q_RANK
P_WIDTH = POOL_WIDTH + Q_RANK + KV_RANK + 2 * QK_ROPE
KV_BLOCK = KV_RANK + 2 * QK_ROPE

DFT_N1 = 64
DFT_N2 = 128
DFT_ROWS = 16

NORM_ROWS = 16
MLP_TM = 512
MLP_TF = 1024
ATTN_TQ = 1024
ATTN_TK = 1408

V7X_VMEM_LIMIT =56 * 1024 * 1024

F32 = jnp.float32
BF16 = jnp.bfloat16


def _params(*sem):
    return pltpu.CompilerParams(dimension_semantics=sem, vmem_limit_bytes=V7X_VMEM_LIMIT)


def _resident(shape):
    nd = len(shape)
    return pl.BlockSpec(shape, lambda *_: (0,) * nd, pipeline_mode=pl.Buffered(1))


def _rms(x, g):
    return x * lax.rsqrt(jnp.mean(x * x, axis=-1, keepdims=True) + EPS) * g


def _mod_kernel(c_ref, w_ref, b_ref, o_ref):
    c = c_ref[...]
    s = c * (1.0 / (1.0 + jnp.exp(-c)))
    o_ref[0] = jnp.dot(s, w_ref[0], preferred_element_type=F32, precision=lax.Precision.HIGHEST) + b_ref[0]


def _modulation(cc, w_mod, b_mod):
    rows = cc.shape[0]
    tn = 1536
    n6 = 6 * D_MODEL
    return pl.pallas_call(
        _mod_kernel,
        grid=(DEPTH, n6 // tn),
        in_specs=[
            pl.BlockSpec((rows, D_MODEL), lambda l, j: (0, 0)),
            pl.BlockSpec((1, D_MODEL, tn), lambda l, j: (l, 0, j)),
            pl.BlockSpec((1, 1, tn), lambda l, j: (l, 0, j)),
        ],
        out_specs=pl.BlockSpec((1, rows, tn), lambda l, j: (l, 0, j)),
        out_shape=jax.ShapeDtypeStruct((DEPTH, rows, n6), F32),
        compiler_params=_params("parallel", "parallel"),
        name="modulation",
    )(cc, w_mod, b_mod.reshape(DEPTH, 1, n6))


def _normmod_rows(x_ref, g_ref, sh_ref, sc_ref, dst_ref):
    gs = g_ref[...] * (1.0 + sc_ref[0])
    sh = sh_ref[0]
    for r in range(0, x_ref.shape[1], NORM_ROWS):
        x = x_ref[0, r:r + NORM_ROWS, :]
        inv = lax.rsqrt(jnp.mean(x * x, axis=-1, keepdims=True) + EPS)
        dst_ref[r:r + NORM_ROWS, :] = (x * inv * gs + sh).astype(dst_ref.dtype)


def _normmod_kernel(x_ref, g_ref, sh_ref, sc_ref, o_ref):
    _normmod_rows(x_ref, g_ref, sh_ref, sc_ref, o_ref.at[0])


def _normmod(x, g, sh, sc, out_dtype=BF16):
    b, n, d = x.shape
    tm = min(512, n)
    vec = pl.BlockSpec((1, 1, d), lambda bi, i: (bi, 0, 0))
    return pl.pallas_call(
        _normmod_kernel,
        grid=(b, n // tm),
        in_specs=[pl.BlockSpec((1, tm, d), lambda bi, i: (bi, i, 0)),
                  pl.BlockSpec((1, d), lambda bi, i: (0, 0)), vec, vec],
        out_specs=pl.BlockSpec((1, tm, d), lambda bi, i: (bi, i, 0)),
        out_shape=jax.ShapeDtypeStruct((b, n, d), out_dtype),
        compiler_params=_params("parallel", "parallel"),
        name="normmod",
    )(x, g.reshape(1, d), sh, sc)


def _rowmm_kernel(*refs, n_in, has_res):
    a_refs, w_refs = refs[:n_in], refs[n_in:2 * n_in]
    o_ref = refs[-1]
    acc = None
    for a, w in zip(a_refs, w_refs):
        d = jnp.dot(a[0], w[...], preferred_element_type=F32)
        acc = d if acc is None else acc + d
    if has_res:
        res_ref, gate_ref = refs[2 * n_in], refs[2 * n_in + 1]
        acc = res_ref[0] + gate_ref[0] * acc
    o_ref[0] = acc.astype(o_ref.dtype)


def _rowmm(a_list, w_list, out_dtype, res=None, gate=None, tm=512):
    b, n, _ = a_list[0].shape
    nout = w_list[0].shape[1]
    tm = min(tm, n)
    in_specs = [pl.BlockSpec((1, tm, a.shape[2]), lambda bi, i: (bi, i, 0)) for a in a_list]
    in_specs += [_resident(w.shape) for w in w_list]
    args = list(a_list) + list(w_list)
    if res is not None:
        in_specs += [pl.BlockSpec((1, tm, nout), lambda bi, i: (bi, i, 0)),
                     pl.BlockSpec((1, 1, nout), lambda bi, i: (bi, 0, 0))]
        args += [res, gate]
    return pl.pallas_call(
        functools.partial(_rowmm_kernel, n_in=len(a_list), has_res=res is not None),
        grid=(b, n // tm),
        in_specs=in_specs,
        out_specs=pl.BlockSpec((1, tm, nout), lambda bi, i: (bi, i, 0)),
        out_shape=jax.ShapeDtypeStruct((b, n, nout), out_dtype),
        compiler_params=_params("parallel", "parallel"),
        name="rowmm",
    )(*args)


def _rope_cols(t, c_ref, s_ref, rope):
    if not rope:
        return t
    return t * c_ref[...] + pltpu.roll(t, QK_ROPE, 1) * s_ref[...]


def _qproj_kernel(p_ref, g_ref, w_ref, c_ref, s_ref, o_ref, *, rope):
    z = _rms(p_ref[0], g_ref[...]).astype(BF16)
    for h in range(N_HEADS):
        acc = jnp.dot(z, w_ref[h], preferred_element_type=F32)
        o_ref[0, h, :, :QK_NOPE] = (acc[:, :QK_NOPE] * Q_SCALE).astype(o_ref.dtype)
        r = _rope_cols(acc[:, QK_NOPE:], c_ref, s_ref, rope)
        o_ref[0, h, :, QK_NOPE:] = (r[:, :QK_ROPE] * Q_SCALE).astype(o_ref.dtype)


def _qproj(p, q_norm, wq, cos2, sin2, rope):
    b, n, _ = p.shape
    tm = min(512, n)
    tab = pl.BlockSpec((tm, 2 * QK_ROPE), lambda bi, i: (i, 0))
    return pl.pallas_call(
        functools.partial(_qproj_kernel, rope=rope),
        grid=(b, n // tm),
        in_specs=[pl.BlockSpec((1, tm, Q_RANK), lambda bi, i: (bi, i, P_QLAT // Q_RANK)),
                  pl.BlockSpec((1, Q_RANK), lambda bi, i: (0, 0)),
                  _resident(wq.shape), tab, tab],
        out_specs=pl.BlockSpec((1, N_HEADS, tm, QK_HEAD), lambda bi, i: (bi, 0, i, 0)),
        out_shape=jax.ShapeDtypeStruct((b, N_HEADS, n, QK_HEAD), BF16),
        compiler_params=_params("parallel", "parallel"),
        name="qproj",
    )(p, q_norm.reshape(1, Q_RANK), wq, cos2, sin2)


def _kvproj_kernel(p_ref, g_ref, wk_ref, wvt_ref, c_ref, s_ref, *rest, rope):
    k_ref, vt_ref = rest[-2], rest[-1]
    blk = p_ref[0]
    z = _rms(blk[:, :KV_RANK], g_ref[...]).astype(BF16)
    kr = _rope_cols(blk[:, KV_RANK:], c_ref, s_ref, rope)[:, :QK_ROPE].astype(k_ref.dtype)
    for h in range(N_HEADS):
        k_ref[0, h, :, :QK_NOPE] = jnp.dot(z, wk_ref[h], preferred_element_type=F32).astype(k_ref.dtype)
        k_ref[0, h, :, QK_NOPE:] = kr
        vt = lax.dot_general(wvt_ref[h], z, (((1,), (1,)), ((), ())), preferred_element_type=F32)
        vt_ref[0, h] = vt.astype(vt_ref.dtype)


def _kvproj(p, kv_norm, wk, wvt, cos2, sin2, rope, n_keys, row_block0, k_all=None, vt_all=None):
    b, n, _ = p.shape
    tm = min(512, n)
    tab = pl.BlockSpec((tm, 2 * QK_ROPE), lambda bi, i: (i, 0))
    in_specs = [pl.BlockSpec((1, tm, KV_BLOCK), lambda bi, i: (bi, i, P_KV // KV_BLOCK)),
                pl.BlockSpec((1, KV_RANK), lambda bi, i: (0, 0)),
                _resident(wk.shape), _resident(wvt.shape), tab, tab]
    args = [p, kv_norm.reshape(1, KV_RANK), wk, wvt, cos2, sin2]
    aliases = {}
    if k_all is not None:
        aliases = {len(args): 0, len(args) + 1: 1}
        in_specs += [pl.BlockSpec(memory_space=pl.ANY), pl.BlockSpec(memory_space=pl.ANY)]
        args += [k_all, vt_all]
    return pl.pallas_call(
        functools.partial(_kvproj_kernel, rope=rope),
        grid=(b, n // tm),
        in_specs=in_specs,
        out_specs=[pl.BlockSpec((1, N_HEADS, tm, QK_HEAD), lambda bi, i: (bi, 0, row_block0 + i, 0)),
                   pl.BlockSpec((1, N_HEADS, V_HEAD, tm), lambda bi, i: (bi, 0, 0, row_block0 + i))],
        out_shape=[jax.ShapeDtypeStruct((b, N_HEADS, n_keys, QK_HEAD), BF16),
                   jax.ShapeDtypeStruct((b, N_HEADS, V_HEAD, n_keys), BF16)],
        input_output_aliases=aliases,
        compiler_params=_params("parallel", "parallel"),
        name="kvproj",
    )(*args)


def _attn_kernel(q_ref, k_ref, vt_ref, o_ref, *, tk):
    q = q_ref[0, 0]
    tq = q.shape[0]
    n_blocks = k_ref.shape[2] // tk

    def scores(j):
        return lax.dot_general(k_ref[0, 0, j * tk:(j + 1) * tk, :], q, (((1,), (1,)), ((), ())),
                               preferred_element_type=F32)

    m = jnp.full((1, tq), -jnp.inf, F32)
    l = jnp.zeros((1, tq), F32)
    acc = jnp.zeros((V_HEAD, tq), F32)
    st_next = scores(0)
    for j in range(n_blocks):
        st = st_next
        if j + 1 < n_blocks:
            st_next = scores(j + 1)
        m_new = jnp.maximum(m, jnp.max(st, axis=0, keepdims=True))
        alpha = jnp.exp2(m - m_new)
        p = jnp.exp2(st - m_new)
        l = alpha * l + jnp.sum(p, axis=0, keepdims=True)
        acc = alpha * acc + jnp.dot(vt_ref[0, 0, :, j * tk:(j + 1) * tk], p.astype(BF16),
                                    preferred_element_type=F32)
        m = m_new
    o_ref[0] = jnp.transpose(acc / l).astype(o_ref.dtype)


def _attention(q, k_all, vt_all, kv_len, kv_block0, tk):
    b, h, n, _ = q.shape
    tq = min(ATTN_TQ, n)
    return pl.pallas_call(
        functools.partial(_attn_kernel, tk=tk),
        grid=(b, h, n // tq),
        in_specs=[pl.BlockSpec((1, 1, tq, QK_HEAD), lambda bi, hi, i: (bi, hi, i, 0)),
                  pl.BlockSpec((1, 1, kv_len, QK_HEAD), lambda bi, hi, i: (bi, hi, kv_block0, 0)),
                  pl.BlockSpec((1, 1, V_HEAD, kv_len), lambda bi, hi, i: (bi, hi, 0, kv_block0))],
        out_specs=pl.BlockSpec((1, tq, V_HEAD), lambda bi, hi, i: (bi, i, hi)),
        out_shape=jax.ShapeDtypeStruct((b, n, h * V_HEAD), BF16),
        compiler_params=_params("parallel", "parallel", "parallel"),
        name="attention",
    )(q, k_all, vt_all)


def _pool_kernel(u_ref, prev_ref, next_ref, w_ref, s_ref, o_ref, ext_sc, *, n, tm):
    i = pl.program_id(1)
    keep_prev = jnp.where(i > 0, 1.0, 0.0)
    keep_next = jnp.where(i < pl.num_programs(1) - 1, 1.0, 0.0)
    ext_sc[0:POOL_HALO] = prev_ref[0] * keep_prev
    ext_sc[POOL_HALO:POOL_HALO + tm] = u_ref[0]
    ext_sc[POOL_HALO + tm:] = next_ref[0] * keep_next
    t = i * tm + lax.broadcasted_iota(jnp.int32, (tm, 1), 0)
    for g, win in enumerate(POOL_WINDOWS):
        half = win // 2
        cols = slice(g * POOL_GROUP, (g + 1) * POOL_GROUP)
        tot = ext_sc[POOL_HALO - half:POOL_HALO - half + tm, cols]
        for d in range(1 - half, half):
            tot = tot + ext_sc[POOL_HALO + d:POOL_HALO + d + tm, cols]
        cnt = (jnp.minimum(t + half, n) - jnp.maximum(t - half, 0)).astype(F32)
        pooled = (tot / cnt - ext_sc[POOL_HALO:POOL_HALO + tm, cols]).astype(BF16)
        y = jnp.dot(pooled, w_ref[g], preferred_element_type=F32) * s_ref[:, cols]
        o_ref[0, :, cols] = y.astype(o_ref.dtype)


def _pool(p, w_pool, pool_scale):
    b, n, _ = p.shape
    tm = min(512, n)
    hb = tm // POOL_HALO
    last_hb = n // POOL_HALO - 1
    return pl.pallas_call(
        functools.partial(_pool_kernel, n=n, tm=tm),
        grid=(b, n // tm),
        in_specs=[pl.BlockSpec((1, tm, POOL_WIDTH), lambda bi, i: (bi, i, P_POOL)),
                  pl.BlockSpec((1, POOL_HALO, POOL_WIDTH), lambda bi, i: (bi, jnp.maximum(i * hb - 1, 0), P_POOL)),
                  pl.BlockSpec((1, POOL_HALO, POOL_WIDTH),
                               lambda bi, i: (bi, jnp.minimum((i + 1) * hb, last_hb), P_POOL)),
                  _resident(w_pool.shape),
                  pl.BlockSpec((1, POOL_WIDTH), lambda bi, i: (0, 0))],
        out_specs=pl.BlockSpec((1, tm, POOL_WIDTH), lambda bi, i: (bi, i, 0)),
        out_shape=jax.ShapeDtypeStruct((b, n, POOL_WIDTH), BF16),
        scratch_shapes=[pltpu.VMEM((tm + 2 * POOL_HALO, POOL_WIDTH), F32)],
        compiler_params=_params("parallel", "parallel"),
        name="pool",
    )(p, p, p, w_pool, pool_scale.reshape(1, POOL_WIDTH))


def _dft1_kernel(x_ref, a_ref, o_ref):
    tc = x_ref.shape[-1]
    x = x_ref[0].reshape(DFT_N1 * DFT_ROWS, tc)
    y = jnp.dot(a_ref[...], x, preferred_element_type=F32)
    o_ref[0] = y.astype(o_ref.dtype).reshape(2, DFT_N1, DFT_ROWS, tc)


def _dft_stage1(h, a_kron):
    b, n, d = h.shape
    tc = 1024
    h4 = h.reshape(b, DFT_N1, DFT_N2, d)
    return pl.pallas_call(
        _dft1_kernel,
        grid=(b, DFT_N2 // DFT_ROWS, d // tc),
        in_specs=[pl.BlockSpec((1, DFT_N1, DFT_ROWS, tc), lambda bi, r, c: (bi, 0, r, c)),
                  _resident(a_kron.shape)],
        out_specs=pl.BlockSpec((1, 2, DFT_N1, DFT_ROWS, tc), lambda bi, r, c: (bi, 0, 0, r, c)),
        out_shape=jax.ShapeDtypeStruct((b, 2, DFT_N1, DFT_N2, d), BF16),
        compiler_params=_params("parallel", "parallel", "parallel"),
        name="dft_stage1",
    )(h4, a_kron)


def _dft2_kernel(y_ref, m_ref, cc_ref, sc_ref, o_ref, *, rows_out, seq_len):
    y = y_ref[...].reshape(-1, D_MODEL)
    pq = jnp.dot(m_ref[0], y, preferred_element_type=F32).astype(BF16)
    ortho = 1.0 / math.sqrt(seq_len * FOURIER_GROUP)
    for g in range(FOURIER_GROUPS):
        cols = slice(g * FOURIER_GROUP, (g + 1) * FOURIER_GROUP)
        f = (jnp.dot(pq[:rows_out, cols], cc_ref[...], preferred_element_type=F32)
             - jnp.dot(pq[rows_out:, cols], sc_ref[...], preferred_element_type=F32))
        o_ref[0, :, cols] = (f * ortho).astype(o_ref.dtype)


def _dft_stage2(y, m, cc, sc):
    b = y.shape[0]
    out = pl.pallas_call(
        functools.partial(_dft2_kernel, rows_out=DFT_N2, seq_len=DFT_N1 * DFT_N2),
        grid=(b, DFT_N1),
        in_specs=[pl.BlockSpec((1, 2, 1, DFT_N2, D_MODEL), lambda bi, k: (bi, 0, k, 0, 0)),
                  pl.BlockSpec((1, 2 * DFT_N2, 2 * DFT_N2), lambda bi, k: (k, 0, 0)),
                  _resident(cc.shape), _resident(sc.shape)],
        out_specs=pl.BlockSpec((1, DFT_N2, D_MODEL), lambda bi, k: (bi, 0, k)),
        out_shape=jax.ShapeDtypeStruct((b, DFT_N2, DFT_N1 * D_MODEL), BF16),
        compiler_params=_params("parallel", "parallel"),
        name="dft_stage2",
    )(y, m, cc, sc)
    return out.reshape(b, DFT_N1 * DFT_N2, D_MODEL)


def _dft_small(h, m, cc, sc):
    b, n, d = h.shape
    return pl.pallas_call(
        functools.partial(_dft2_kernel, rows_out=n, seq_len=n),
        grid=(b,),
        in_specs=[pl.BlockSpec((1, n, d), lambda bi: (bi, 0, 0)),
                  _resident(m.shape), _resident(cc.shape), _resident(sc.shape)],
        out_specs=pl.BlockSpec((1, n, d), lambda bi: (bi, 0, 0)),
        out_shape=jax.ShapeDtypeStruct((b, n, d), BF16),
        compiler_params=_params("parallel"),
        name="dft_small",
    )(h, m, cc, sc)


def _cos_sin(num, den):
    ang = (num % den).astype(F32) * (2.0 * math.pi / den)
    return jnp.cos(ang), jnp.sin(ang)


def _dft_tables(n_ctx):
    i1 = jnp.arange(DFT_N1, dtype=jnp.int32)
    c1, s1 = _cos_sin(i1[:, None] * i1[None, :], DFT_N1)
    a = jnp.concatenate([c1, -s1], axis=0)
    eye = jnp.eye(DFT_ROWS, dtype=F32)
    a_kron = jnp.einsum('pt,jk->pjtk', a, eye).reshape(2 * DFT_N1 * DFT_ROWS, DFT_N1 * DFT_ROWS)
    n = DFT_N1 * DFT_N2
    t2 = jnp.arange(DFT_N2, dtype=jnp.int32)
    k = i1[:, None, None] + DFT_N1 * t2[None, :, None]
    ck, sk = _cos_sin(k * t2[None, None, :], n)
    m = jnp.concatenate([jnp.concatenate([ck, sk], axis=2),
                         jnp.concatenate([sk, -ck], axis=2)], axis=1)
    ic = jnp.arange(FOURIER_GROUP, dtype=jnp.int32)
    cc, sc = _cos_sin(ic[:, None] * ic[None, :], FOURIER_GROUP)
    il = jnp.arange(n_ctx, dtype=jnp.int32)
    cl, sl = _cos_sin(il[:, None] * il[None, :], n_ctx)
    m_ctx = jnp.concatenate([cl, sl], axis=0)[None]

    return dict(a_kron=a_kron.astype(BF16), m=m.astype(BF16), m_ctx=m_ctx.astype(BF16),
                cc=cc.astype(BF16), sc=sc.astype(BF16))


def _mlp_kernel(x_ref, g_ref, sh_ref, sc_ref, gate_ref, w1_ref, w2_ref, *rest, final):
    o_ref, h_sc, acc_sc = rest[-3:]
    j = pl.program_id(2)

    @pl.when(j == 0)
    def _():
        _normmod_rows(x_ref, g_ref, sh_ref, sc_ref, h_sc)
        acc_sc[...] = jnp.zeros(acc_sc.shape, F32)

    a = jnp.maximum(jnp.dot(h_sc[...], w1_ref[...], preferred_element_type=F32), 0.0)
    acc_sc[...] += jnp.dot((a * a).astype(BF16), w2_ref[...], preferred_element_type=F32)

    @pl.when(j == pl.num_programs(2) - 1)
    def _():
        gate = gate_ref[0]
        for r in range(0, x_ref.shape[1], NORM_ROWS):
            rows = slice(r, r + NORM_ROWS)
            y = x_ref[0, rows, :] + gate * acc_sc[rows, :]
            if final:
                y = _rms(y, rest[0][...])
            o_ref[0, rows, :] = y


def _mlp(x, g, sh, sc, gate, w1, w2, final_g=None):
    b, n, d = x.shape
    tm = min(MLP_TM, n)
    tf = MLP_TF
    vec = pl.BlockSpec((1, 1, d), lambda bi, i, j: (bi, 0, 0))
    row = pl.BlockSpec((1, d), lambda bi, i, j: (0, 0))
    in_specs = [pl.BlockSpec((1, tm, d), lambda bi, i, j: (bi, i, 0)), row, vec, vec, vec,
                pl.BlockSpec((d, tf), lambda bi, i, j: (0, j)),
                pl.BlockSpec((tf, d), lambda bi, i, j: (j, 0))]
    args = [x, g.reshape(1, d), sh, sc, gate, w1, w2]
    if final_g is not None:
        in_specs.append(row)
        args.append(final_g.reshape(1, d))
    return pl.pallas_call(
        functools.partial(_mlp_kernel, final=final_g is not None),
        grid=(b, n // tm, D_FF // tf),
        in_specs=in_specs,
        out_specs=pl.BlockSpec((1, tm, d), lambda bi, i, j: (bi, i, 0)),
        out_shape=jax.ShapeDtypeStruct((b, n, d), F32),
        scratch_shapes=[pltpu.VMEM((tm, d), BF16), pltpu.VMEM((tm, d), F32)],
        compiler_params=_params("parallel", "parallel", "arbitrary"),
        name="mlp",
    )(*args)


def _rotate_half_axial(x):
    xr = x.reshape(x.shape[:-1] + (2, 2, ROPE_FREQS))
    return jnp.concatenate([-xr[..., 1:, :], xr[..., :1, :]], axis=-2).reshape(x.shape)


def _rope_tables(n):
    rows = n // GRID_W
    r = jnp.broadcast_to(jnp.arange(rows, dtype=F32)[:, None], (rows, GRID_W)).reshape(n)
    col = jnp.broadcast_to(jnp.arange(GRID_W, dtype=F32)[None, :], (rows, GRID_W)).reshape(n)
    inv = ROPE_THETA ** (-2.0 * jnp.arange(ROPE_FREQS, dtype=F32) / ROPE_AXIS)
    ang = jnp.stack([r[:, None] * inv, col[:, None] * inv], axis=1)
    ang = jnp.broadcast_to(ang[:, :, None, :], (n, 2, 2, ROPE_FREQS)).reshape(n, QK_ROPE)
    zeros = jnp.zeros((n, QK_ROPE), F32)
    return (jnp.concatenate([jnp.cos(ang), zeros], axis=1), jnp.concatenate([jnp.sin(ang), zeros], axis=1))


def _even_weights(w_in, w_uq, w_ukv):
    w_kr = w_in[:, Q_RANK + KV_RANK:Q_RANK + KV_RANK + QK_ROPE]
    w_in2 = jnp.concatenate([w_in[:, Q_RANK + KV_RANK + QK_ROPE:], w_in[:, :Q_RANK],
                             w_in[:, Q_RANK:Q_RANK + KV_RANK], w_kr, _rotate_half_axial(w_kr)], axis=1)
    wq = w_uq.reshape(Q_RANK, N_HEADS, QK_HEAD)
    wq = jnp.concatenate([wq, _rotate_half_axial(wq[..., QK_NOPE:])], axis=-1)
    wq = jnp.transpose(wq, (1, 0, 2))
    wkv = w_ukv.reshape(KV_RANK, N_HEADS, QK_NOPE + V_HEAD)
    wk = jnp.transpose(wkv[..., :QK_NOPE], (1, 0, 2))
    wvt = jnp.transpose(wkv[..., QK_NOPE:], (1, 2, 0))
    return w_in2.astype(BF16), wq.astype(BF16), wk.astype(BF16), wvt.astype(BF16)


def kernel(x, c, ctx, c_ctx, w_mod, b_mod, norm1, norm2, w_in, q_norm, w_uq, kv_norm, w_ukv, w_pool,
           pool_scale, w_out_even, w_out_odd, w_mlp1, w_mlp2, final_norm):
    b, n, d = x.shape
    n_ctx = ctx.shape[1]
    assert n == DFT_N1 * DFT_N2 and d == D_MODEL and n % GRID_W == 0
    n_keys = n + n_ctx
    tk = ATTN_TK
    assert n_keys % tk == 0 and n % 512 == 0 and n_ctx == 256

    cos2, sin2 = _rope_tables(n)
    dft = _dft_tables(n_ctx)
    w1 = w_mlp1.astype(BF16)
    w2 = w_mlp2.astype(BF16)

    cc = jnp.concatenate([c, c_ctx[None], jnp.zeros((8 - b - 1, d), F32)], axis=0)
    mods = _modulation(cc, w_mod, b_mod)

    updates_ctx = [False] * DEPTH
    for l in reversed(range(DEPTH - 1)):
        updates_ctx[l] = (l + 1) % 2 == 0 or updates_ctx[l + 1]

    for l in range(DEPTH):
        even = l % 2 == 0
        i = l // 2
        mx = [mods[l, :b, k * d:(k + 1) * d].reshape(b, 1, d) for k in range(6)]
        mc = [jnp.broadcast_to(mods[l, b, k * d:(k + 1) * d].reshape(1, 1, d), (b, 1, d)) for k in range(6)]
        ctx_update = updates_ctx[l]

        hx = _normmod(x, norm1[l], mx[0], mx[1])
        if even or ctx_update:
            hc = _normmod(ctx, norm1[l], mc[0], mc[1])
        if even:
            w_in2, wq, wk, wvt = _even_weights(w_in[i], w_uq[i], w_ukv[i])
            w_out = w_out_even[i].astype(BF16)
            wp = w_pool[i].astype(BF16)
            px = _rowmm([hx], [w_in2], F32)
            pc = _rowmm([hc], [w_in2], F32)
            q = _qproj(px, q_norm[i], wq, cos2, sin2, rope=True)
            k_all, vt_all = _kvproj(px, kv_norm[i], wk, wvt, cos2, sin2, True, n_keys, 0)
            k_all, vt_all = _kvproj(pc, kv_norm[i], wk, wvt, cos2, sin2, False, n_keys, n // n_ctx, k_all, vt_all)
            attn_x = _attention(q, k_all, vt_all, n_keys, 0, tk)
            pool_x = _pool(px, wp, pool_scale[i])
            x = _rowmm([attn_x, pool_x], [w_out[:MLA_OUT], w_out[MLA_OUT:]], F32, res=x, gate=mx[2])
            if ctx_update:
                cq = _qproj(pc, q_norm[i], wq, cos2, sin2, rope=False)
                attn_c = _attention(cq, k_all, vt_all, n_ctx, n // n_ctx, n_ctx)
                pool_c = _pool(pc, wp, pool_scale[i])
                ctx = _rowmm([attn_c, pool_c], [w_out[:MLA_OUT], w_out[MLA_OUT:]], F32, res=ctx, gate=mc[2])
        else:
            w_out = w_out_odd[i].astype(BF16)
            fx = _dft_stage2(_dft_stage1(hx, dft['a_kron']), dft['m'], dft['cc'], dft['sc'])
            x = _rowmm([fx], [w_out], F32, res=x, gate=mx[2])
            if ctx_update:
                fc = _dft_small(hc, dft['m_ctx'], dft['cc'], dft['sc'])
                ctx = _rowmm([fc], [w_out], F32, res=ctx, gate=mc[2])
        x = _mlp(x, norm2[l], mx[3], mx[4], mx[5], w1[l], w2[l], final_norm if l == DEPTH - 1 else None)
        if ctx_update:
            ctx = _mlp(ctx, norm2[l], mc[3], mc[4], mc[5], w1[l], w2[l])
    return x
```

```python
import functools
import math

import jax
import jax.numpy as jnp
import numpy as np
from jax import lax
from jax.experimental import pallas as pl
from jax.experimental.pallas import tpu as pltpu

D_MODEL = 2048
DEPTH = 4
GRID_W = 64
N_HEADS = 8
Q_RANK = 512
KV_RANK = 256
QK_NOPE = 128
QK_ROPE = 64
V_HEAD = 128
QK_HEAD = QK_NOPE + QK_ROPE
ATTN_SCALE = QK_HEAD ** -0.5
Q_SCALE = ATTN_SCALE * math.log2(math.e)
MLA_OUT = N_HEADS * V_HEAD
ROPE_AXIS = QK_ROPE // 2
ROPE_FREQS = ROPE_AXIS // 2
ROPE_THETA = 10000.0
POOL_WIDTH = D_MODEL // 2
POOL_WINDOWS = (2, 4, 8, 16)
POOL_GROUP = POOL_WIDTH // len(POOL_WINDOWS)
POOL_HALO = 8
FOURIER_GROUPS = 4
FOURIER_GROUP = D_MODEL // FOURIER_GROUPS
D_FF = 4 * D_MODEL
EPS = 1e-6

P_POOL = 0
P_QLAT = POOL_WIDTH
P_KV = POOL_WIDTH + Q_RANK
P_WIDTH = POOL_WIDTH + Q_RANK + KV_RANK + 2 * QK_ROPE
KV_BLOCK = KV_RANK + 2 * QK_ROPE

DFT_N1 = 64
DFT_N2 = 128
DFT_ROWS = 16

NORM_ROWS = 16
MLP_TM = 512
MLP_TF = 1024
MLP_ROWS = 128
ATTN_TQ = 1024
ATTN_TK_FIXED = 1408
ATTN_TK_ONLINE = 1408
ATTN_BOUND_SLACK = 1.02
ATTN_MIN_DENOM = 2.0 ** -60

V7X_VMEM_LIMIT =56 * 1024 * 1024

F32 = jnp.float32
BF16 = jnp.bfloat16


def _params(*sem):
    return pltpu.CompilerParams(dimension_semantics=sem, vmem_limit_bytes=V7X_VMEM_LIMIT)


def _resident(shape):
    nd = len(shape)
    return pl.BlockSpec(shape, lambda *_: (0,) * nd, pipeline_mode=pl.Buffered(1))


def _rms(x, g):
    return x * lax.rsqrt(jnp.mean(x * x, axis=-1, keepdims=True) + EPS) * g


def _mod_kernel(c_ref, w_ref, b_ref, o_ref):
    c = c_ref[...]
    s = c * (1.0 / (1.0 + jnp.exp(-c)))
    o_ref[0] = jnp.dot(s, w_ref[0], preferred_element_type=F32, precision=lax.Precision.HIGHEST) + b_ref[0]


def _modulation(cc, w_mod, b_mod):
    rows = cc.shape[0]
    tn = 1536
    n6 = 6 * D_MODEL
    return pl.pallas_call(
        _mod_kernel,
        grid=(DEPTH, n6 // tn),
        in_specs=[
            pl.BlockSpec((rows, D_MODEL), lambda l, j: (0, 0)),
            pl.BlockSpec((1, D_MODEL, tn), lambda l, j: (l, 0, j)),
            pl.BlockSpec((1, 1, tn), lambda l, j: (l, 0, j)),
        ],
        out_specs=pl.BlockSpec((1, rows, tn), lambda l, j: (l, 0, j)),
        out_shape=jax.ShapeDtypeStruct((DEPTH, rows, n6), F32),
        compiler_params=_params("parallel", "parallel"),
        name="modulation",
    )(cc, w_mod, b_mod.reshape(DEPTH, 1, n6))


def _normmod_rows(x_ref, g_ref, sh_ref, sc_ref, dst_ref, start=0, stop=None):
    gs = g_ref[...] * (1.0 + sc_ref[0])
    sh = sh_ref[0]
    for r in range(start, x_ref.shape[1] if stop is None else stop, NORM_ROWS):
        x = x_ref[0, r:r + NORM_ROWS, :]
        inv = lax.rsqrt(jnp.mean(x * x, axis=-1, keepdims=True) + EPS)
        dst_ref[r:r + NORM_ROWS, :] = (x * inv * gs + sh).astype(dst_ref.dtype)


def _normmod_kernel(x_ref, g_ref, sh_ref, sc_ref, o_ref):
    _normmod_rows(x_ref, g_ref, sh_ref, sc_ref, o_ref.at[0])


def _normmod(x, g, sh, sc, out_dtype=BF16):
    b, n, d = x.shape
    tm = min(512, n)
    vec = pl.BlockSpec((1, 1, d), lambda bi, i: (bi, 0, 0))
    return pl.pallas_call(
        _normmod_kernel,
        grid=(b, n // tm),
        in_specs=[pl.BlockSpec((1, tm, d), lambda bi, i: (bi, i, 0)),
                  pl.BlockSpec((1, d), lambda bi, i: (0, 0)), vec, vec],
        out_specs=pl.BlockSpec((1, tm, d), lambda bi, i: (bi, i, 0)),
        out_shape=jax.ShapeDtypeStruct((b, n, d), out_dtype),
        compiler_params=_params("parallel", "parallel"),
        name="normmod",
    )(x, g.reshape(1, d), sh, sc)


def _normproj_kernel(x_ref, g_ref, sh_ref, sc_ref, w_ref, o_ref, h_sc):
    for r in range(0, x_ref.shape[1], MLP_ROWS):
        _normmod_rows(x_ref, g_ref, sh_ref, sc_ref, h_sc, r, r + MLP_ROWS)
        o_ref[0, r:r + MLP_ROWS, :] = jnp.dot(h_sc[r:r + MLP_ROWS, :], w_ref[...],
                                              preferred_element_type=F32).astype(o_ref.dtype)


def _normproj(x, g, sh, sc, w, out_dtype=F32):
    b, n, d = x.shape
    tm = min(512, n)
    vec = pl.BlockSpec((1, 1, d), lambda bi, i: (bi, 0, 0))
    return pl.pallas_call(
        _normproj_kernel,
        grid=(b, n // tm),
        in_specs=[pl.BlockSpec((1, tm, d), lambda bi, i: (bi, i, 0)),
                  pl.BlockSpec((1, d), lambda bi, i: (0, 0)), vec, vec, _resident(w.shape)],
        out_specs=pl.BlockSpec((1, tm, w.shape[1]), lambda bi, i: (bi, i, 0)),
        out_shape=jax.ShapeDtypeStruct((b, n, w.shape[1]), out_dtype),
        scratch_shapes=[pltpu.VMEM((tm, d), BF16)],
        compiler_params=_params("parallel", "parallel"),
        name="normproj",
    )(x, g.reshape(1, d), sh, sc, w)


def _rowmm_kernel(*refs, n_in, has_res):
    a_refs, w_refs = refs[:n_in], refs[n_in:2 * n_in]
    o_ref = refs[-1]
    acc = None
    for a, w in zip(a_refs, w_refs):
        d = jnp.dot(a[0], w[...], preferred_element_type=F32)
        acc = d if acc is None else acc + d
    if has_res:
        res_ref, gate_ref = refs[2 * n_in], refs[2 * n_in + 1]
        acc = res_ref[0] + gate_ref[0] * acc
    o_ref[0] = acc.astype(o_ref.dtype)


def _rowmm(a_list, w_list, out_dtype, res=None, gate=None, tm=512):
    b, n, _ = a_list[0].shape
    nout = w_list[0].shape[1]
    tm = min(tm, n)
    in_specs = [pl.BlockSpec((1, tm, a.shape[2]), lambda bi, i: (bi, i, 0)) for a in a_list]
    in_specs += [_resident(w.shape) for w in w_list]
    args = list(a_list) + list(w_list)
    if res is not None:
        in_specs += [pl.BlockSpec((1, tm, nout), lambda bi, i: (bi, i, 0)),
                     pl.BlockSpec((1, 1, nout), lambda bi, i: (bi, 0, 0))]
        args += [res, gate]
    return pl.pallas_call(
        functools.partial(_rowmm_kernel, n_in=len(a_list), has_res=res is not None),
        grid=(b, n // tm),
        in_specs=in_specs,
        out_specs=pl.BlockSpec((1, tm, nout), lambda bi, i: (bi, i, 0)),
        out_shape=jax.ShapeDtypeStruct((b, n, nout), out_dtype),
        compiler_params=_params("parallel", "parallel"),
        name="rowmm",
    )(*args)


def _rope_cols(t, c_ref, s_ref, rope):
    if not rope:
        return t
    return t * c_ref[...] + pltpu.roll(t, QK_ROPE, 1) * s_ref[...]


def _qproj_kernel(p_ref, g_ref, w_ref, c_ref, s_ref, o_ref, *, rope):
    z = _rms(p_ref[0], g_ref[...]).astype(BF16)
    width = QK_NOPE + 2 * QK_ROPE
    acc = jnp.dot(z, w_ref[...], preferred_element_type=F32)
    for h in range(N_HEADS):
        o_ref[0, h, :, :QK_NOPE] = (acc[:, h * width:h * width + QK_NOPE] * Q_SCALE).astype(o_ref.dtype)
        r = _rope_cols(acc[:, h * width + QK_NOPE:(h + 1) * width], c_ref, s_ref, rope)
        o_ref[0, h, :, QK_NOPE:] = (r[:, :QK_ROPE] * Q_SCALE).astype(o_ref.dtype)


def _qproj(p, q_norm, wq, cos2, sin2, rope):
    b, n, _ = p.shape
    tm = min(512, n)
    tab = pl.BlockSpec((tm, 2 * QK_ROPE), lambda bi, i: (i, 0))
    return pl.pallas_call(
        functools.partial(_qproj_kernel, rope=rope),
        grid=(b, n // tm),
        in_specs=[pl.BlockSpec((1, tm, Q_RANK), lambda bi, i: (bi, i, P_QLAT // Q_RANK)),
                  pl.BlockSpec((1, Q_RANK), lambda bi, i: (0, 0)),
                  _resident(wq.shape), tab, tab],
        out_specs=pl.BlockSpec((1, N_HEADS, tm, QK_HEAD), lambda bi, i: (bi, 0, i, 0)),
        out_shape=jax.ShapeDtypeStruct((b, N_HEADS, n, QK_HEAD), BF16),
        compiler_params=_params("parallel", "parallel"),
        name="qproj",
    )(p, q_norm.reshape(1, Q_RANK), wq, cos2, sin2)


def _kvproj_kernel(px_ref, pc_ref, g_ref, wk_ref, wvt_ref, c_ref, s_ref, k_ref, vt_ref, kmax_ref):
    @pl.when(pl.program_id(1) == 0)
    def _():
        kmax_ref[...] = jnp.zeros(kmax_ref.shape, F32)

    def emit(blk, rope):
        z = _rms(blk[:, :KV_RANK], g_ref[...]).astype(BF16)
        kr = _rope_cols(blk[:, KV_RANK:], c_ref, s_ref, rope)[:, :QK_ROPE].astype(k_ref.dtype)
        kn = jnp.dot(z, wk_ref[...], preferred_element_type=F32).astype(k_ref.dtype)
        vt = lax.dot_general(wvt_ref[...], z, (((1,), (1,)), ((), ())),
                             preferred_element_type=F32).astype(vt_ref.dtype)
        kn2 = kn.astype(F32) * kn.astype(F32)
        kr2 = jnp.sum(kr.astype(F32) * kr.astype(F32), axis=1, keepdims=True)
        for h in range(N_HEADS):
            k_ref[0, h, :, :QK_NOPE] = kn[:, h * QK_NOPE:(h + 1) * QK_NOPE]
            k_ref[0, h, :, QK_NOPE:] = kr
            vt_ref[0, h] = vt[h * V_HEAD:(h + 1) * V_HEAD, :]
            norm2 = jnp.sum(kn2[:, h * QK_NOPE:(h + 1) * QK_NOPE], axis=1, keepdims=True) + kr2
            kmax_ref[0, h] = jnp.maximum(kmax_ref[0, h], jnp.max(norm2, axis=0, keepdims=True))

    is_ctx = pl.program_id(1) == pl.num_programs(1) - 1

    @pl.when(jnp.logical_not(is_ctx))
    def _():
        emit(px_ref[0], True)

    @pl.when(is_ctx)
    def _():
        emit(pc_ref[0], False)


def _kvproj(px, pc, kv_norm, wk, wvt, cos2, sin2):
    b, n, _ = px.shape
    tm = pc.shape[1]
    assert n % tm == 0
    nt = n // tm
    tab = pl.BlockSpec((tm, 2 * QK_ROPE), lambda bi, i: (jnp.minimum(i, nt - 1), 0))
    return pl.pallas_call(
        _kvproj_kernel,
        grid=(b, nt + 1),
        in_specs=[pl.BlockSpec((1, tm, KV_BLOCK), lambda bi, i: (bi, jnp.minimum(i, nt - 1), P_KV // KV_BLOCK)),
                  pl.BlockSpec((1, tm, KV_BLOCK), lambda bi, i: (bi, 0, P_KV // KV_BLOCK)),
                  pl.BlockSpec((1, KV_RANK), lambda bi, i: (0, 0)),
                  _resident(wk.shape), _resident(wvt.shape), tab, tab],
        out_specs=[pl.BlockSpec((1, N_HEADS, tm, QK_HEAD), lambda bi, i: (bi, 0, i, 0)),
                   pl.BlockSpec((1, N_HEADS, V_HEAD, tm), lambda bi, i: (bi, 0, 0, i)),
                   pl.BlockSpec((1, N_HEADS, 8, 128), lambda bi, i: (bi, 0, 0, 0))],
        out_shape=[jax.ShapeDtypeStruct((b, N_HEADS, n + tm, QK_HEAD), BF16),
                   jax.ShapeDtypeStruct((b, N_HEADS, V_HEAD, n + tm), BF16),
                   jax.ShapeDtypeStruct((b, N_HEADS, 8, 128), F32)],
        compiler_params=_params("parallel", "arbitrary"),
        name="kvproj",
    )(px, pc, kv_norm.reshape(1, KV_RANK), wk, wvt, cos2, sin2)


def _attn_kernel(q_ref, k_ref, vt_ref, kmax_ref, o_ref, *, tk_fixed, tk_online):
    q = q_ref[0, 0]
    tq = q.shape[0]
    kv_len = k_ref.shape[2]
    nt = (((1,), (1,)), ((), ()))

    def scores(j, tk):
        return lax.dot_general(k_ref[0, 0, j * tk:(j + 1) * tk, :], q, nt, preferred_element_type=F32)

    def weighted_values(j, tk, p):
        return jnp.dot(vt_ref[0, 0, :, j * tk:(j + 1) * tk], p.astype(BF16), preferred_element_type=F32)

    def store(acc, l):
        o_ref[0] = jnp.transpose(acc / l).astype(o_ref.dtype)

    qf = q.astype(F32)
    qn2 = lax.dot_general(jnp.ones((8, q.shape[1]), BF16), (qf * qf).astype(BF16), nt,
                          preferred_element_type=F32)[:1]
    ref = jnp.sqrt(qn2 * kmax_ref[0, 0, :1, :1]) * ATTN_BOUND_SLACK
    l = jnp.zeros((1, tq), F32)
    acc = jnp.zeros((V_HEAD, tq), F32)
    for j in range(kv_len // tk_fixed):
        p = jnp.exp2(scores(j, tk_fixed) - ref)
        l = l + jnp.sum(p, axis=0, keepdims=True)
        acc = acc + weighted_values(j, tk_fixed, p)
    store(acc, l)

    @pl.when(jnp.logical_not(jnp.min(l) >= ATTN_MIN_DENOM))
    def _():
        n_blocks = kv_len // tk_online
        m = jnp.full((1, tq), -jnp.inf, F32)
        l = jnp.zeros((1, tq), F32)
        acc = jnp.zeros((V_HEAD, tq), F32)
        st_next = scores(0, tk_online)
        for j in range(n_blocks):
            st = st_next
            if j + 1 < n_blocks:
                st_next = scores(j + 1, tk_online)
            m_new = jnp.maximum(m, jnp.max(st, axis=0, keepdims=True))
            alpha = jnp.exp2(m - m_new)
            p = jnp.exp2(st - m_new)
            l = alpha * l + jnp.sum(p, axis=0, keepdims=True)
            acc = alpha * acc + weighted_values(j, tk_online, p)
            m = m_new
        store(acc, l)


def _attention(q, k_all, vt_all, kmax, kv_len, kv_block0):
    b, h, n, _ = q.shape
    tq = min(ATTN_TQ, n)
    tk_fixed, tk_online = min(ATTN_TK_FIXED, kv_len), min(ATTN_TK_ONLINE, kv_len)
    assert kv_len % tk_fixed == 0 and kv_len % tk_online == 0
    return pl.pallas_call(
        functools.partial(_attn_kernel, tk_fixed=tk_fixed, tk_online=tk_online),
        grid=(b, h, n // tq),
        in_specs=[pl.BlockSpec((1, 1, tq, QK_HEAD), lambda bi, hi, i: (bi, hi, i, 0)),
                  pl.BlockSpec((1, 1, kv_len, QK_HEAD), lambda bi, hi, i: (bi, hi, kv_block0, 0)),
                  pl.BlockSpec((1, 1, V_HEAD, kv_len), lambda bi, hi, i: (bi, hi, 0, kv_block0)),
                  pl.BlockSpec((1, 1, 8, 128), lambda bi, hi, i: (bi, hi, 0, 0))],
        out_specs=pl.BlockSpec((1, tq, V_HEAD), lambda bi, hi, i: (bi, i, hi)),
        out_shape=jax.ShapeDtypeStruct((b, n, h * V_HEAD), BF16),
        compiler_params=_params("parallel", "parallel", "parallel"),
        name="attention",
    )(q, k_all, vt_all, kmax)


def _pool_kernel(u_ref, prev_ref, next_ref, w_ref, s_ref, o_ref, ext_sc, *, n, tm):
    i = pl.program_id(1)
    keep_prev = jnp.where(i > 0, 1.0, 0.0)
    keep_next = jnp.where(i < pl.num_programs(1) - 1, 1.0, 0.0)
    ext_sc[0:POOL_HALO] = prev_ref[0] * keep_prev
    ext_sc[POOL_HALO:POOL_HALO + tm] = u_ref[0]
    ext_sc[POOL_HALO + tm:] = next_ref[0] * keep_next
    t = i * tm + lax.broadcasted_iota(jnp.int32, (tm, 1), 0)
    for g, win in enumerate(POOL_WINDOWS):
        half = win // 2
        cols = slice(g * POOL_GROUP, (g + 1) * POOL_GROUP)
        tot = ext_sc[POOL_HALO - half:POOL_HALO - half + tm, cols]
        for d in range(1 - half, half):
            tot = tot + ext_sc[POOL_HALO + d:POOL_HALO + d + tm, cols]
        cnt = (jnp.minimum(t + half, n) - jnp.maximum(t - half, 0)).astype(F32)
        pooled = (tot / cnt - ext_sc[POOL_HALO:POOL_HALO + tm, cols]).astype(BF16)
        y = jnp.dot(pooled, w_ref[g], preferred_element_type=F32) * s_ref[:, cols]
        o_ref[0, :, cols] = y.astype(o_ref.dtype)


def _pool(p, w_pool, pool_scale):
    b, n, _ = p.shape
    tm = min(512, n)
    hb = tm // POOL_HALO
    last_hb = n // POOL_HALO - 1
    return pl.pallas_call(
        functools.partial(_pool_kernel, n=n, tm=tm),
        grid=(b, n // tm),
        in_specs=[pl.BlockSpec((1, tm, POOL_WIDTH), lambda bi, i: (bi, i, P_POOL)),
                  pl.BlockSpec((1, POOL_HALO, POOL_WIDTH), lambda bi, i: (bi, jnp.maximum(i * hb - 1, 0), P_POOL)),
                  pl.BlockSpec((1, POOL_HALO, POOL_WIDTH),
                               lambda bi, i: (bi, jnp.minimum((i + 1) * hb, last_hb), P_POOL)),
                  _resident(w_pool.shape),
                  pl.BlockSpec((1, POOL_WIDTH), lambda bi, i: (0, 0))],
        out_specs=pl.BlockSpec((1, tm, POOL_WIDTH), lambda bi, i: (bi, i, 0)),
        out_shape=jax.ShapeDtypeStruct((b, n, POOL_WIDTH), BF16),
        scratch_shapes=[pltpu.VMEM((tm + 2 * POOL_HALO, POOL_WIDTH), F32)],
        compiler_params=_params("parallel", "parallel"),
        name="pool",
    )(p, p, p, w_pool, pool_scale.reshape(1, POOL_WIDTH))


def _dft1_kernel(x_ref, a_ref, o_ref):
    tc = x_ref.shape[-1]
    x = x_ref[0].reshape(DFT_N1 * DFT_ROWS, tc)
    y = jnp.dot(a_ref[...], x, preferred_element_type=F32)
    o_ref[0] = y.astype(o_ref.dtype).reshape(2, DFT_N1, DFT_ROWS, tc)


def _dft_stage1(h, a_kron):
    b, n, d = h.shape
    tc = 1024
    h4 = h.reshape(b, DFT_N1, DFT_N2, d)
    return pl.pallas_call(
        _dft1_kernel,
        grid=(b, DFT_N2 // DFT_ROWS, d // tc),
        in_specs=[pl.BlockSpec((1, DFT_N1, DFT_ROWS, tc), lambda bi, r, c: (bi, 0, r, c)),
                  _resident(a_kron.shape)],
        out_specs=pl.BlockSpec((1, 2, DFT_N1, DFT_ROWS, tc), lambda bi, r, c: (bi, 0, 0, r, c)),
        out_shape=jax.ShapeDtypeStruct((b, 2, DFT_N1, DFT_N2, d), BF16),
        compiler_params=_params("parallel", "parallel", "parallel"),
        name="dft_stage1",
    )(h4, a_kron)


def _dft2_kernel(y_ref, m_ref, cc_ref, sc_ref, o_ref, *, rows_out, seq_len):
    y = y_ref[...].reshape(-1, D_MODEL)
    pq = jnp.dot(m_ref[0], y, preferred_element_type=F32).astype(BF16)
    ortho = 1.0 / math.sqrt(seq_len * FOURIER_GROUP)
    for g in range(FOURIER_GROUPS):
        cols = slice(g * FOURIER_GROUP, (g + 1) * FOURIER_GROUP)
        f = (jnp.dot(pq[:rows_out, cols], cc_ref[...], preferred_element_type=F32)
             - jnp.dot(pq[rows_out:, cols], sc_ref[...], preferred_element_type=F32))
        o_ref[0, :, cols] = (f * ortho).astype(o_ref.dtype)


def _dft_stage2(y, m, cc, sc):
    b = y.shape[0]
    out = pl.pallas_call(
        functools.partial(_dft2_kernel, rows_out=DFT_N2, seq_len=DFT_N1 * DFT_N2),
        grid=(b, DFT_N1),
        in_specs=[pl.BlockSpec((1, 2, 1, DFT_N2, D_MODEL), lambda bi, k: (bi, 0, k, 0, 0)),
                  pl.BlockSpec((1, 2 * DFT_N2, 2 * DFT_N2), lambda bi, k: (k, 0, 0)),
                  _resident(cc.shape), _resident(sc.shape)],
        out_specs=pl.BlockSpec((1, DFT_N2, D_MODEL), lambda bi, k: (bi, 0, k)),
        out_shape=jax.ShapeDtypeStruct((b, DFT_N2, DFT_N1 * D_MODEL), BF16),
        compiler_params=_params("parallel", "parallel"),
        name="dft_stage2",
    )(y, m, cc, sc)
    return out.reshape(b, DFT_N1 * DFT_N2, D_MODEL)


def _dft_small(h, m, cc, sc):
    b, n, d = h.shape
    return pl.pallas_call(
        functools.partial(_dft2_kernel, rows_out=n, seq_len=n),
        grid=(b,),
        in_specs=[pl.BlockSpec((1, n, d), lambda bi: (bi, 0, 0)),
                  _resident(m.shape), _resident(cc.shape), _resident(sc.shape)],
        out_specs=pl.BlockSpec((1, n, d), lambda bi: (bi, 0, 0)),
        out_shape=jax.ShapeDtypeStruct((b, n, d), BF16),
        compiler_params=_params("parallel"),
        name="dft_small",
    )(h, m, cc, sc)


def _cos_sin(num, den):
    ang = (num % den).astype(F32) * (2.0 * math.pi / den)
    return jnp.cos(ang), jnp.sin(ang)


def _dft_tables(n_ctx):
    i1 = jnp.arange(DFT_N1, dtype=jnp.int32)
    c1, s1 = _cos_sin(i1[:, None] * i1[None, :], DFT_N1)
    a = jnp.concatenate([c1, -s1], axis=0)
    eye = jnp.eye(DFT_ROWS, dtype=F32)
    a_kron = jnp.einsum('pt,jk->pjtk', a, eye).reshape(2 * DFT_N1 * DFT_ROWS, DFT_N1 * DFT_ROWS)
    n = DFT_N1 * DFT_N2
    t2 = jnp.arange(DFT_N2, dtype=jnp.int32)
    k = i1[:, None, None] + DFT_N1 * t2[None, :, None]
    ck, sk = _cos_sin(k * t2[None, None, :], n)
    m = jnp.concatenate([jnp.concatenate([ck, sk], axis=2),
                         jnp.concatenate([sk, -ck], axis=2)], axis=1)
    ic = jnp.arange(FOURIER_GROUP, dtype=jnp.int32)
    cc, sc = _cos_sin(ic[:, None] * ic[None, :], FOURIER_GROUP)
    il = jnp.arange(n_ctx, dtype=jnp.int32)
    cl, sl = _cos_sin(il[:, None] * il[None, :], n_ctx)
    m_ctx = jnp.concatenate([cl, sl], axis=0)[None]

    return dict(a_kron=a_kron.astype(BF16), m=m.astype(BF16), m_ctx=m_ctx.astype(BF16),
                cc=cc.astype(BF16), sc=sc.astype(BF16))


def _mlp_kernel(x_ref, g_ref, sh_ref, sc_ref, gate_ref, w1_ref, w2_ref, *rest, final):
    o_ref, h_sc, acc_sc = rest[-3:]
    j = pl.program_id(2)
    last = pl.num_programs(2) - 1
    tm = x_ref.shape[1]
    chunks = [slice(r, r + MLP_ROWS) for r in range(0, tm, MLP_ROWS)]

    def hidden(rows):
        a = jnp.maximum(jnp.dot(h_sc[rows, :], w1_ref[...], preferred_element_type=F32), 0.0)
        return (a * a).astype(BF16)

    @pl.when(j == 0)
    def _():
        parts = []
        for rows in chunks:
            _normmod_rows(x_ref, g_ref, sh_ref, sc_ref, h_sc, rows.start, rows.stop)
            parts.append(hidden(rows))
        acc_sc[...] = jnp.dot(jnp.concatenate(parts, axis=0), w2_ref[...], preferred_element_type=F32)

    @pl.when(jnp.logical_and(j > 0, j < last))
    def _():
        acc_sc[...] += jnp.dot(hidden(slice(None)), w2_ref[...], preferred_element_type=F32)

    @pl.when(j == last)
    def _():
        gate = gate_ref[0]
        a = hidden(slice(None))
        for rows in chunks:
            upd = acc_sc[rows, :] + jnp.dot(a[rows, :], w2_ref[...], preferred_element_type=F32)
            for r in range(0, MLP_ROWS, NORM_ROWS):
                y = x_ref[0, rows.start + r:rows.start + r + NORM_ROWS, :] + gate * upd[r:r + NORM_ROWS, :]
                if final:
                    y = _rms(y, rest[0][...])
                o_ref[0, rows.start + r:rows.start + r + NORM_ROWS, :] = y


def _mlp(x, g, sh, sc, gate, w1, w2, final_g=None):
    b, n, d = x.shape
    tm = min(MLP_TM, n)
    tf = MLP_TF
    vec = pl.BlockSpec((1, 1, d), lambda bi, i, j: (bi, 0, 0))
    row = pl.BlockSpec((1, d), lambda bi, i, j: (0, 0))
    in_specs = [pl.BlockSpec((1, tm, d), lambda bi, i, j: (bi, i, 0)), row, vec, vec, vec,
                pl.BlockSpec((d, tf), lambda bi, i, j: (0, j)),
                pl.BlockSpec((tf, d), lambda bi, i, j: (j, 0))]
    args = [x, g.reshape(1, d), sh, sc, gate, w1, w2]
    if final_g is not None:
        in_specs.append(row)
        args.append(final_g.reshape(1, d))
    return pl.pallas_call(
        functools.partial(_mlp_kernel, final=final_g is not None),
        grid=(b, n // tm, D_FF // tf),
        in_specs=in_specs,
        out_specs=pl.BlockSpec((1, tm, d), lambda bi, i, j: (bi, i, 0)),
        out_shape=jax.ShapeDtypeStruct((b, n, d), F32),
        scratch_shapes=[pltpu.VMEM((tm, d), BF16), pltpu.VMEM((tm, d), F32)],
        compiler_params=_params("parallel", "parallel", "arbitrary"),
        name="mlp",
    )(*args)


def _rotate_half_axial(x):
    xr = x.reshape(x.shape[:-1] + (2, 2, ROPE_FREQS))
    return jnp.concatenate([-xr[..., 1:, :], xr[..., :1, :]], axis=-2).reshape(x.shape)


def _rope_tables(n):
    rows = n // GRID_W
    r = jnp.broadcast_to(jnp.arange(rows, dtype=F32)[:, None], (rows, GRID_W)).reshape(n)
    col = jnp.broadcast_to(jnp.arange(GRID_W, dtype=F32)[None, :], (rows, GRID_W)).reshape(n)
    inv = ROPE_THETA ** (-2.0 * jnp.arange(ROPE_FREQS, dtype=F32) / ROPE_AXIS)
    ang = jnp.stack([r[:, None] * inv, col[:, None] * inv], axis=1)
    ang = jnp.broadcast_to(ang[:, :, None, :], (n, 2, 2, ROPE_FREQS)).reshape(n, QK_ROPE)
    zeros = jnp.zeros((n, QK_ROPE), F32)
    return (jnp.concatenate([jnp.cos(ang), zeros], axis=1), jnp.concatenate([jnp.sin(ang), zeros], axis=1))


def _even_weights(w_in, w_uq, w_ukv):
    w_kr = w_in[:, Q_RANK + KV_RANK:Q_RANK + KV_RANK + QK_ROPE]
    w_in2 = jnp.concatenate([w_in[:, Q_RANK + KV_RANK + QK_ROPE:], w_in[:, :Q_RANK],
                             w_in[:, Q_RANK:Q_RANK + KV_RANK], w_kr, _rotate_half_axial(w_kr)], axis=1)
    wq = w_uq.reshape(Q_RANK, N_HEADS, QK_HEAD)
    wq = jnp.concatenate([wq, _rotate_half_axial(wq[..., QK_NOPE:])], axis=-1)
    wq = wq.reshape(Q_RANK, N_HEADS * (QK_HEAD + QK_ROPE))
    wkv = w_ukv.reshape(KV_RANK, N_HEADS, QK_NOPE + V_HEAD)
    wk = wkv[..., :QK_NOPE].reshape(KV_RANK, N_HEADS * QK_NOPE)
    wvt = jnp.transpose(wkv[..., QK_NOPE:], (1, 2, 0)).reshape(N_HEADS * V_HEAD, KV_RANK)
    return w_in2.astype(BF16), wq.astype(BF16), wk.astype(BF16), wvt.astype(BF16)


def kernel(x, c, ctx, c_ctx, w_mod, b_mod, norm1, norm2, w_in, q_norm, w_uq, kv_norm, w_ukv, w_pool,
           pool_scale, w_out_even, w_out_odd, w_mlp1, w_mlp2, final_norm):
    b, n, d = x.shape
    n_ctx = ctx.shape[1]
    assert n == DFT_N1 * DFT_N2 and d == D_MODEL and n % GRID_W == 0
    n_keys = n + n_ctx
    assert n % 512 == 0 and n % n_ctx == 0

    cos2, sin2 = _rope_tables(n)
    dft = _dft_tables(n_ctx)
    w1 = w_mlp1.astype(BF16)
    w2 = w_mlp2.astype(BF16)

    cc = jnp.concatenate([c, c_ctx[None], jnp.zeros((8 - b - 1, d), F32)], axis=0)
    mods = _modulation(cc, w_mod, b_mod)

    updates_ctx = [False] * DEPTH
    for l in reversed(range(DEPTH - 1)):
        updates_ctx[l] = (l + 1) % 2 == 0 or updates_ctx[l + 1]

    for l in range(DEPTH):
        even = l % 2 == 0
        i = l // 2
        mx = [mods[l, :b, k * d:(k + 1) * d].reshape(b, 1, d) for k in range(6)]
        mc = [jnp.broadcast_to(mods[l, b, k * d:(k + 1) * d].reshape(1, 1, d), (b, 1, d)) for k in range(6)]
        ctx_update = updates_ctx[l]

        if even:
            w_in2, wq, wk, wvt = _even_weights(w_in[i], w_uq[i], w_ukv[i])
            w_out = w_out_even[i].astype(BF16)
            wp = w_pool[i].astype(BF16)
            px = _normproj(x, norm1[l], mx[0], mx[1], w_in2)
            pc = _normproj(ctx, norm1[l], mc[0], mc[1], w_in2)
            q = _qproj(px, q_norm[i], wq, cos2, sin2, rope=True)
            k_all, vt_all, kmax = _kvproj(px, pc, kv_norm[i], wk, wvt, cos2, sin2)
            attn_x = _attention(q, k_all, vt_all, kmax, n_keys, 0)
            pool_x = _pool(px, wp, pool_scale[i])
            x = _rowmm([attn_x, pool_x], [w_out[:MLA_OUT], w_out[MLA_OUT:]], F32, res=x, gate=mx[2])
            if ctx_update:
                cq = _qproj(pc, q_norm[i], wq, cos2, sin2, rope=False)
                attn_c = _attention(cq, k_all, vt_all, kmax, n_ctx, n // n_ctx)
                pool_c = _pool(pc, wp, pool_scale[i])
                ctx = _rowmm([attn_c, pool_c], [w_out[:MLA_OUT], w_out[MLA_OUT:]], F32, res=ctx, gate=mc[2])
        else:
            w_out = w_out_odd[i].astype(BF16)
            hx = _normmod(x, norm1[l], mx[0], mx[1])
            if ctx_update:
                hc = _normmod(ctx, norm1[l], mc[0], mc[1])
            fx = _dft_stage2(_dft_stage1(hx, dft['a_kron']), dft['m'], dft['cc'], dft['sc'])
            x = _rowmm([fx], [w_out], F32, res=x, gate=mx[2])
            if ctx_update:
                fc = _dft_small(hc, dft['m_ctx'], dft['cc'], dft['sc'])
                ctx = _rowmm([fc], [w_out], F32, res=ctx, gate=mc[2])
        x = _mlp(x, norm2[l], mx[3], mx[4], mx[5], w1[l], w2[l], final_norm if l == DEPTH - 1 else None)
        if ctx_update:
            ctx = _mlp(ctx.reshape(1, b * n_ctx, d), norm2[l], mc[3][:1], mc[4][:1], mc[5][:1],
                       w1[l], w2[l]).reshape(b, n_ctx, d)
    return x
```

```python
import functools
import math

import jax
import jax.numpy as jnp
import numpy as np
from jax import lax
from jax.experimental import pallas as pl
from jax.experimental.pallas import tpu as pltpu

D_MODEL = 2048
DEPTH = 4
GRID_W = 64
N_HEADS = 8
Q_RANK = 512
KV_RANK = 256
QK_NOPE = 128
QK_ROPE = 64
V_HEAD = 128
QK_HEAD = QK_NOPE + QK_ROPE
ATTN_SCALE = QK_HEAD ** -0.5
Q_SCALE = ATTN_SCALE * math.log2(math.e)
MLA_OUT = N_HEADS * V_HEAD
ROPE_AXIS = QK_ROPE // 2
ROPE_FREQS = ROPE_AXIS // 2
ROPE_THETA = 10000.0
POOL_WIDTH = D_MODEL // 2
POOL_WINDOWS = (2, 4, 8, 16)
POOL_GROUP = POOL_WIDTH // len(POOL_WINDOWS)
POOL_HALO = 8
FOURIER_GROUPS = 4
FOURIER_GROUP = D_MODEL // FOURIER_GROUPS
D_FF = 4 * D_MODEL
EPS = 1e-6

P_POOL = 0
P_QLAT = POOL_WIDTH
P_KV = POOL_WIDTH + Q_RANK
P_WIDTH = POOL_WIDTH + Q_RANK + KV_RANK + 2 * QK_ROPE
KV_BLOCK = KV_RANK + 2 * QK_ROPE

DFT_N1 = 64
DFT_N2 = 128
DFT_ROWS = 16
DFT_STRIDE_BLOCK = 8
DFT_COLS = 512

NORM_ROWS = 16
MLP_TM = 512
MLP_TF = 1024
MLP_ROWS = 128
ATTN_TQ = 1024
ATTN_TK_FIXED = 1408
ATTN_TK_ONLINE = 1408
ATTN_BOUND_SLACK = 1.02
ATTN_MIN_DENOM = 2.0 ** -60

V7X_VMEM_LIMIT =56 * 1024 * 1024

F32 = jnp.float32
BF16 = jnp.bfloat16


def _params(*sem):
    return pltpu.CompilerParams(dimension_semantics=sem, vmem_limit_bytes=V7X_VMEM_LIMIT)


def _resident(shape):
    nd = len(shape)
    return pl.BlockSpec(shape, lambda *_: (0,) * nd, pipeline_mode=pl.Buffered(1))


def _rms(x, g):
    return x * lax.rsqrt(jnp.mean(x * x, axis=-1, keepdims=True) + EPS) * g


def _mod_kernel(c_ref, w_ref, b_ref, o_ref):
    c = c_ref[...]
    s = c * (1.0 / (1.0 + jnp.exp(-c)))
    o_ref[0] = jnp.dot(s, w_ref[0], preferred_element_type=F32, precision=lax.Precision.HIGHEST) + b_ref[0]


def _modulation(cc, w_mod, b_mod):
    rows = cc.shape[0]
    tn = 1536
    n6 = 6 * D_MODEL
    return pl.pallas_call(
        _mod_kernel,
        grid=(DEPTH, n6 // tn),
        in_specs=[
            pl.BlockSpec((rows, D_MODEL), lambda l, j: (0, 0)),
            pl.BlockSpec((1, D_MODEL, tn), lambda l, j: (l, 0, j)),
            pl.BlockSpec((1, 1, tn), lambda l, j: (l, 0, j)),
        ],
        out_specs=pl.BlockSpec((1, rows, tn), lambda l, j: (l, 0, j)),
        out_shape=jax.ShapeDtypeStruct((DEPTH, rows, n6), F32),
        compiler_params=_params("parallel", "parallel"),
        name="modulation",
    )(cc, w_mod, b_mod.reshape(DEPTH, 1, n6))


def _normmod_rows(x_ref, g_ref, sh_ref, sc_ref, dst_ref, start=0, stop=None):
    gs = g_ref[...] * (1.0 + sc_ref[0])
    sh = sh_ref[0]
    for r in range(start, x_ref.shape[1] if stop is None else stop, NORM_ROWS):
        x = x_ref[0, r:r + NORM_ROWS, :]
        inv = lax.rsqrt(jnp.mean(x * x, axis=-1, keepdims=True) + EPS)
        dst_ref[r:r + NORM_ROWS, :] = (x * inv * gs + sh).astype(dst_ref.dtype)


def _normmod_kernel(x_ref, g_ref, sh_ref, sc_ref, o_ref):
    _normmod_rows(x_ref, g_ref, sh_ref, sc_ref, o_ref.at[0])


def _normmod(x, g, sh, sc, out_dtype=BF16):
    b, n, d = x.shape
    tm = min(512, n)
    vec = pl.BlockSpec((1, 1, d), lambda bi, i: (bi, 0, 0))
    return pl.pallas_call(
        _normmod_kernel,
        grid=(b, n // tm),
        in_specs=[pl.BlockSpec((1, tm, d), lambda bi, i: (bi, i, 0)),
                  pl.BlockSpec((1, d), lambda bi, i: (0, 0)), vec, vec],
        out_specs=pl.BlockSpec((1, tm, d), lambda bi, i: (bi, i, 0)),
        out_shape=jax.ShapeDtypeStruct((b, n, d), out_dtype),
        compiler_params=_params("parallel", "parallel"),
        name="normmod",
    )(x, g.reshape(1, d), sh, sc)


def _normproj_kernel(x_ref, g_ref, sh_ref, sc_ref, w_ref, o_ref, h_sc):
    for r in range(0, x_ref.shape[1], MLP_ROWS):
        _normmod_rows(x_ref, g_ref, sh_ref, sc_ref, h_sc, r, r + MLP_ROWS)
        o_ref[0, r:r + MLP_ROWS, :] = jnp.dot(h_sc[r:r + MLP_ROWS, :], w_ref[...],
                                              preferred_element_type=F32).astype(o_ref.dtype)


def _normproj(x, g, sh, sc, w, out_dtype=F32):
    b, n, d = x.shape
    tm = min(512, n)
    vec = pl.BlockSpec((1, 1, d), lambda bi, i: (bi, 0, 0))
    return pl.pallas_call(
        _normproj_kernel,
        grid=(b, n // tm),
        in_specs=[pl.BlockSpec((1, tm, d), lambda bi, i: (bi, i, 0)),
                  pl.BlockSpec((1, d), lambda bi, i: (0, 0)), vec, vec, _resident(w.shape)],
        out_specs=pl.BlockSpec((1, tm, w.shape[1]), lambda bi, i: (bi, i, 0)),
        out_shape=jax.ShapeDtypeStruct((b, n, w.shape[1]), out_dtype),
        scratch_shapes=[pltpu.VMEM((tm, d), BF16)],
        compiler_params=_params("parallel", "parallel"),
        name="normproj",
    )(x, g.reshape(1, d), sh, sc, w)


def _rowmm_kernel(*refs, n_in, has_res):
    a_refs, w_refs = refs[:n_in], refs[n_in:2 * n_in]
    o_ref = refs[-1]
    acc = None
    for a, w in zip(a_refs, w_refs):
        d = jnp.dot(a[0], w[...], preferred_element_type=F32)
        acc = d if acc is None else acc + d
    if has_res:
        res_ref, gate_ref = refs[2 * n_in], refs[2 * n_in + 1]
        acc = res_ref[0] + gate_ref[0] * acc
    o_ref[0] = acc.astype(o_ref.dtype)


def _rowmm(a_list, w_list, out_dtype, res=None, gate=None, tm=512):
    b, n, _ = a_list[0].shape
    nout = w_list[0].shape[1]
    tm = min(tm, n)
    in_specs = [pl.BlockSpec((1, tm, a.shape[2]), lambda bi, i: (bi, i, 0)) for a in a_list]
    in_specs += [_resident(w.shape) for w in w_list]
    args = list(a_list) + list(w_list)
    if res is not None:
        in_specs += [pl.BlockSpec((1, tm, nout), lambda bi, i: (bi, i, 0)),
                     pl.BlockSpec((1, 1, nout), lambda bi, i: (bi, 0, 0))]
        args += [res, gate]
    return pl.pallas_call(
        functools.partial(_rowmm_kernel, n_in=len(a_list), has_res=res is not None),
        grid=(b, n // tm),
        in_specs=in_specs,
        out_specs=pl.BlockSpec((1, tm, nout), lambda bi, i: (bi, i, 0)),
        out_shape=jax.ShapeDtypeStruct((b, n, nout), out_dtype),
        compiler_params=_params("parallel", "parallel"),
        name="rowmm",
    )(*args)


def _rowmm_strided_kernel(f_ref, w_ref, res_ref, gate_ref, o_ref):
    gate = gate_ref[0]
    d = w_ref.shape[0]
    n2 = f_ref.shape[1]
    steps = res_ref.shape[2]
    rows = jnp.concatenate([f_ref[0, :, s * d:(s + 1) * d] for s in range(steps)], axis=0)
    for c in range(0, d, DFT_COLS):
        cols = slice(c, c + DFT_COLS)
        y = jnp.dot(rows, w_ref[:, cols], preferred_element_type=F32)
        for s in range(steps):
            o_ref[0, :, s, cols] = res_ref[0, :, s, cols] + gate[:, cols] * y[s * n2:(s + 1) * n2, :]


def _rowmm_strided(f, w, res, gate):
    b, n, d = res.shape
    res4 = res.reshape(b, DFT_N2, DFT_N1, d)
    blk = pl.BlockSpec((1, DFT_N2, DFT_STRIDE_BLOCK, d), lambda bi, i: (bi, 0, i, 0))
    out = pl.pallas_call(
        _rowmm_strided_kernel,
        grid=(b, DFT_N1 // DFT_STRIDE_BLOCK),
        in_specs=[pl.BlockSpec((1, DFT_N2, DFT_STRIDE_BLOCK * d), lambda bi, i: (bi, 0, i)),
                  _resident(w.shape), blk,
                  pl.BlockSpec((1, 1, d), lambda bi, i: (bi, 0, 0))],
        out_specs=blk,
        out_shape=jax.ShapeDtypeStruct(res4.shape, F32),
        compiler_params=_params("parallel", "parallel"),
        name="rowmm_strided",
    )(f, w, res4, gate)
    return out.reshape(b, n, d)


def _rope_cols(t, c_ref, s_ref, rope):
    if not rope:
        return t
    return t * c_ref[...] + pltpu.roll(t, QK_ROPE, 1) * s_ref[...]


def _qproj_kernel(p_ref, g_ref, w_ref, c_ref, s_ref, o_ref, *, rope):
    z = _rms(p_ref[0], g_ref[...]).astype(BF16)
    width = QK_NOPE + 2 * QK_ROPE
    acc = jnp.dot(z, w_ref[...], preferred_element_type=F32)
    for h in range(N_HEADS):
        o_ref[0, h, :, :QK_NOPE] = (acc[:, h * width:h * width + QK_NOPE] * Q_SCALE).astype(o_ref.dtype)
        r = _rope_cols(acc[:, h * width + QK_NOPE:(h + 1) * width], c_ref, s_ref, rope)
        o_ref[0, h, :, QK_NOPE:] = (r[:, :QK_ROPE] * Q_SCALE).astype(o_ref.dtype)


def _qproj(p, q_norm, wq, cos2, sin2, rope):
    b, n, _ = p.shape
    tm = min(512, n)
    tab = pl.BlockSpec((tm, 2 * QK_ROPE), lambda bi, i: (i, 0))
    return pl.pallas_call(
        functools.partial(_qproj_kernel, rope=rope),
        grid=(b, n // tm),
        in_specs=[pl.BlockSpec((1, tm, Q_RANK), lambda bi, i: (bi, i, P_QLAT // Q_RANK)),
                  pl.BlockSpec((1, Q_RANK), lambda bi, i: (0, 0)),
                  _resident(wq.shape), tab, tab],
        out_specs=pl.BlockSpec((1, N_HEADS, tm, QK_HEAD), lambda bi, i: (bi, 0, i, 0)),
        out_shape=jax.ShapeDtypeStruct((b, N_HEADS, n, QK_HEAD), BF16),
        compiler_params=_params("parallel", "parallel"),
        name="qproj",
    )(p, q_norm.reshape(1, Q_RANK), wq, cos2, sin2)


def _kvproj_kernel(px_ref, pc_ref, g_ref, wk_ref, wvt_ref, c_ref, s_ref, k_ref, vt_ref, kmax_ref):
    @pl.when(pl.program_id(1) == 0)
    def _():
        kmax_ref[...] = jnp.zeros(kmax_ref.shape, F32)

    def emit(blk, rope):
        z = _rms(blk[:, :KV_RANK], g_ref[...]).astype(BF16)
        kr = _rope_cols(blk[:, KV_RANK:], c_ref, s_ref, rope)[:, :QK_ROPE].astype(k_ref.dtype)
        kn = jnp.dot(z, wk_ref[...], preferred_element_type=F32).astype(k_ref.dtype)
        vt = lax.dot_general(wvt_ref[...], z, (((1,), (1,)), ((), ())),
                             preferred_element_type=F32).astype(vt_ref.dtype)
        kn2 = kn.astype(F32) * kn.astype(F32)
        kr2 = jnp.sum(kr.astype(F32) * kr.astype(F32), axis=1, keepdims=True)
        for h in range(N_HEADS):
            k_ref[0, h, :, :QK_NOPE] = kn[:, h * QK_NOPE:(h + 1) * QK_NOPE]
            k_ref[0, h, :, QK_NOPE:] = kr
            vt_ref[0, h] = vt[h * V_HEAD:(h + 1) * V_HEAD, :]
            norm2 = jnp.sum(kn2[:, h * QK_NOPE:(h + 1) * QK_NOPE], axis=1, keepdims=True) + kr2
            kmax_ref[0, h] = jnp.maximum(kmax_ref[0, h], jnp.max(norm2, axis=0, keepdims=True))

    is_ctx = pl.program_id(1) == pl.num_programs(1) - 1

    @pl.when(jnp.logical_not(is_ctx))
    def _():
        emit(px_ref[0], True)

    @pl.when(is_ctx)
    def _():
        emit(pc_ref[0], False)


def _kvproj(px, pc, kv_norm, wk, wvt, cos2, sin2):
    b, n, _ = px.shape
    tm = pc.shape[1]
    assert n % tm == 0
    nt = n // tm
    tab = pl.BlockSpec((tm, 2 * QK_ROPE), lambda bi, i: (jnp.minimum(i, nt - 1), 0))
    return pl.pallas_call(
        _kvproj_kernel,
        grid=(b, nt + 1),
        in_specs=[pl.BlockSpec((1, tm, KV_BLOCK), lambda bi, i: (bi, jnp.minimum(i, nt - 1), P_KV // KV_BLOCK)),
                  pl.BlockSpec((1, tm, KV_BLOCK), lambda bi, i: (bi, 0, P_KV // KV_BLOCK)),
                  pl.BlockSpec((1, KV_RANK), lambda bi, i: (0, 0)),
                  _resident(wk.shape), _resident(wvt.shape), tab, tab],
        out_specs=[pl.BlockSpec((1, N_HEADS, tm, QK_HEAD), lambda bi, i: (bi, 0, i, 0)),
                   pl.BlockSpec((1, N_HEADS, V_HEAD, tm), lambda bi, i: (bi, 0, 0, i)),
                   pl.BlockSpec((1, N_HEADS, 8, 128), lambda bi, i: (bi, 0, 0, 0))],
        out_shape=[jax.ShapeDtypeStruct((b, N_HEADS, n + tm, QK_HEAD), BF16),
                   jax.ShapeDtypeStruct((b, N_HEADS, V_HEAD, n + tm), BF16),
                   jax.ShapeDtypeStruct((b, N_HEADS, 8, 128), F32)],
        compiler_params=_params("parallel", "arbitrary"),
        name="kvproj",
    )(px, pc, kv_norm.reshape(1, KV_RANK), wk, wvt, cos2, sin2)


def _attn_kernel(q_ref, k_ref, vt_ref, kmax_ref, o_ref, *, tk_fixed, tk_online):
    q = q_ref[0, 0]
    tq = q.shape[0]
    kv_len = k_ref.shape[2]
    nt = (((1,), (1,)), ((), ()))

    def scores(j, tk):
        return lax.dot_general(k_ref[0, 0, j * tk:(j + 1) * tk, :], q, nt, preferred_element_type=F32)

    def weighted_values(j, tk, p):
        return jnp.dot(vt_ref[0, 0, :, j * tk:(j + 1) * tk], p.astype(BF16), preferred_element_type=F32)

    def store(acc, l):
        o_ref[0] = jnp.transpose(acc / l).astype(o_ref.dtype)

    qf = q.astype(F32)
    qn2 = lax.dot_general(jnp.ones((8, q.shape[1]), BF16), (qf * qf).astype(BF16), nt,
                          preferred_element_type=F32)[:1]
    ref = jnp.sqrt(qn2 * kmax_ref[0, 0, :1, :1]) * ATTN_BOUND_SLACK
    l = jnp.zeros((1, tq), F32)
    acc = jnp.zeros((V_HEAD, tq), F32)
    for j in range(kv_len // tk_fixed):
        p = jnp.exp2(scores(j, tk_fixed) - ref)
        l = l + jnp.sum(p, axis=0, keepdims=True)
        acc = acc + weighted_values(j, tk_fixed, p)
    store(acc, l)

    @pl.when(jnp.logical_not(jnp.min(l) >= ATTN_MIN_DENOM))
    def _():
        n_blocks = kv_len // tk_online
        m = jnp.full((1, tq), -jnp.inf, F32)
        l = jnp.zeros((1, tq), F32)
        acc = jnp.zeros((V_HEAD, tq), F32)
        st_next = scores(0, tk_online)
        for j in range(n_blocks):
            st = st_next
            if j + 1 < n_blocks:
                st_next = scores(j + 1, tk_online)
            m_new = jnp.maximum(m, jnp.max(st, axis=0, keepdims=True))
            alpha = jnp.exp2(m - m_new)
            p = jnp.exp2(st - m_new)
            l = alpha * l + jnp.sum(p, axis=0, keepdims=True)
            acc = alpha * acc + weighted_values(j, tk_online, p)
            m = m_new
        store(acc, l)


def _attention(q, k_all, vt_all, kmax, kv_len, kv_block0):
    b, h, n, _ = q.shape
    tq = min(ATTN_TQ, n)
    tk_fixed, tk_online = min(ATTN_TK_FIXED, kv_len), min(ATTN_TK_ONLINE, kv_len)
    assert kv_len % tk_fixed == 0 and kv_len % tk_online == 0
    return pl.pallas_call(
        functools.partial(_attn_kernel, tk_fixed=tk_fixed, tk_online=tk_online),
        grid=(b, h, n // tq),
        in_specs=[pl.BlockSpec((1, 1, tq, QK_HEAD), lambda bi, hi, i: (bi, hi, i, 0)),
                  pl.BlockSpec((1, 1, kv_len, QK_HEAD), lambda bi, hi, i: (bi, hi, kv_block0, 0)),
                  pl.BlockSpec((1, 1, V_HEAD, kv_len), lambda bi, hi, i: (bi, hi, 0, kv_block0)),
                  pl.BlockSpec((1, 1, 8, 128), lambda bi, hi, i: (bi, hi, 0, 0))],
        out_specs=pl.BlockSpec((1, tq, V_HEAD), lambda bi, hi, i: (bi, i, hi)),
        out_shape=jax.ShapeDtypeStruct((b, n, h * V_HEAD), BF16),
        compiler_params=_params("parallel", "parallel", "parallel"),
        name="attention",
    )(q, k_all, vt_all, kmax)


def _pool_kernel(u_ref, prev_ref, next_ref, w_ref, s_ref, o_ref, ext_sc, *, n, tm):
    i = pl.program_id(1)
    keep_prev = jnp.where(i > 0, 1.0, 0.0)
    keep_next = jnp.where(i < pl.num_programs(1) - 1, 1.0, 0.0)
    ext_sc[0:POOL_HALO] = prev_ref[0] * keep_prev
    ext_sc[POOL_HALO:POOL_HALO + tm] = u_ref[0]
    ext_sc[POOL_HALO + tm:] = next_ref[0] * keep_next
    t = i * tm + lax.broadcasted_iota(jnp.int32, (tm, 1), 0)
    for g, win in enumerate(POOL_WINDOWS):
        half = win // 2
        cols = slice(g * POOL_GROUP, (g + 1) * POOL_GROUP)
        tot = ext_sc[POOL_HALO - half:POOL_HALO - half + tm, cols]
        for d in range(1 - half, half):
            tot = tot + ext_sc[POOL_HALO + d:POOL_HALO + d + tm, cols]
        cnt = (jnp.minimum(t + half, n) - jnp.maximum(t - half, 0)).astype(F32)
        pooled = (tot / cnt - ext_sc[POOL_HALO:POOL_HALO + tm, cols]).astype(BF16)
        y = jnp.dot(pooled, w_ref[g], preferred_element_type=F32) * s_ref[:, cols]
        o_ref[0, :, cols] = y.astype(o_ref.dtype)


def _pool(p, w_pool, pool_scale):
    b, n, _ = p.shape
    tm = min(512, n)
    hb = tm // POOL_HALO
    last_hb = n // POOL_HALO - 1
    return pl.pallas_call(
        functools.partial(_pool_kernel, n=n, tm=tm),
        grid=(b, n // tm),
        in_specs=[pl.BlockSpec((1, tm, POOL_WIDTH), lambda bi, i: (bi, i, P_POOL)),
                  pl.BlockSpec((1, POOL_HALO, POOL_WIDTH), lambda bi, i: (bi, jnp.maximum(i * hb - 1, 0), P_POOL)),
                  pl.BlockSpec((1, POOL_HALO, POOL_WIDTH),
                               lambda bi, i: (bi, jnp.minimum((i + 1) * hb, last_hb), P_POOL)),
                  _resident(w_pool.shape),
                  pl.BlockSpec((1, POOL_WIDTH), lambda bi, i: (0, 0))],
        out_specs=pl.BlockSpec((1, tm, POOL_WIDTH), lambda bi, i: (bi, i, 0)),
        out_shape=jax.ShapeDtypeStruct((b, n, POOL_WIDTH), BF16),
        scratch_shapes=[pltpu.VMEM((tm + 2 * POOL_HALO, POOL_WIDTH), F32)],
        compiler_params=_params("parallel", "parallel"),
        name="pool",
    )(p, p, p, w_pool, pool_scale.reshape(1, POOL_WIDTH))


def _dft1_kernel(x_ref, g_ref, sh_ref, sc_ref, a_ref, o_ref, h_sc):
    gs = g_ref[...] * (1.0 + sc_ref[0])
    sh = sh_ref[0]
    for t1 in range(DFT_N1):
        x = x_ref[0, t1]
        inv = lax.rsqrt(jnp.mean(x * x, axis=-1, keepdims=True) + EPS)
        h_sc[t1 * DFT_ROWS:(t1 + 1) * DFT_ROWS, :] = (x * inv * gs + sh).astype(h_sc.dtype)
    for c in range(0, D_MODEL, DFT_COLS):
        y = jnp.dot(a_ref[...], h_sc[:, c:c + DFT_COLS], preferred_element_type=F32)
        o_ref[0, :, :, :, c:c + DFT_COLS] = y.astype(o_ref.dtype).reshape(2, DFT_N1, DFT_ROWS, DFT_COLS)


def _dft_stage1(x, g, sh, sc, a_kron):
    b, n, d = x.shape
    vec = pl.BlockSpec((1, 1, d), lambda bi, r: (bi, 0, 0))
    return pl.pallas_call(
        _dft1_kernel,
        grid=(b, DFT_N2 // DFT_ROWS),
        in_specs=[pl.BlockSpec((1, DFT_N1, DFT_ROWS, d), lambda bi, r: (bi, 0, r, 0)),
                  pl.BlockSpec((1, d), lambda bi, r: (0, 0)), vec, vec,
                  _resident(a_kron.shape)],
        out_specs=pl.BlockSpec((1, 2, DFT_N1, DFT_ROWS, d), lambda bi, r: (bi, 0, 0, r, 0)),
        out_shape=jax.ShapeDtypeStruct((b, 2, DFT_N1, DFT_N2, d), BF16),
        scratch_shapes=[pltpu.VMEM((DFT_N1 * DFT_ROWS, d), BF16)],
        compiler_params=_params("parallel", "parallel"),
        name="dft_stage1",
    )(x.reshape(b, DFT_N1, DFT_N2, d), g.reshape(1, d), sh, sc, a_kron)


def _dft2_kernel(y_ref, m_ref, cc_ref, sc_ref, o_ref, *, rows_out, seq_len):
    y = y_ref[...].reshape(-1, D_MODEL)
    pq = jnp.dot(m_ref[0], y, preferred_element_type=F32).astype(BF16)
    ortho = 1.0 / math.sqrt(seq_len * FOURIER_GROUP)
    for g in range(FOURIER_GROUPS):
        cols = slice(g * FOURIER_GROUP, (g + 1) * FOURIER_GROUP)
        f = (jnp.dot(pq[:rows_out, cols], cc_ref[...], preferred_element_type=F32)
             - jnp.dot(pq[rows_out:, cols], sc_ref[...], preferred_element_type=F32))
        o_ref[0, :, cols] = (f * ortho).astype(o_ref.dtype)


def _dft_stage2(y, m, cc, sc):
    b = y.shape[0]
    return pl.pallas_call(
        functools.partial(_dft2_kernel, rows_out=DFT_N2, seq_len=DFT_N1 * DFT_N2),
        grid=(b, DFT_N1),
        in_specs=[pl.BlockSpec((1, 2, 1, DFT_N2, D_MODEL), lambda bi, k: (bi, 0, k, 0, 0)),
                  pl.BlockSpec((1, 2 * DFT_N2, 2 * DFT_N2), lambda bi, k: (k, 0, 0)),
                  _resident(cc.shape), _resident(sc.shape)],
        out_specs=pl.BlockSpec((1, DFT_N2, D_MODEL), lambda bi, k: (bi, 0, k)),
        out_shape=jax.ShapeDtypeStruct((b, DFT_N2, DFT_N1 * D_MODEL), BF16),
        compiler_params=_params("parallel", "parallel"),
        name="dft_stage2",
    )(y, m, cc, sc)


def _dft_small(h, m, cc, sc):
    b, n, d = h.shape
    return pl.pallas_call(
        functools.partial(_dft2_kernel, rows_out=n, seq_len=n),
        grid=(b,),
        in_specs=[pl.BlockSpec((1, n, d), lambda bi: (bi, 0, 0)),
                  _resident(m.shape), _resident(cc.shape), _resident(sc.shape)],
        out_specs=pl.BlockSpec((1, n, d), lambda bi: (bi, 0, 0)),
        out_shape=jax.ShapeDtypeStruct((b, n, d), BF16),
        compiler_params=_params("parallel"),
        name="dft_small",
    )(h, m, cc, sc)


def _cos_sin(num, den):
    ang = (num % den).astype(F32) * (2.0 * math.pi / den)
    return jnp.cos(ang), jnp.sin(ang)


def _dft_tables(n_ctx):
    i1 = jnp.arange(DFT_N1, dtype=jnp.int32)
    c1, s1 = _cos_sin(i1[:, None] * i1[None, :], DFT_N1)
    a = jnp.concatenate([c1, -s1], axis=0)
    eye = jnp.eye(DFT_ROWS, dtype=F32)
    a_kron = jnp.einsum('pt,jk->pjtk', a, eye).reshape(2 * DFT_N1 * DFT_ROWS, DFT_N1 * DFT_ROWS)
    n = DFT_N1 * DFT_N2
    t2 = jnp.arange(DFT_N2, dtype=jnp.int32)
    k = i1[:, None, None] + DFT_N1 * t2[None, :, None]
    ck, sk = _cos_sin(k * t2[None, None, :], n)
    m = jnp.concatenate([jnp.concatenate([ck, sk], axis=2),
                         jnp.concatenate([sk, -ck], axis=2)], axis=1)
    ic = jnp.arange(FOURIER_GROUP, dtype=jnp.int32)
    cc, sc = _cos_sin(ic[:, None] * ic[None, :], FOURIER_GROUP)
    il = jnp.arange(n_ctx, dtype=jnp.int32)
    cl, sl = _cos_sin(il[:, None] * il[None, :], n_ctx)
    m_ctx = jnp.concatenate([cl, sl], axis=0)[None]

    return dict(a_kron=a_kron.astype(BF16), m=m.astype(BF16), m_ctx=m_ctx.astype(BF16),
                cc=cc.astype(BF16), sc=sc.astype(BF16))


def _mlp_kernel(x_ref, g_ref, sh_ref, sc_ref, gate_ref, w1_ref, w2_ref, *rest, final):
    o_ref, h_sc, acc_sc = rest[-3:]
    j = pl.program_id(2)
    last = pl.num_programs(2) - 1
    tm = x_ref.shape[1]
    chunks = [slice(r, r + MLP_ROWS) for r in range(0, tm, MLP_ROWS)]

    def hidden(rows):
        a = jnp.maximum(jnp.dot(h_sc[rows, :], w1_ref[...], preferred_element_type=F32), 0.0)
        return (a * a).astype(BF16)

    @pl.when(j == 0)
    def _():
        parts = []
        for rows in chunks:
            _normmod_rows(x_ref, g_ref, sh_ref, sc_ref, h_sc, rows.start, rows.stop)
            parts.append(hidden(rows))
        acc_sc[...] = jnp.dot(jnp.concatenate(parts, axis=0), w2_ref[...], preferred_element_type=F32)

    @pl.when(jnp.logical_and(j > 0, j < last))
    def _():
        acc_sc[...] += jnp.dot(hidden(slice(None)), w2_ref[...], preferred_element_type=F32)

    @pl.when(j == last)
    def _():
        gate = gate_ref[0]
        a = hidden(slice(None))
        for rows in chunks:
            upd = acc_sc[rows, :] + jnp.dot(a[rows, :], w2_ref[...], preferred_element_type=F32)
            for r in range(0, MLP_ROWS, NORM_ROWS):
                y = x_ref[0, rows.start + r:rows.start + r + NORM_ROWS, :] + gate * upd[r:r + NORM_ROWS, :]
                if final:
                    y = _rms(y, rest[0][...])
                o_ref[0, rows.start + r:rows.start + r + NORM_ROWS, :] = y


def _mlp(x, g, sh, sc, gate, w1, w2, layer, final_g=None):
    b, n, d = x.shape
    tm = min(MLP_TM, n)
    tf = MLP_TF
    vec = pl.BlockSpec((1, 1, d), lambda bi, i, j: (bi, 0, 0))
    row = pl.BlockSpec((1, d), lambda bi, i, j: (0, 0))
    in_specs = [pl.BlockSpec((1, tm, d), lambda bi, i, j: (bi, i, 0)), row, vec, vec, vec,
                pl.BlockSpec((None, d, tf), lambda bi, i, j: (layer, 0, j)),
                pl.BlockSpec((None, tf, d), lambda bi, i, j: (layer, j, 0))]
    args = [x, g.reshape(1, d), sh, sc, gate, w1, w2]
    if final_g is not None:
        in_specs.append(row)
        args.append(final_g.reshape(1, d))
    return pl.pallas_call(
        functools.partial(_mlp_kernel, final=final_g is not None),
        grid=(b, n // tm, D_FF // tf),
        in_specs=in_specs,
        out_specs=pl.BlockSpec((1, tm, d), lambda bi, i, j: (bi, i, 0)),
        out_shape=jax.ShapeDtypeStruct((b, n, d), F32),
        scratch_shapes=[pltpu.VMEM((tm, d), BF16), pltpu.VMEM((tm, d), F32)],
        compiler_params=_params("parallel", "parallel", "arbitrary"),
        name="mlp",
    )(*args)


def _rotate_half_axial(x):
    xr = x.reshape(x.shape[:-1] + (2, 2, ROPE_FREQS))
    return jnp.concatenate([-xr[..., 1:, :], xr[..., :1, :]], axis=-2).reshape(x.shape)


def _rope_tables(n):
    rows = n // GRID_W
    r = jnp.broadcast_to(jnp.arange(rows, dtype=F32)[:, None], (rows, GRID_W)).reshape(n)
    col = jnp.broadcast_to(jnp.arange(GRID_W, dtype=F32)[None, :], (rows, GRID_W)).reshape(n)
    inv = ROPE_THETA ** (-2.0 * jnp.arange(ROPE_FREQS, dtype=F32) / ROPE_AXIS)
    ang = jnp.stack([r[:, None] * inv, col[:, None] * inv], axis=1)
    ang = jnp.broadcast_to(ang[:, :, None, :], (n, 2, 2, ROPE_FREQS)).reshape(n, QK_ROPE)
    zeros = jnp.zeros((n, QK_ROPE), F32)
    return (jnp.concatenate([jnp.cos(ang), zeros], axis=1), jnp.concatenate([jnp.sin(ang), zeros], axis=1))


def _even_weights(w_in, w_uq, w_ukv):
    w_kr = w_in[:, Q_RANK + KV_RANK:Q_RANK + KV_RANK + QK_ROPE]
    w_in2 = jnp.concatenate([w_in[:, Q_RANK + KV_RANK + QK_ROPE:], w_in[:, :Q_RANK],
                             w_in[:, Q_RANK:Q_RANK + KV_RANK], w_kr, _rotate_half_axial(w_kr)], axis=1)
    wq = w_uq.reshape(Q_RANK, N_HEADS, QK_HEAD)
    wq = jnp.concatenate([wq, _rotate_half_axial(wq[..., QK_NOPE:])], axis=-1)
    wq = wq.reshape(Q_RANK, N_HEADS * (QK_HEAD + QK_ROPE))
    wkv = w_ukv.reshape(KV_RANK, N_HEADS, QK_NOPE + V_HEAD)
    wk = wkv[..., :QK_NOPE].reshape(KV_RANK, N_HEADS * QK_NOPE)
    wvt = jnp.transpose(wkv[..., QK_NOPE:], (1, 2, 0)).reshape(N_HEADS * V_HEAD, KV_RANK)
    return w_in2.astype(BF16), wq.astype(BF16), wk.astype(BF16), wvt.astype(BF16)


def kernel(x, c, ctx, c_ctx, w_mod, b_mod, norm1, norm2, w_in, q_norm, w_uq, kv_norm, w_ukv, w_pool,
           pool_scale, w_out_even, w_out_odd, w_mlp1, w_mlp2, final_norm):
    b, n, d = x.shape
    n_ctx = ctx.shape[1]
    assert n == DFT_N1 * DFT_N2 and d == D_MODEL and n % GRID_W == 0
    n_keys = n + n_ctx
    assert n % 512 == 0 and n % n_ctx == 0

    cos2, sin2 = _rope_tables(n)
    dft = _dft_tables(n_ctx)
    w1 = w_mlp1.astype(BF16)
    w2 = w_mlp2.astype(BF16)

    cc = jnp.concatenate([c, c_ctx[None], jnp.zeros((8 - b - 1, d), F32)], axis=0)
    mods = _modulation(cc, w_mod, b_mod)

    updates_ctx = [False] * DEPTH
    for l in reversed(range(DEPTH - 1)):
        updates_ctx[l] = (l + 1) % 2 == 0 or updates_ctx[l + 1]

    for l in range(DEPTH):
        even = l % 2 == 0
        i = l // 2
        mx = [mods[l, :b, k * d:(k + 1) * d].reshape(b, 1, d) for k in range(6)]
        mc = [jnp.broadcast_to(mods[l, b, k * d:(k + 1) * d].reshape(1, 1, d), (b, 1, d)) for k in range(6)]
        ctx_update = updates_ctx[l]

        if even:
            w_in2, wq, wk, wvt = _even_weights(w_in[i], w_uq[i], w_ukv[i])
            w_out = w_out_even[i].astype(BF16)
            wp = w_pool[i].astype(BF16)
            px = _normproj(x, norm1[l], mx[0], mx[1], w_in2)
            pc = _normproj(ctx, norm1[l], mc[0], mc[1], w_in2)
            q = _qproj(px, q_norm[i], wq, cos2, sin2, rope=True)
            k_all, vt_all, kmax = _kvproj(px, pc, kv_norm[i], wk, wvt, cos2, sin2)
            attn_x = _attention(q, k_all, vt_all, kmax, n_keys, 0)
            pool_x = _pool(px, wp, pool_scale[i])
            x = _rowmm([attn_x, pool_x], [w_out[:MLA_OUT], w_out[MLA_OUT:]], F32, res=x, gate=mx[2])
            if ctx_update:
                cq = _qproj(pc, q_norm[i], wq, cos2, sin2, rope=False)
                attn_c = _attention(cq, k_all, vt_all, kmax, n_ctx, n // n_ctx)
                pool_c = _pool(pc, wp, pool_scale[i])
                ctx = _rowmm([attn_c, pool_c], [w_out[:MLA_OUT], w_out[MLA_OUT:]], F32, res=ctx, gate=mc[2])
        else:
            w_out = w_out_odd[i].astype(BF16)
            if ctx_update:
                hc = _normmod(ctx, norm1[l], mc[0], mc[1])
            y1 = _dft_stage1(x, norm1[l], mx[0], mx[1], dft['a_kron'])
            fx = _dft_stage2(y1, dft['m'], dft['cc'], dft['sc'])
            x = _rowmm_strided(fx, w_out, x, mx[2])
            if ctx_update:
                fc = _dft_small(hc, dft['m_ctx'], dft['cc'], dft['sc'])
                ctx = _rowmm([fc], [w_out], F32, res=ctx, gate=mc[2])
        x = _mlp(x, norm2[l], mx[3], mx[4], mx[5], w1, w2, l, final_norm if l == DEPTH - 1 else None)
        if ctx_update:
            ctx = _mlp(ctx.reshape(1, b * n_ctx, d), norm2[l], mc[3][:1], mc[4][:1], mc[5][:1],
                       w1, w2, l).reshape(b, n_ctx, d)
    return x
```

```python
import functools
import math

import jax
import jax.numpy as jnp
import numpy as np
from jax import lax
from jax.experimental import pallas as pl
from jax.experimental.pallas import tpu as pltpu

D_MODEL = 2048
DEPTH = 4
GRID_W = 64
N_HEADS = 8
Q_RANK = 512
KV_RANK = 256
QK_NOPE = 128
QK_ROPE = 64
V_HEAD = 128
QK_HEAD = QK_NOPE + QK_ROPE
ATTN_SCALE = QK_HEAD ** -0.5
Q_SCALE = ATTN_SCALE * math.log2(math.e)
MLA_OUT = N_HEADS * V_HEAD
ROPE_AXIS = QK_ROPE // 2
ROPE_FREQS = ROPE_AXIS // 2
ROPE_THETA = 10000.0
POOL_WIDTH = D_MODEL // 2
POOL_WINDOWS = (2, 4, 8, 16)
POOL_GROUP = POOL_WIDTH // len(POOL_WINDOWS)
POOL_HALO = 8
FOURIER_GROUPS = 4
FOURIER_GROUP = D_MODEL // FOURIER_GROUPS
D_FF = 4 * D_MODEL
EPS = 1e-6

P_POOL = 0
P_QLAT = POOL_WIDTH
P_KV = POOL_WIDTH + Q_RANK
P_WIDTH = POOL_WIDTH + Q_RANK + KV_RANK + 2 * QK_ROPE
KV_BLOCK = KV_RANK + 2 * QK_ROPE

DFT_N1 = 64
DFT_N2 = 128
DFT_ROWS = 16
DFT2_K1 = 4
DFT_STRIDE_BLOCK = 8
DFT_COLS = 512

NORM_ROWS = 16
MLP_TM = 512
MLP_TF = 1024
MLP_ROWS = 128
ATTN_TQ = 1024
ATTN_TK_FIXED = 2816
ATTN_TK_ONLINE = 384
ATTN_BOUND_SLACK = 1.02
ATTN_MIN_DENOM = 2.0 ** -60

V7X_VMEM_LIMIT =56 * 1024 * 1024

F32 = jnp.float32
BF16 = jnp.bfloat16


def _params(*sem):
    return pltpu.CompilerParams(dimension_semantics=sem, vmem_limit_bytes=V7X_VMEM_LIMIT)


def _resident(shape):
    nd = len(shape)
    return pl.BlockSpec(shape, lambda *_: (0,) * nd, pipeline_mode=pl.Buffered(1))


def _rms(x, g):
    return x * lax.rsqrt(jnp.mean(x * x, axis=-1, keepdims=True) + EPS) * g


def _mod_kernel(c_ref, w_ref, b_ref, o_ref):
    c = c_ref[...]
    s = c * (1.0 / (1.0 + jnp.exp(-c)))
    o_ref[0] = jnp.dot(s, w_ref[0], preferred_element_type=F32, precision=lax.Precision.HIGHEST) + b_ref[0]


def _modulation(cc, w_mod, b_mod):
    rows = cc.shape[0]
    tn = 1536
    n6 = 6 * D_MODEL
    return pl.pallas_call(
        _mod_kernel,
        grid=(DEPTH, n6 // tn),
        in_specs=[
            pl.BlockSpec((rows, D_MODEL), lambda l, j: (0, 0)),
            pl.BlockSpec((1, D_MODEL, tn), lambda l, j: (l, 0, j)),
            pl.BlockSpec((1, 1, tn), lambda l, j: (l, 0, j)),
        ],
        out_specs=pl.BlockSpec((1, rows, tn), lambda l, j: (l, 0, j)),
        out_shape=jax.ShapeDtypeStruct((DEPTH, rows, n6), F32),
        compiler_params=_params("parallel", "parallel"),
        name="modulation",
    )(cc, w_mod, b_mod.reshape(DEPTH, 1, n6))


def _normmod_rows(x_ref, g_ref, sh_ref, sc_ref, dst_ref, start=0, stop=None):
    gs = g_ref[...] * (1.0 + sc_ref[0])
    sh = sh_ref[0]
    for r in range(start, x_ref.shape[1] if stop is None else stop, NORM_ROWS):
        x = x_ref[0, r:r + NORM_ROWS, :]
        inv = lax.rsqrt(jnp.mean(x * x, axis=-1, keepdims=True) + EPS)
        dst_ref[r:r + NORM_ROWS, :] = (x * inv * gs + sh).astype(dst_ref.dtype)


def _normmod_kernel(x_ref, g_ref, sh_ref, sc_ref, o_ref):
    _normmod_rows(x_ref, g_ref, sh_ref, sc_ref, o_ref.at[0])


def _normmod(x, g, sh, sc, out_dtype=BF16):
    b, n, d = x.shape
    tm = min(512, n)
    vec = pl.BlockSpec((1, 1, d), lambda bi, i: (bi, 0, 0))
    return pl.pallas_call(
        _normmod_kernel,
        grid=(b, n // tm),
        in_specs=[pl.BlockSpec((1, tm, d), lambda bi, i: (bi, i, 0)),
                  pl.BlockSpec((1, d), lambda bi, i: (0, 0)), vec, vec],
        out_specs=pl.BlockSpec((1, tm, d), lambda bi, i: (bi, i, 0)),
        out_shape=jax.ShapeDtypeStruct((b, n, d), out_dtype),
        compiler_params=_params("parallel", "parallel"),
        name="normmod",
    )(x, g.reshape(1, d), sh, sc)


def _rope_cols(t, cos2, sin2, rope):
    if not rope:
        return t
    return t * cos2 + pltpu.roll(t, QK_ROPE, 1) * sin2


def _evenproj_kernel(x_ref, g_ref, sh_ref, sc_ref, w_ref, qg_ref, wq_ref, c_ref, s_ref,
                     u_ref, kv_ref, q_ref, h_sc, *, rope):
    width = QK_NOPE + 2 * QK_ROPE
    for r in range(0, x_ref.shape[1], MLP_ROWS):
        rows = slice(r, r + MLP_ROWS)
        _normmod_rows(x_ref, g_ref, sh_ref, sc_ref, h_sc, r, r + MLP_ROWS)
        p = jnp.dot(h_sc[rows, :], w_ref[...], preferred_element_type=F32)
        u_ref[0, rows, :] = p[:, P_POOL:P_POOL + POOL_WIDTH]
        kv_ref[0, rows, :] = p[:, P_KV:]
        z = _rms(p[:, P_QLAT:P_QLAT + Q_RANK], qg_ref[...]).astype(BF16)
        acc = jnp.dot(z, wq_ref[...], preferred_element_type=F32)
        for h in range(N_HEADS):
            q_ref[0, h, rows, :QK_NOPE] = (acc[:, h * width:h * width + QK_NOPE] * Q_SCALE).astype(q_ref.dtype)
            t = _rope_cols(acc[:, h * width + QK_NOPE:(h + 1) * width], c_ref[rows, :], s_ref[rows, :], rope)
            q_ref[0, h, rows, QK_NOPE:] = (t[:, :QK_ROPE] * Q_SCALE).astype(q_ref.dtype)


def _evenproj(x, g, sh, sc, w_in2, q_norm, wq, cos2, sin2, rope):
    b, n, d = x.shape
    tm = min(512, n)
    vec = pl.BlockSpec((1, 1, d), lambda bi, i: (bi, 0, 0))
    tab = pl.BlockSpec((tm, 2 * QK_ROPE), lambda bi, i: (i, 0))
    return pl.pallas_call(
        functools.partial(_evenproj_kernel, rope=rope),
        grid=(b, n // tm),
        in_specs=[pl.BlockSpec((1, tm, d), lambda bi, i: (bi, i, 0)),
                  pl.BlockSpec((1, d), lambda bi, i: (0, 0)), vec, vec, _resident(w_in2.shape),
                  pl.BlockSpec((1, Q_RANK), lambda bi, i: (0, 0)), _resident(wq.shape), tab, tab],
        out_specs=[pl.BlockSpec((1, tm, POOL_WIDTH), lambda bi, i: (bi, i, 0)),
                   pl.BlockSpec((1, tm, KV_BLOCK), lambda bi, i: (bi, i, 0)),
                   pl.BlockSpec((1, N_HEADS, tm, QK_HEAD), lambda bi, i: (bi, 0, i, 0))],
        out_shape=[jax.ShapeDtypeStruct((b, n, POOL_WIDTH), F32),
                   jax.ShapeDtypeStruct((b, n, KV_BLOCK), F32),
                   jax.ShapeDtypeStruct((b, N_HEADS, n, QK_HEAD), BF16)],
        scratch_shapes=[pltpu.VMEM((tm, d), BF16)],
        compiler_params=_params("parallel", "parallel"),
        name="evenproj",
    )(x, g.reshape(1, d), sh, sc, w_in2, q_norm.reshape(1, Q_RANK), wq, cos2, sin2)


def _rowmm_kernel(*refs, n_in, has_res):
    a_refs, w_refs = refs[:n_in], refs[n_in:2 * n_in]
    o_ref = refs[-1]
    acc = None
    for a, w in zip(a_refs, w_refs):
        d = jnp.dot(a[0], w[...], preferred_element_type=F32)
        acc = d if acc is None else acc + d
    if has_res:
        res_ref, gate_ref = refs[2 * n_in], refs[2 * n_in + 1]
        acc = res_ref[0] + gate_ref[0] * acc
    o_ref[0] = acc.astype(o_ref.dtype)


def _rowmm(a_list, w_list, out_dtype, res=None, gate=None, tm=512):
    b, n, _ = a_list[0].shape
    nout = w_list[0].shape[1]
    tm = min(tm, n)
    in_specs = [pl.BlockSpec((1, tm, a.shape[2]), lambda bi, i: (bi, i, 0)) for a in a_list]
    in_specs += [_resident(w.shape) for w in w_list]
    args = list(a_list) + list(w_list)
    if res is not None:
        in_specs += [pl.BlockSpec((1, tm, nout), lambda bi, i: (bi, i, 0)),
                     pl.BlockSpec((1, 1, nout), lambda bi, i: (bi, 0, 0))]
        args += [res, gate]
    return pl.pallas_call(
        functools.partial(_rowmm_kernel, n_in=len(a_list), has_res=res is not None),
        grid=(b, n // tm),
        in_specs=in_specs,
        out_specs=pl.BlockSpec((1, tm, nout), lambda bi, i: (bi, i, 0)),
        out_shape=jax.ShapeDtypeStruct((b, n, nout), out_dtype),
        compiler_params=_params("parallel", "parallel"),
        name="rowmm",
    )(*args)


def _rowmm_strided_kernel(f_ref, w_ref, res_ref, gate_ref, o_ref):
    gate = gate_ref[0]
    d = w_ref.shape[0]
    n2 = f_ref.shape[1]
    steps = res_ref.shape[2]
    rows = jnp.concatenate([f_ref[0, :, s * d:(s + 1) * d] for s in range(steps)], axis=0)
    for c in range(0, d, DFT_COLS):
        cols = slice(c, c + DFT_COLS)
        y = jnp.dot(rows, w_ref[:, cols], preferred_element_type=F32)
        for s in range(steps):
            o_ref[0, :, s, cols] = res_ref[0, :, s, cols] + gate[:, cols] * y[s * n2:(s + 1) * n2, :]


def _rowmm_strided(f, w, res, gate):
    b, n, d = res.shape
    res4 = res.reshape(b, DFT_N2, DFT_N1, d)
    blk = pl.BlockSpec((1, DFT_N2, DFT_STRIDE_BLOCK, d), lambda bi, i: (bi, 0, i, 0))
    out = pl.pallas_call(
        _rowmm_strided_kernel,
        grid=(b, DFT_N1 // DFT_STRIDE_BLOCK),
        in_specs=[pl.BlockSpec((1, DFT_N2, DFT_STRIDE_BLOCK * d), lambda bi, i: (bi, 0, i)),
                  _resident(w.shape), blk,
                  pl.BlockSpec((1, 1, d), lambda bi, i: (bi, 0, 0))],
        out_specs=blk,
        out_shape=jax.ShapeDtypeStruct(res4.shape, F32),
        compiler_params=_params("parallel", "parallel"),
        name="rowmm_strided",
    )(f, w, res4, gate)
    return out.reshape(b, n, d)


def _kvproj_kernel(px_ref, pc_ref, g_ref, wk_ref, wvt_ref, c_ref, s_ref, k_ref, vt_ref, kmax_ref):
    @pl.when(pl.program_id(1) == 0)
    def _():
        kmax_ref[...] = jnp.zeros(kmax_ref.shape, F32)

    def emit(blk, rope):
        z = _rms(blk[:, :KV_RANK], g_ref[...]).astype(BF16)
        kr = _rope_cols(blk[:, KV_RANK:], c_ref[...], s_ref[...], rope)[:, :QK_ROPE].astype(k_ref.dtype)
        kn = jnp.dot(z, wk_ref[...], preferred_element_type=F32).astype(k_ref.dtype)
        vt = lax.dot_general(wvt_ref[...], z, (((1,), (1,)), ((), ())),
                             preferred_element_type=F32).astype(vt_ref.dtype)
        kn2 = kn.astype(F32) * kn.astype(F32)
        kr2 = jnp.sum(kr.astype(F32) * kr.astype(F32), axis=1, keepdims=True)
        for h in range(N_HEADS):
            k_ref[0, h, :, :QK_NOPE] = kn[:, h * QK_NOPE:(h + 1) * QK_NOPE]
            k_ref[0, h, :, QK_NOPE:] = kr
            vt_ref[0, h] = vt[h * V_HEAD:(h + 1) * V_HEAD, :]
            norm2 = jnp.sum(kn2[:, h * QK_NOPE:(h + 1) * QK_NOPE], axis=1, keepdims=True) + kr2
            kmax_ref[0, h] = jnp.maximum(kmax_ref[0, h], jnp.max(norm2, axis=0, keepdims=True))

    is_ctx = pl.program_id(1) == pl.num_programs(1) - 1

    @pl.when(jnp.logical_not(is_ctx))
    def _():
        emit(px_ref[0], True)

    @pl.when(is_ctx)
    def _():
        emit(pc_ref[0], False)


def _kvproj(px, pc, kv_norm, wk, wvt, cos2, sin2):
    b, n, _ = px.shape
    tm = pc.shape[1]
    assert n % tm == 0
    nt = n // tm
    tab = pl.BlockSpec((tm, 2 * QK_ROPE), lambda bi, i: (jnp.minimum(i, nt - 1), 0))
    return pl.pallas_call(
        _kvproj_kernel,
        grid=(b, nt + 1),
        in_specs=[pl.BlockSpec((1, tm, KV_BLOCK), lambda bi, i: (bi, jnp.minimum(i, nt - 1), 0)),
                  pl.BlockSpec((1, tm, KV_BLOCK), lambda bi, i: (bi, 0, 0)),
                  pl.BlockSpec((1, KV_RANK), lambda bi, i: (0, 0)),
                  _resident(wk.shape), _resident(wvt.shape), tab, tab],
        out_specs=[pl.BlockSpec((1, N_HEADS, tm, QK_HEAD), lambda bi, i: (bi, 0, i, 0)),
                   pl.BlockSpec((1, N_HEADS, V_HEAD, tm), lambda bi, i: (bi, 0, 0, i)),
                   pl.BlockSpec((1, N_HEADS, 8, 128), lambda bi, i: (bi, 0, 0, 0))],
        out_shape=[jax.ShapeDtypeStruct((b, N_HEADS, n + tm, QK_HEAD), BF16),
                   jax.ShapeDtypeStruct((b, N_HEADS, V_HEAD, n + tm), BF16),
                   jax.ShapeDtypeStruct((b, N_HEADS, 8, 128), F32)],
        compiler_params=_params("parallel", "arbitrary"),
        name="kvproj",
    )(px, pc, kv_norm.reshape(1, KV_RANK), wk, wvt, cos2, sin2)


def _attn_kernel(q_ref, k_ref, vt_ref, kmax_ref, o_ref, *, tk_fixed, tk_online):
    q = q_ref[0, 0]
    tq = q.shape[0]
    kv_len = k_ref.shape[2]
    nt = (((1,), (1,)), ((), ()))

    def scores(j, tk):
        return lax.dot_general(k_ref[0, 0, j * tk:(j + 1) * tk, :], q, nt, preferred_element_type=F32)

    def weighted_values(j, tk, p):
        return jnp.dot(vt_ref[0, 0, :, j * tk:(j + 1) * tk], p.astype(BF16), preferred_element_type=F32)

    def store(acc, l):
        o_ref[0] = jnp.transpose(acc / l).astype(o_ref.dtype)

    qf = q.astype(F32)
    qn2 = lax.dot_general(jnp.ones((8, q.shape[1]), BF16), (qf * qf).astype(BF16), nt,
                          preferred_element_type=F32)[:1]
    ref = jnp.sqrt(qn2 * kmax_ref[0, 0, :1, :1]) * ATTN_BOUND_SLACK
    l = jnp.zeros((1, tq), F32)
    acc = jnp.zeros((V_HEAD, tq), F32)
    for j in range(kv_len // tk_fixed):
        p = jnp.exp2(scores(j, tk_fixed) - ref)
        l = l + jnp.sum(p, axis=0, keepdims=True)
        acc = acc + weighted_values(j, tk_fixed, p)
    store(acc, l)

    @pl.when(jnp.logical_not(jnp.min(l) >= ATTN_MIN_DENOM))
    def _():
        n_blocks = kv_len // tk_online
        m = jnp.full((1, tq), -jnp.inf, F32)
        l = jnp.zeros((1, tq), F32)
        acc = jnp.zeros((V_HEAD, tq), F32)
        st_next = scores(0, tk_online)
        for j in range(n_blocks):
            st = st_next
            if j + 1 < n_blocks:
                st_next = scores(j + 1, tk_online)
            m_new = jnp.maximum(m, jnp.max(st, axis=0, keepdims=True))
            alpha = jnp.exp2(m - m_new)
            p = jnp.exp2(st - m_new)
            l = alpha * l + jnp.sum(p, axis=0, keepdims=True)
            acc = alpha * acc + weighted_values(j, tk_online, p)
            m = m_new
        store(acc, l)


def _attention(q, k_all, vt_all, kmax, kv_len, kv_block0):
    b, h, n, _ = q.shape
    tq = min(ATTN_TQ, n)
    tk_fixed, tk_online = min(ATTN_TK_FIXED, kv_len), min(ATTN_TK_ONLINE, kv_len)
    assert kv_len % tk_fixed == 0 and kv_len % tk_online == 0
    return pl.pallas_call(
        functools.partial(_attn_kernel, tk_fixed=tk_fixed, tk_online=tk_online),
        grid=(b, h, n // tq),
        in_specs=[pl.BlockSpec((1, 1, tq, QK_HEAD), lambda bi, hi, i: (bi, hi, i, 0)),
                  pl.BlockSpec((1, 1, kv_len, QK_HEAD), lambda bi, hi, i: (bi, hi, kv_block0, 0)),
                  pl.BlockSpec((1, 1, V_HEAD, kv_len), lambda bi, hi, i: (bi, hi, 0, kv_block0)),
                  pl.BlockSpec((1, 1, 8, 128), lambda bi, hi, i: (bi, hi, 0, 0))],
        out_specs=pl.BlockSpec((1, tq, V_HEAD), lambda bi, hi, i: (bi, i, hi)),
        out_shape=jax.ShapeDtypeStruct((b, n, h * V_HEAD), BF16),
        compiler_params=_params("parallel", "parallel", "parallel"),
        name="attention",
    )(q, k_all, vt_all, kmax)


def _pool_kernel(u_ref, prev_ref, next_ref, w_ref, s_ref, o_ref, ext_sc, *, n, tm):
    i = pl.program_id(1)
    keep_prev = jnp.where(i > 0, 1.0, 0.0)
    keep_next = jnp.where(i < pl.num_programs(1) - 1, 1.0, 0.0)
    ext_sc[0:POOL_HALO] = prev_ref[0] * keep_prev
    ext_sc[POOL_HALO:POOL_HALO + tm] = u_ref[0]
    ext_sc[POOL_HALO + tm:] = next_ref[0] * keep_next
    t = i * tm + lax.broadcasted_iota(jnp.int32, (tm, 1), 0)
    for g, win in enumerate(POOL_WINDOWS):
        half = win // 2
        cols = slice(g * POOL_GROUP, (g + 1) * POOL_GROUP)
        tot = ext_sc[POOL_HALO - half:POOL_HALO - half + tm, cols]
        for d in range(1 - half, half):
            tot = tot + ext_sc[POOL_HALO + d:POOL_HALO + d + tm, cols]
        cnt = (jnp.minimum(t + half, n) - jnp.maximum(t - half, 0)).astype(F32)
        pooled = (tot / cnt - ext_sc[POOL_HALO:POOL_HALO + tm, cols]).astype(BF16)
        y = jnp.dot(pooled, w_ref[g], preferred_element_type=F32) * s_ref[:, cols]
        o_ref[0, :, cols] = y.astype(o_ref.dtype)


def _pool(p, w_pool, pool_scale):
    b, n, _ = p.shape
    tm = min(512, n)
    hb = tm // POOL_HALO
    last_hb = n // POOL_HALO - 1
    return pl.pallas_call(
        functools.partial(_pool_kernel, n=n, tm=tm),
        grid=(b, n // tm),
        in_specs=[pl.BlockSpec((1, tm, POOL_WIDTH), lambda bi, i: (bi, i, 0)),
                  pl.BlockSpec((1, POOL_HALO, POOL_WIDTH), lambda bi, i: (bi, jnp.maximum(i * hb - 1, 0), 0)),
                  pl.BlockSpec((1, POOL_HALO, POOL_WIDTH),
                               lambda bi, i: (bi, jnp.minimum((i + 1) * hb, last_hb), 0)),
                  _resident(w_pool.shape),
                  pl.BlockSpec((1, POOL_WIDTH), lambda bi, i: (0, 0))],
        out_specs=pl.BlockSpec((1, tm, POOL_WIDTH), lambda bi, i: (bi, i, 0)),
        out_shape=jax.ShapeDtypeStruct((b, n, POOL_WIDTH), BF16),
        scratch_shapes=[pltpu.VMEM((tm + 2 * POOL_HALO, POOL_WIDTH), F32)],
        compiler_params=_params("parallel", "parallel"),
        name="pool",
    )(p, p, p, w_pool, pool_scale.reshape(1, POOL_WIDTH))


def _dft1_kernel(x_ref, g_ref, sh_ref, sc_ref, a_ref, o_ref, h_sc):
    gs = g_ref[...] * (1.0 + sc_ref[0])
    sh = sh_ref[0]
    for t1 in range(DFT_N1):
        x = x_ref[0, t1]
        inv = lax.rsqrt(jnp.mean(x * x, axis=-1, keepdims=True) + EPS)
        h_sc[t1 * DFT_ROWS:(t1 + 1) * DFT_ROWS, :] = (x * inv * gs + sh).astype(h_sc.dtype)
    for c in range(0, D_MODEL, DFT_COLS):
        y = jnp.dot(a_ref[...], h_sc[:, c:c + DFT_COLS], preferred_element_type=F32)
        o_ref[0, :, :, :, c:c + DFT_COLS] = y.astype(o_ref.dtype).reshape(2, DFT_N1, DFT_ROWS, DFT_COLS)


def _dft_stage1(x, g, sh, sc, a_kron):
    b, n, d = x.shape
    vec = pl.BlockSpec((1, 1, d), lambda bi, r: (bi, 0, 0))
    return pl.pallas_call(
        _dft1_kernel,
        grid=(b, DFT_N2 // DFT_ROWS),
        in_specs=[pl.BlockSpec((1, DFT_N1, DFT_ROWS, d), lambda bi, r: (bi, 0, r, 0)),
                  pl.BlockSpec((1, d), lambda bi, r: (0, 0)), vec, vec,
                  _resident(a_kron.shape)],
        out_specs=pl.BlockSpec((1, 2, DFT_N1, DFT_ROWS, d), lambda bi, r: (bi, 0, 0, r, 0)),
        out_shape=jax.ShapeDtypeStruct((b, 2, DFT_N1, DFT_N2, d), BF16),
        scratch_shapes=[pltpu.VMEM((DFT_N1 * DFT_ROWS, d), BF16)],
        compiler_params=_params("parallel", "parallel"),
        name="dft_stage1",
    )(x.reshape(b, DFT_N1, DFT_N2, d), g.reshape(1, d), sh, sc, a_kron)


def _dft2_rows(y, m, cc_ref, sc_ref, rows_out, seq_len, store):
    pq = jnp.dot(m, y, preferred_element_type=F32).astype(BF16)
    ortho = 1.0 / math.sqrt(seq_len * FOURIER_GROUP)
    for g in range(FOURIER_GROUPS):
        cols = slice(g * FOURIER_GROUP, (g + 1) * FOURIER_GROUP)
        f = (jnp.dot(pq[:rows_out, cols], cc_ref[...], preferred_element_type=F32)
             - jnp.dot(pq[rows_out:, cols], sc_ref[...], preferred_element_type=F32))
        store(cols, f * ortho)


def _dft2_kernel(y_ref, m_ref, cc_ref, sc_ref, o_ref):
    for s in range(y_ref.shape[2]):
        y = jnp.concatenate([y_ref[0, 0, s], y_ref[0, 1, s]], axis=0)

        def store(cols, f, s=s):
            o_ref[0, :, s * D_MODEL + cols.start:s * D_MODEL + cols.stop] = f.astype(o_ref.dtype)

        _dft2_rows(y, m_ref[s], cc_ref, sc_ref, DFT_N2, DFT_N1 * DFT_N2, store)


def _dft_small_kernel(h_ref, m_ref, cc_ref, sc_ref, o_ref):
    def store(cols, f):
        o_ref[0, :, cols] = f.astype(o_ref.dtype)

    n = h_ref.shape[1]
    _dft2_rows(h_ref[0], m_ref[0], cc_ref, sc_ref, n, n, store)


def _dft_stage2(y, m, cc, sc):
    b = y.shape[0]
    return pl.pallas_call(
        _dft2_kernel,
        grid=(b, DFT_N1 // DFT2_K1),
        in_specs=[pl.BlockSpec((1, 2, DFT2_K1, DFT_N2, D_MODEL), lambda bi, k: (bi, 0, k, 0, 0)),
                  pl.BlockSpec((DFT2_K1, 2 * DFT_N2, 2 * DFT_N2), lambda bi, k: (k, 0, 0)),
                  _resident(cc.shape), _resident(sc.shape)],
        out_specs=pl.BlockSpec((1, DFT_N2, DFT2_K1 * D_MODEL), lambda bi, k: (bi, 0, k)),
        out_shape=jax.ShapeDtypeStruct((b, DFT_N2, DFT_N1 * D_MODEL), BF16),
        compiler_params=_params("parallel", "parallel"),
        name="dft_stage2",
    )(y, m, cc, sc)


def _dft_small(h, m, cc, sc):
    b, n, d = h.shape
    return pl.pallas_call(
        _dft_small_kernel,
        grid=(b,),
        in_specs=[pl.BlockSpec((1, n, d), lambda bi: (bi, 0, 0)),
                  _resident(m.shape), _resident(cc.shape), _resident(sc.shape)],
        out_specs=pl.BlockSpec((1, n, d), lambda bi: (bi, 0, 0)),
        out_shape=jax.ShapeDtypeStruct((b, n, d), BF16),
        compiler_params=_params("parallel"),
        name="dft_small",
    )(h, m, cc, sc)


def _cos_sin(num, den):
    ang = (num % den).astype(F32) * (2.0 * math.pi / den)
    return jnp.cos(ang), jnp.sin(ang)


def _dft_tables(n_ctx):
    i1 = jnp.arange(DFT_N1, dtype=jnp.int32)
    c1, s1 = _cos_sin(i1[:, None] * i1[None, :], DFT_N1)
    a = jnp.concatenate([c1, -s1], axis=0)
    eye = jnp.eye(DFT_ROWS, dtype=F32)
    a_kron = jnp.einsum('pt,jk->pjtk', a, eye).reshape(2 * DFT_N1 * DFT_ROWS, DFT_N1 * DFT_ROWS)
    n = DFT_N1 * DFT_N2
    t2 = jnp.arange(DFT_N2, dtype=jnp.int32)
    k = i1[:, None, None] + DFT_N1 * t2[None, :, None]
    ck, sk = _cos_sin(k * t2[None, None, :], n)
    m = jnp.concatenate([jnp.concatenate([ck, sk], axis=2),
                         jnp.concatenate([sk, -ck], axis=2)], axis=1)
    ic = jnp.arange(FOURIER_GROUP, dtype=jnp.int32)
    cc, sc = _cos_sin(ic[:, None] * ic[None, :], FOURIER_GROUP)
    il = jnp.arange(n_ctx, dtype=jnp.int32)
    cl, sl = _cos_sin(il[:, None] * il[None, :], n_ctx)
    m_ctx = jnp.concatenate([cl, sl], axis=0)[None]

    return dict(a_kron=a_kron.astype(BF16), m=m.astype(BF16), m_ctx=m_ctx.astype(BF16),
                cc=cc.astype(BF16), sc=sc.astype(BF16))


def _mlp_kernel(x_ref, g_ref, sh_ref, sc_ref, gate_ref, w1_ref, w2_ref, *rest, final):
    o_ref, h_sc, acc_sc = rest[-3:]
    j = pl.program_id(2)
    last = pl.num_programs(2) - 1
    tm = x_ref.shape[1]
    chunks = [slice(r, r + MLP_ROWS) for r in range(0, tm, MLP_ROWS)]

    def hidden(rows):
        a = jnp.maximum(jnp.dot(h_sc[rows, :], w1_ref[...], preferred_element_type=F32), 0.0)
        return (a * a).astype(BF16)

    @pl.when(j == 0)
    def _():
        parts = []
        for rows in chunks:
            _normmod_rows(x_ref, g_ref, sh_ref, sc_ref, h_sc, rows.start, rows.stop)
            parts.append(hidden(rows))
        acc_sc[...] = jnp.dot(jnp.concatenate(parts, axis=0), w2_ref[...], preferred_element_type=F32)

    @pl.when(jnp.logical_and(j > 0, j < last))
    def _():
        acc_sc[...] += jnp.dot(hidden(slice(None)), w2_ref[...], preferred_element_type=F32)

    @pl.when(j == last)
    def _():
        gate = gate_ref[0]
        a = hidden(slice(None))
        for rows in chunks:
            upd = acc_sc[rows, :] + jnp.dot(a[rows, :], w2_ref[...], preferred_element_type=F32)
            for r in range(0, MLP_ROWS, NORM_ROWS):
                y = x_ref[0, rows.start + r:rows.start + r + NORM_ROWS, :] + gate * upd[r:r + NORM_ROWS, :]
                if final:
                    y = _rms(y, rest[0][...])
                o_ref[0, rows.start + r:rows.start + r + NORM_ROWS, :] = y


def _mlp(x, g, sh, sc, gate, w1, w2, layer, final_g=None):
    b, n, d = x.shape
    tm = min(MLP_TM, n)
    tf = MLP_TF
    vec = pl.BlockSpec((1, 1, d), lambda bi, i, j: (bi, 0, 0))
    row = pl.BlockSpec((1, d), lambda bi, i, j: (0, 0))
    in_specs = [pl.BlockSpec((1, tm, d), lambda bi, i, j: (bi, i, 0)), row, vec, vec, vec,
                pl.BlockSpec((None, d, tf), lambda bi, i, j: (layer, 0, j)),
                pl.BlockSpec((None, tf, d), lambda bi, i, j: (layer, j, 0))]
    args = [x, g.reshape(1, d), sh, sc, gate, w1, w2]
    if final_g is not None:
        in_specs.append(row)
        args.append(final_g.reshape(1, d))
    return pl.pallas_call(
        functools.partial(_mlp_kernel, final=final_g is not None),
        grid=(b, n // tm, D_FF // tf),
        in_specs=in_specs,
        out_specs=pl.BlockSpec((1, tm, d), lambda bi, i, j: (bi, i, 0)),
        out_shape=jax.ShapeDtypeStruct((b, n, d), F32),
        scratch_shapes=[pltpu.VMEM((tm, d), BF16), pltpu.VMEM((tm, d), F32)],
        compiler_params=_params("parallel", "parallel", "arbitrary"),
        name="mlp",
    )(*args)


def _rotate_half_axial(x):
    xr = x.reshape(x.shape[:-1] + (2, 2, ROPE_FREQS))
    return jnp.concatenate([-xr[..., 1:, :], xr[..., :1, :]], axis=-2).reshape(x.shape)


def _rope_tables(n):
    rows = n // GRID_W
    r = jnp.broadcast_to(jnp.arange(rows, dtype=F32)[:, None], (rows, GRID_W)).reshape(n)
    col = jnp.broadcast_to(jnp.arange(GRID_W, dtype=F32)[None, :], (rows, GRID_W)).reshape(n)
    inv = ROPE_THETA ** (-2.0 * jnp.arange(ROPE_FREQS, dtype=F32) / ROPE_AXIS)
    ang = jnp.stack([r[:, None] * inv, col[:, None] * inv], axis=1)
    ang = jnp.broadcast_to(ang[:, :, None, :], (n, 2, 2, ROPE_FREQS)).reshape(n, QK_ROPE)
    zeros = jnp.zeros((n, QK_ROPE), F32)
    return (jnp.concatenate([jnp.cos(ang), zeros], axis=1), jnp.concatenate([jnp.sin(ang), zeros], axis=1))


def _even_weights(w_in, w_uq, w_ukv):
    w_kr = w_in[:, Q_RANK + KV_RANK:Q_RANK + KV_RANK + QK_ROPE]
    w_in2 = jnp.concatenate([w_in[:, Q_RANK + KV_RANK + QK_ROPE:], w_in[:, :Q_RANK],
                             w_in[:, Q_RANK:Q_RANK + KV_RANK], w_kr, _rotate_half_axial(w_kr)], axis=1)
    wq = w_uq.reshape(Q_RANK, N_HEADS, QK_HEAD)
    wq = jnp.concatenate([wq, _rotate_half_axial(wq[..., QK_NOPE:])], axis=-1)
    wq = wq.reshape(Q_RANK, N_HEADS * (QK_HEAD + QK_ROPE))
    wkv = w_ukv.reshape(KV_RANK, N_HEADS, QK_NOPE + V_HEAD)
    wk = wkv[..., :QK_NOPE].reshape(KV_RANK, N_HEADS * QK_NOPE)
    wvt = jnp.transpose(wkv[..., QK_NOPE:], (1, 2, 0)).reshape(N_HEADS * V_HEAD, KV_RANK)
    return w_in2.astype(BF16), wq.astype(BF16), wk.astype(BF16), wvt.astype(BF16)


def kernel(x, c, ctx, c_ctx, w_mod, b_mod, norm1, norm2, w_in, q_norm, w_uq, kv_norm, w_ukv, w_pool,
           pool_scale, w_out_even, w_out_odd, w_mlp1, w_mlp2, final_norm):
    b, n, d = x.shape
    n_ctx = ctx.shape[1]
    assert n == DFT_N1 * DFT_N2 and d == D_MODEL and n % GRID_W == 0
    n_keys = n + n_ctx
    assert n % 512 == 0 and n % n_ctx == 0

    cos2, sin2 = _rope_tables(n)
    dft = _dft_tables(n_ctx)
    w1 = w_mlp1.astype(BF16)
    w2 = w_mlp2.astype(BF16)

    cc = jnp.concatenate([c, c_ctx[None], jnp.zeros((8 - b - 1, d), F32)], axis=0)
    mods = _modulation(cc, w_mod, b_mod)

    updates_ctx = [False] * DEPTH
    for l in reversed(range(DEPTH - 1)):
        updates_ctx[l] = (l + 1) % 2 == 0 or updates_ctx[l + 1]

    for l in range(DEPTH):
        even = l % 2 == 0
        i = l // 2
        mx = [mods[l, :b, k * d:(k + 1) * d].reshape(b, 1, d) for k in range(6)]
        mc = [jnp.broadcast_to(mods[l, b, k * d:(k + 1) * d].reshape(1, 1, d), (b, 1, d)) for k in range(6)]
        ctx_update = updates_ctx[l]

        if even:
            w_in2, wq, wk, wvt = _even_weights(w_in[i], w_uq[i], w_ukv[i])
            w_out = w_out_even[i].astype(BF16)
            wp = w_pool[i].astype(BF16)
            ux, kvx, q = _evenproj(x, norm1[l], mx[0], mx[1], w_in2, q_norm[i], wq, cos2, sin2, rope=True)
            uc, kvc, cq = _evenproj(ctx, norm1[l], mc[0], mc[1], w_in2, q_norm[i], wq, cos2, sin2, rope=False)
            k_all, vt_all, kmax = _kvproj(kvx, kvc, kv_norm[i], wk, wvt, cos2, sin2)
            attn_x = _attention(q, k_all, vt_all, kmax, n_keys, 0)
            pool_x = _pool(ux, wp, pool_scale[i])
            x = _rowmm([attn_x, pool_x], [w_out[:MLA_OUT], w_out[MLA_OUT:]], F32, res=x, gate=mx[2])
            if ctx_update:
                attn_c = _attention(cq, k_all, vt_all, kmax, n_ctx, n // n_ctx)
                pool_c = _pool(uc, wp, pool_scale[i])
                ctx = _rowmm([attn_c, pool_c], [w_out[:MLA_OUT], w_out[MLA_OUT:]], F32, res=ctx, gate=mc[2])
        else:
            w_out = w_out_odd[i].astype(BF16)
            if ctx_update:
                hc = _normmod(ctx, norm1[l], mc[0], mc[1])
            y1 = _dft_stage1(x, norm1[l], mx[0], mx[1], dft['a_kron'])
            fx = _dft_stage2(y1, dft['m'], dft['cc'], dft['sc'])
            x = _rowmm_strided(fx, w_out, x, mx[2])
            if ctx_update:
                fc = _dft_small(hc, dft['m_ctx'], dft['cc'], dft['sc'])
                ctx = _rowmm([fc], [w_out], F32, res=ctx, gate=mc[2])
        x = _mlp(x, norm2[l], mx[3], mx[4], mx[5], w1, w2, l, final_norm if l == DEPTH - 1 else None)
        if ctx_update:
            ctx = _mlp(ctx.reshape(1, b * n_ctx, d), norm2[l], mc[3][:1], mc[4][:1], mc[5][:1],
                       w1, w2, l).reshape(b, n_ctx, d)
    return x
```

```python
import functools
import math

import jax
import jax.numpy as jnp
import numpy as np
from jax import lax
from jax.experimental import pallas as pl
from jax.experimental.pallas import tpu as pltpu

D_MODEL = 2048
DEPTH = 4
GRID_W = 64
N_HEADS = 8
Q_RANK = 512
KV_RANK = 256
QK_NOPE = 128
QK_ROPE = 64
V_HEAD = 128
QK_HEAD = QK_NOPE + QK_ROPE
ATTN_SCALE = QK_HEAD ** -0.5
Q_SCALE = ATTN_SCALE * math.log2(math.e)
MLA_OUT = N_HEADS * V_HEAD
ROPE_AXIS = QK_ROPE // 2
ROPE_FREQS = ROPE_AXIS // 2
ROPE_THETA = 10000.0
POOL_WIDTH = D_MODEL // 2
POOL_WINDOWS = (2, 4, 8, 16)
POOL_GROUP = POOL_WIDTH // len(POOL_WINDOWS)
POOL_HALO = 8
FOURIER_GROUPS = 4
FOURIER_GROUP = D_MODEL // FOURIER_GROUPS
D_FF = 4 * D_MODEL
EPS = 1e-6

P_POOL = 0
P_QLAT = POOL_WIDTH
P_KV = POOL_WIDTH + Q_RANK
P_WIDTH = POOL_WIDTH + Q_RANK + KV_RANK + 2 * QK_ROPE
KV_BLOCK = KV_RANK + 2 * QK_ROPE

DFT_N1 = 64
DFT_N2 = 128
DFT_ROWS = 16
DFT2_K1 = 4
DFT_STRIDE_BLOCK = 8
DFT_COLS = 512

NORM_ROWS = 16
MLP_TM = 512
MLP_TF = 1024
MLP_ROWS = 128
ATTN_TQ = 1024
ATTN_TK_FIXED = 2816
ATTN_TK_ONLINE = 384
ATTN_BOUND_SLACK = 1.02
ATTN_MIN_DENOM = 2.0 ** -60

V7X_VMEM_LIMIT =56 * 1024 * 1024

F32 = jnp.float32
BF16 = jnp.bfloat16


def _params(*sem):
    return pltpu.CompilerParams(dimension_semantics=sem, vmem_limit_bytes=V7X_VMEM_LIMIT)


def _resident(shape):
    nd = len(shape)
    return pl.BlockSpec(shape, lambda *_: (0,) * nd, pipeline_mode=pl.Buffered(1))


def _rms(x, g):
    return x * lax.rsqrt(jnp.mean(x * x, axis=-1, keepdims=True) + EPS) * g


def _mod_kernel(c_ref, w_ref, b_ref, o_ref):
    c = c_ref[...]
    s = c * (1.0 / (1.0 + jnp.exp(-c)))
    o_ref[0] = jnp.dot(s, w_ref[0], preferred_element_type=F32, precision=lax.Precision.HIGHEST) + b_ref[0]


def _modulation(cc, w_mod, b_mod):
    rows = cc.shape[0]
    tn = 1536
    n6 = 6 * D_MODEL
    return pl.pallas_call(
        _mod_kernel,
        grid=(DEPTH, n6 // tn),
        in_specs=[
            pl.BlockSpec((rows, D_MODEL), lambda l, j: (0, 0)),
            pl.BlockSpec((1, D_MODEL, tn), lambda l, j: (l, 0, j)),
            pl.BlockSpec((1, 1, tn), lambda l, j: (l, 0, j)),
        ],
        out_specs=pl.BlockSpec((1, rows, tn), lambda l, j: (l, 0, j)),
        out_shape=jax.ShapeDtypeStruct((DEPTH, rows, n6), F32),
        compiler_params=_params("parallel", "parallel"),
        name="modulation",
    )(cc, w_mod, b_mod.reshape(DEPTH, 1, n6))


def _normmod_rows(x_ref, g_ref, sh_ref, sc_ref, dst_ref, start=0, stop=None):
    gs = g_ref[...] * (1.0 + sc_ref[0])
    sh = sh_ref[0]
    for r in range(start, x_ref.shape[1] if stop is None else stop, NORM_ROWS):
        x = x_ref[0, r:r + NORM_ROWS, :]
        inv = lax.rsqrt(jnp.mean(x * x, axis=-1, keepdims=True) + EPS)
        dst_ref[r:r + NORM_ROWS, :] = (x * inv * gs + sh).astype(dst_ref.dtype)


def _normmod_kernel(x_ref, g_ref, sh_ref, sc_ref, o_ref):
    _normmod_rows(x_ref, g_ref, sh_ref, sc_ref, o_ref.at[0])


def _normmod(x, g, sh, sc, out_dtype=BF16):
    b, n, d = x.shape
    tm = min(512, n)
    vec = pl.BlockSpec((1, 1, d), lambda bi, i: (bi, 0, 0))
    return pl.pallas_call(
        _normmod_kernel,
        grid=(b, n // tm),
        in_specs=[pl.BlockSpec((1, tm, d), lambda bi, i: (bi, i, 0)),
                  pl.BlockSpec((1, d), lambda bi, i: (0, 0)), vec, vec],
        out_specs=pl.BlockSpec((1, tm, d), lambda bi, i: (bi, i, 0)),
        out_shape=jax.ShapeDtypeStruct((b, n, d), out_dtype),
        compiler_params=_params("parallel", "parallel"),
        name="normmod",
    )(x, g.reshape(1, d), sh, sc)


def _rope_cols(t, cos2, sin2, rope):
    if not rope:
        return t
    return t * cos2 + pltpu.roll(t, QK_ROPE, 1) * sin2


def _evenproj_kernel(x_ref, g_ref, sh_ref, sc_ref, w_ref, qg_ref, wq_ref, c_ref, s_ref,
                     u_ref, kv_ref, q_ref, h_sc, *, rope):
    width = QK_NOPE + 2 * QK_ROPE
    for r in range(0, x_ref.shape[1], MLP_ROWS):
        rows = slice(r, r + MLP_ROWS)
        _normmod_rows(x_ref, g_ref, sh_ref, sc_ref, h_sc, r, r + MLP_ROWS)
        p = jnp.dot(h_sc[rows, :], w_ref[...], preferred_element_type=F32)
        u_ref[0, rows, :] = p[:, P_POOL:P_POOL + POOL_WIDTH]
        kv_ref[0, rows, :] = p[:, P_KV:]
        z = _rms(p[:, P_QLAT:P_QLAT + Q_RANK], qg_ref[...]).astype(BF16)
        acc = jnp.dot(z, wq_ref[...], preferred_element_type=F32)
        for h in range(N_HEADS):
            q_ref[0, h, rows, :QK_NOPE] = (acc[:, h * width:h * width + QK_NOPE] * Q_SCALE).astype(q_ref.dtype)
            t = _rope_cols(acc[:, h * width + QK_NOPE:(h + 1) * width], c_ref[rows, :], s_ref[rows, :], rope)
            q_ref[0, h, rows, QK_NOPE:] = (t[:, :QK_ROPE] * Q_SCALE).astype(q_ref.dtype)


def _evenproj(x, g, sh, sc, w_in2, q_norm, wq, cos2, sin2, rope):
    b, n, d = x.shape
    tm = min(512, n)
    vec = pl.BlockSpec((1, 1, d), lambda bi, i: (bi, 0, 0))
    tab = pl.BlockSpec((tm, 2 * QK_ROPE), lambda bi, i: (i, 0))
    return pl.pallas_call(
        functools.partial(_evenproj_kernel, rope=rope),
        grid=(b, n // tm),
        in_specs=[pl.BlockSpec((1, tm, d), lambda bi, i: (bi, i, 0)),
                  pl.BlockSpec((1, d), lambda bi, i: (0, 0)), vec, vec, _resident(w_in2.shape),
                  pl.BlockSpec((1, Q_RANK), lambda bi, i: (0, 0)), _resident(wq.shape), tab, tab],
        out_specs=[pl.BlockSpec((1, tm, POOL_WIDTH), lambda bi, i: (bi, i, 0)),
                   pl.BlockSpec((1, tm, KV_BLOCK), lambda bi, i: (bi, i, 0)),
                   pl.BlockSpec((1, N_HEADS, tm, QK_HEAD), lambda bi, i: (bi, 0, i, 0))],
        out_shape=[jax.ShapeDtypeStruct((b, n, POOL_WIDTH), F32),
                   jax.ShapeDtypeStruct((b, n, KV_BLOCK), F32),
                   jax.ShapeDtypeStruct((b, N_HEADS, n, QK_HEAD), BF16)],
        scratch_shapes=[pltpu.VMEM((tm, d), BF16)],
        compiler_params=_params("parallel", "parallel"),
        name="evenproj",
    )(x, g.reshape(1, d), sh, sc, w_in2, q_norm.reshape(1, Q_RANK), wq, cos2, sin2)


def _rowmm_kernel(*refs, n_in, has_res):
    a_refs, w_refs = refs[:n_in], refs[n_in:2 * n_in]
    o_ref = refs[-1]
    acc = None
    for a, w in zip(a_refs, w_refs):
        d = jnp.dot(a[0], w[...], preferred_element_type=F32)
        acc = d if acc is None else acc + d
    if has_res:
        res_ref, gate_ref = refs[2 * n_in], refs[2 * n_in + 1]
        acc = res_ref[0] + gate_ref[0] * acc
    o_ref[0] = acc.astype(o_ref.dtype)


def _rowmm(a_list, w_list, out_dtype, res=None, gate=None, tm=512):
    b, n, _ = a_list[0].shape
    nout = w_list[0].shape[1]
    tm = min(tm, n)
    in_specs = [pl.BlockSpec((1, tm, a.shape[2]), lambda bi, i: (bi, i, 0)) for a in a_list]
    in_specs += [_resident(w.shape) for w in w_list]
    args = list(a_list) + list(w_list)
    if res is not None:
        in_specs += [pl.BlockSpec((1, tm, nout), lambda bi, i: (bi, i, 0)),
                     pl.BlockSpec((1, 1, nout), lambda bi, i: (bi, 0, 0))]
        args += [res, gate]
    return pl.pallas_call(
        functools.partial(_rowmm_kernel, n_in=len(a_list), has_res=res is not None),
        grid=(b, n // tm),
        in_specs=in_specs,
        out_specs=pl.BlockSpec((1, tm, nout), lambda bi, i: (bi, i, 0)),
        out_shape=jax.ShapeDtypeStruct((b, n, nout), out_dtype),
        compiler_params=_params("parallel", "parallel"),
        name="rowmm",
    )(*args)


def _rowmm_strided_kernel(f_ref, w_ref, res_ref, gate_ref, o_ref):
    gate = gate_ref[0]
    d = w_ref.shape[0]
    n2 = f_ref.shape[1]
    steps = res_ref.shape[2]
    rows = jnp.concatenate([f_ref[0, :, s * d:(s + 1) * d] for s in range(steps)], axis=0)
    for c in range(0, d, DFT_COLS):
        cols = slice(c, c + DFT_COLS)
        y = jnp.dot(rows, w_ref[:, cols], preferred_element_type=F32)
        for s in range(steps):
            o_ref[0, :, s, cols] = res_ref[0, :, s, cols] + gate[:, cols] * y[s * n2:(s + 1) * n2, :]


def _rowmm_strided(f, w, res, gate):
    b, n, d = res.shape
    res4 = res.reshape(b, DFT_N2, DFT_N1, d)
    blk = pl.BlockSpec((1, DFT_N2, DFT_STRIDE_BLOCK, d), lambda bi, i: (bi, 0, i, 0))
    out = pl.pallas_call(
        _rowmm_strided_kernel,
        grid=(b, DFT_N1 // DFT_STRIDE_BLOCK),
        in_specs=[pl.BlockSpec((1, DFT_N2, DFT_STRIDE_BLOCK * d), lambda bi, i: (bi, 0, i)),
                  _resident(w.shape), blk,
                  pl.BlockSpec((1, 1, d), lambda bi, i: (bi, 0, 0))],
        out_specs=blk,
        out_shape=jax.ShapeDtypeStruct(res4.shape, F32),
        compiler_params=_params("parallel", "parallel"),
        name="rowmm_strided",
    )(f, w, res4, gate)
    return out.reshape(b, n, d)


def _kvproj_kernel(px_ref, pc_ref, g_ref, wk_ref, wvt_ref, c_ref, s_ref, k_ref, vt_ref, kmax_ref):
    @pl.when(pl.program_id(1) == 0)
    def _():
        kmax_ref[...] = jnp.zeros(kmax_ref.shape, F32)

    def emit(blk, rope):
        z = _rms(blk[:, :KV_RANK], g_ref[...]).astype(BF16)
        kr = _rope_cols(blk[:, KV_RANK:], c_ref[...], s_ref[...], rope)[:, :QK_ROPE].astype(k_ref.dtype)
        kn = jnp.dot(z, wk_ref[...], preferred_element_type=F32).astype(k_ref.dtype)
        vt = lax.dot_general(wvt_ref[...], z, (((1,), (1,)), ((), ())),
                             preferred_element_type=F32).astype(vt_ref.dtype)
        kn2 = kn.astype(F32) * kn.astype(F32)
        kr2 = jnp.sum(kr.astype(F32) * kr.astype(F32), axis=1, keepdims=True)
        for h in range(N_HEADS):
            k_ref[0, h, :, :QK_NOPE] = kn[:, h * QK_NOPE:(h + 1) * QK_NOPE]
            k_ref[0, h, :, QK_NOPE:] = kr
            vt_ref[0, h] = vt[h * V_HEAD:(h + 1) * V_HEAD, :]
            norm2 = jnp.sum(kn2[:, h * QK_NOPE:(h + 1) * QK_NOPE], axis=1, keepdims=True) + kr2
            kmax_ref[0, h] = jnp.maximum(kmax_ref[0, h], jnp.max(norm2, axis=0, keepdims=True))

    is_ctx = pl.program_id(1) == pl.num_programs(1) - 1

    @pl.when(jnp.logical_not(is_ctx))
    def _():
        emit(px_ref[0], True)

    @pl.when(is_ctx)
    def _():
        emit(pc_ref[0], False)


def _kvproj(px, pc, kv_norm, wk, wvt, cos2, sin2):
    b, n, _ = px.shape
    tm = pc.shape[1]
    assert n % tm == 0
    nt = n // tm
    tab = pl.BlockSpec((tm, 2 * QK_ROPE), lambda bi, i: (jnp.minimum(i, nt - 1), 0))
    return pl.pallas_call(
        _kvproj_kernel,
        grid=(b, nt + 1),
        in_specs=[pl.BlockSpec((1, tm, KV_BLOCK), lambda bi, i: (bi, jnp.minimum(i, nt - 1), 0)),
                  pl.BlockSpec((1, tm, KV_BLOCK), lambda bi, i: (bi, 0, 0)),
                  pl.BlockSpec((1, KV_RANK), lambda bi, i: (0, 0)),
                  _resident(wk.shape), _resident(wvt.shape), tab, tab],
        out_specs=[pl.BlockSpec((1, N_HEADS, tm, QK_HEAD), lambda bi, i: (bi, 0, i, 0)),
                   pl.BlockSpec((1, N_HEADS, V_HEAD, tm), lambda bi, i: (bi, 0, 0, i)),
                   pl.BlockSpec((1, N_HEADS, 8, 128), lambda bi, i: (bi, 0, 0, 0))],
        out_shape=[jax.ShapeDtypeStruct((b, N_HEADS, n + tm, QK_HEAD), BF16),
                   jax.ShapeDtypeStruct((b, N_HEADS, V_HEAD, n + tm), BF16),
                   jax.ShapeDtypeStruct((b, N_HEADS, 8, 128), F32)],
        compiler_params=_params("parallel", "arbitrary"),
        name="kvproj",
    )(px, pc, kv_norm.reshape(1, KV_RANK), wk, wvt, cos2, sin2)


def _attn_kernel(q_ref, k_ref, vt_ref, kmax_ref, o_ref, *, tk_fixed, tk_online):
    q = q_ref[0, 0]
    tq = q.shape[0]
    kv_len = k_ref.shape[2]
    nt = (((1,), (1,)), ((), ()))

    def scores(j, tk):
        return lax.dot_general(k_ref[0, 0, j * tk:(j + 1) * tk, :], q, nt, preferred_element_type=F32)

    def weighted_values(j, tk, p):
        return jnp.dot(vt_ref[0, 0, :, j * tk:(j + 1) * tk], p.astype(BF16), preferred_element_type=F32)

    def store(acc, l):
        o_ref[0] = jnp.transpose(acc / l).astype(o_ref.dtype)

    qf = q.astype(F32)
    qn2 = lax.dot_general(jnp.ones((8, q.shape[1]), BF16), (qf * qf).astype(BF16), nt,
                          preferred_element_type=F32)[:1]
    ref = jnp.sqrt(qn2 * kmax_ref[0, 0, :1, :1]) * ATTN_BOUND_SLACK
    l = jnp.zeros((1, tq), F32)
    acc = jnp.zeros((V_HEAD, tq), F32)
    for j in range(kv_len // tk_fixed):
        p = jnp.exp2(scores(j, tk_fixed) - ref)
        l = l + jnp.sum(p, axis=0, keepdims=True)
        acc = acc + weighted_values(j, tk_fixed, p)
    store(acc, l)

    @pl.when(jnp.logical_not(jnp.min(l) >= ATTN_MIN_DENOM))
    def _():
        n_blocks = kv_len // tk_online
        m = jnp.full((1, tq), -jnp.inf, F32)
        l = jnp.zeros((1, tq), F32)
        acc = jnp.zeros((V_HEAD, tq), F32)
        st_next = scores(0, tk_online)
        for j in range(n_blocks):
            st = st_next
            if j + 1 < n_blocks:
                st_next = scores(j + 1, tk_online)
            m_new = jnp.maximum(m, jnp.max(st, axis=0, keepdims=True))
            alpha = jnp.exp2(m - m_new)
            p = jnp.exp2(st - m_new)
            l = alpha * l + jnp.sum(p, axis=0, keepdims=True)
            acc = alpha * acc + weighted_values(j, tk_online, p)
            m = m_new
        store(acc, l)


def _attention(q, k_all, vt_all, kmax, kv_len, kv_block0):
    b, h, n, _ = q.shape
    tq = min(ATTN_TQ, n)
    tk_fixed, tk_online = min(ATTN_TK_FIXED, kv_len), min(ATTN_TK_ONLINE, kv_len)
    assert kv_len % tk_fixed == 0 and kv_len % tk_online == 0
    return pl.pallas_call(
        functools.partial(_attn_kernel, tk_fixed=tk_fixed, tk_online=tk_online),
        grid=(b, h, n // tq),
        in_specs=[pl.BlockSpec((1, 1, tq, QK_HEAD), lambda bi, hi, i: (bi, hi, i, 0)),
                  pl.BlockSpec((1, 1, kv_len, QK_HEAD), lambda bi, hi, i: (bi, hi, kv_block0, 0)),
                  pl.BlockSpec((1, 1, V_HEAD, kv_len), lambda bi, hi, i: (bi, hi, 0, kv_block0)),
                  pl.BlockSpec((1, 1, 8, 128), lambda bi, hi, i: (bi, hi, 0, 0))],
        out_specs=pl.BlockSpec((1, tq, V_HEAD), lambda bi, hi, i: (bi, i, hi)),
        out_shape=jax.ShapeDtypeStruct((b, n, h * V_HEAD), BF16),
        compiler_params=_params("parallel", "parallel", "parallel"),
        name="attention",
    )(q, k_all, vt_all, kmax)


def _pool_kernel(u_ref, prev_ref, next_ref, w_ref, s_ref, o_ref, ext_sc, *, n, tm):
    i = pl.program_id(1)
    keep_prev = jnp.where(i > 0, 1.0, 0.0)
    keep_next = jnp.where(i < pl.num_programs(1) - 1, 1.0, 0.0)
    ext_sc[0:POOL_HALO] = prev_ref[0] * keep_prev
    ext_sc[POOL_HALO:POOL_HALO + tm] = u_ref[0]
    ext_sc[POOL_HALO + tm:] = next_ref[0] * keep_next
    t = i * tm + lax.broadcasted_iota(jnp.int32, (tm, 1), 0)
    for g, win in enumerate(POOL_WINDOWS):
        half = win // 2
        cols = slice(g * POOL_GROUP, (g + 1) * POOL_GROUP)
        tot = ext_sc[POOL_HALO - half:POOL_HALO - half + tm, cols]
        for d in range(1 - half, half):
            tot = tot + ext_sc[POOL_HALO + d:POOL_HALO + d + tm, cols]
        cnt = (jnp.minimum(t + half, n) - jnp.maximum(t - half, 0)).astype(F32)
        pooled = (tot / cnt - ext_sc[POOL_HALO:POOL_HALO + tm, cols]).astype(BF16)
        y = jnp.dot(pooled, w_ref[g], preferred_element_type=F32) * s_ref[:, cols]
        o_ref[0, :, cols] = y.astype(o_ref.dtype)


def _pool(p, w_pool, pool_scale):
    b, n, _ = p.shape
    tm = min(512, n)
    hb = tm // POOL_HALO
    last_hb = n // POOL_HALO - 1
    return pl.pallas_call(
        functools.partial(_pool_kernel, n=n, tm=tm),
        grid=(b, n // tm),
        in_specs=[pl.BlockSpec((1, tm, POOL_WIDTH), lambda bi, i: (bi, i, 0)),
                  pl.BlockSpec((1, POOL_HALO, POOL_WIDTH), lambda bi, i: (bi, jnp.maximum(i * hb - 1, 0), 0)),
                  pl.BlockSpec((1, POOL_HALO, POOL_WIDTH),
                               lambda bi, i: (bi, jnp.minimum((i + 1) * hb, last_hb), 0)),
                  _resident(w_pool.shape),
                  pl.BlockSpec((1, POOL_WIDTH), lambda bi, i: (0, 0))],
        out_specs=pl.BlockSpec((1, tm, POOL_WIDTH), lambda bi, i: (bi, i, 0)),
        out_shape=jax.ShapeDtypeStruct((b, n, POOL_WIDTH), BF16),
        scratch_shapes=[pltpu.VMEM((tm + 2 * POOL_HALO, POOL_WIDTH), F32)],
        compiler_params=_params("parallel", "parallel"),
        name="pool",
    )(p, p, p, w_pool, pool_scale.reshape(1, POOL_WIDTH))


def _dft1_kernel(x_ref, g_ref, sh_ref, sc_ref, a_ref, o_ref, h_sc):
    gs = g_ref[...] * (1.0 + sc_ref[0])
    sh = sh_ref[0]
    for t1 in range(DFT_N1):
        x = x_ref[0, t1]
        inv = lax.rsqrt(jnp.mean(x * x, axis=-1, keepdims=True) + EPS)
        h_sc[t1 * DFT_ROWS:(t1 + 1) * DFT_ROWS, :] = (x * inv * gs + sh).astype(h_sc.dtype)
    for c in range(0, D_MODEL, DFT_COLS):
        y = jnp.dot(a_ref[...], h_sc[:, c:c + DFT_COLS], preferred_element_type=F32)
        o_ref[0, :, :, :, c:c + DFT_COLS] = y.astype(o_ref.dtype).reshape(2, DFT_N1, DFT_ROWS, DFT_COLS)


def _dft_stage1(x, g, sh, sc, a_kron):
    b, n, d = x.shape
    vec = pl.BlockSpec((1, 1, d), lambda bi, r: (bi, 0, 0))
    return pl.pallas_call(
        _dft1_kernel,
        grid=(b, DFT_N2 // DFT_ROWS),
        in_specs=[pl.BlockSpec((1, DFT_N1, DFT_ROWS, d), lambda bi, r: (bi, 0, r, 0)),
                  pl.BlockSpec((1, d), lambda bi, r: (0, 0)), vec, vec,
                  _resident(a_kron.shape)],
        out_specs=pl.BlockSpec((1, 2, DFT_N1, DFT_ROWS, d), lambda bi, r: (bi, 0, 0, r, 0)),
        out_shape=jax.ShapeDtypeStruct((b, 2, DFT_N1, DFT_N2, d), BF16),
        scratch_shapes=[pltpu.VMEM((DFT_N1 * DFT_ROWS, d), BF16)],
        compiler_params=_params("parallel", "parallel"),
        name="dft_stage1",
    )(x.reshape(b, DFT_N1, DFT_N2, d), g.reshape(1, d), sh, sc, a_kron)


def _dft2_rows(y, m, cc_ref, sc_ref, rows_out, seq_len, store):
    pq = jnp.dot(m, y, preferred_element_type=F32).astype(BF16)
    ortho = 1.0 / math.sqrt(seq_len * FOURIER_GROUP)
    for g in range(FOURIER_GROUPS):
        cols = slice(g * FOURIER_GROUP, (g + 1) * FOURIER_GROUP)
        f = (jnp.dot(pq[:rows_out, cols], cc_ref[...], preferred_element_type=F32)
             - jnp.dot(pq[rows_out:, cols], sc_ref[...], preferred_element_type=F32))
        store(cols, f * ortho)


def _dft2_kernel(y_ref, m_ref, cc_ref, sc_ref, o_ref):
    for s in range(y_ref.shape[2]):
        y = jnp.concatenate([y_ref[0, 0, s], y_ref[0, 1, s]], axis=0)

        def store(cols, f, s=s):
            o_ref[0, :, s * D_MODEL + cols.start:s * D_MODEL + cols.stop] = f.astype(o_ref.dtype)

        _dft2_rows(y, m_ref[s], cc_ref, sc_ref, DFT_N2, DFT_N1 * DFT_N2, store)


def _dft_small_kernel(h_ref, m_ref, cc_ref, sc_ref, o_ref):
    def store(cols, f):
        o_ref[0, :, cols] = f.astype(o_ref.dtype)

    n = h_ref.shape[1]
    _dft2_rows(h_ref[0], m_ref[0], cc_ref, sc_ref, n, n, store)


def _dft_stage2(y, m, cc, sc):
    b = y.shape[0]
    return pl.pallas_call(
        _dft2_kernel,
        grid=(b, DFT_N1 // DFT2_K1),
        in_specs=[pl.BlockSpec((1, 2, DFT2_K1, DFT_N2, D_MODEL), lambda bi, k: (bi, 0, k, 0, 0)),
                  pl.BlockSpec((DFT2_K1, 2 * DFT_N2, 2 * DFT_N2), lambda bi, k: (k, 0, 0)),
                  _resident(cc.shape), _resident(sc.shape)],
        out_specs=pl.BlockSpec((1, DFT_N2, DFT2_K1 * D_MODEL), lambda bi, k: (bi, 0, k)),
        out_shape=jax.ShapeDtypeStruct((b, DFT_N2, DFT_N1 * D_MODEL), BF16),
        compiler_params=_params("parallel", "parallel"),
        name="dft_stage2",
    )(y, m, cc, sc)


def _dft_small(h, m, cc, sc):
    b, n, d = h.shape
    return pl.pallas_call(
        _dft_small_kernel,
        grid=(b,),
        in_specs=[pl.BlockSpec((1, n, d), lambda bi: (bi, 0, 0)),
                  _resident(m.shape), _resident(cc.shape), _resident(sc.shape)],
        out_specs=pl.BlockSpec((1, n, d), lambda bi: (bi, 0, 0)),
        out_shape=jax.ShapeDtypeStruct((b, n, d), BF16),
        compiler_params=_params("parallel"),
        name="dft_small",
    )(h, m, cc, sc)


def _cos_sin(num, den):
    ang = (num % den).astype(F32) * (2.0 * math.pi / den)
    return jnp.cos(ang), jnp.sin(ang)


def _dft_tables(n_ctx):
    i1 = jnp.arange(DFT_N1, dtype=jnp.int32)
    c1, s1 = _cos_sin(i1[:, None] * i1[None, :], DFT_N1)
    a = jnp.concatenate([c1, -s1], axis=0)
    eye = jnp.eye(DFT_ROWS, dtype=F32)
    a_kron = jnp.einsum('pt,jk->pjtk', a, eye).reshape(2 * DFT_N1 * DFT_ROWS, DFT_N1 * DFT_ROWS)
    n = DFT_N1 * DFT_N2
    t2 = jnp.arange(DFT_N2, dtype=jnp.int32)
    k = i1[:, None, None] + DFT_N1 * t2[None, :, None]
    ck, sk = _cos_sin(k * t2[None, None, :], n)
    m = jnp.concatenate([jnp.concatenate([ck, sk], axis=2),
                         jnp.concatenate([sk, -ck], axis=2)], axis=1)
    ic = jnp.arange(FOURIER_GROUP, dtype=jnp.int32)
    cc, sc = _cos_sin(ic[:, None] * ic[None, :], FOURIER_GROUP)
    il = jnp.arange(n_ctx, dtype=jnp.int32)
    cl, sl = _cos_sin(il[:, None] * il[None, :], n_ctx)
    m_ctx = jnp.concatenate([cl, sl], axis=0)[None]

    return dict(a_kron=a_kron.astype(BF16), m=m.astype(BF16), m_ctx=m_ctx.astype(BF16),
                cc=cc.astype(BF16), sc=sc.astype(BF16))


def _mlp_kernel(x_ref, g_ref, sh_ref, sc_ref, gate_ref, w1_hbm, w2_hbm, *rest, layer, final):
    o_ref, h_sc, acc_sc, w1_buf, w2_buf, sem = rest[-6:]
    tm = x_ref.shape[1]
    tf = w1_buf.shape[2]
    n_chunks = w1_hbm.shape[2] // tf
    chunks = [slice(r, r + MLP_ROWS) for r in range(0, tm, MLP_ROWS)]
    step = pl.program_id(0) * pl.num_programs(1) + pl.program_id(1)
    n_steps = pl.num_programs(0) * pl.num_programs(1)

    def weight_copies(j):
        slot = j % 2
        return (pltpu.make_async_copy(w1_hbm.at[layer, :, pl.ds(j * tf, tf)], w1_buf.at[slot], sem.at[slot, 0]),
                pltpu.make_async_copy(w2_hbm.at[layer, pl.ds(j * tf, tf), :], w2_buf.at[slot], sem.at[slot, 1]))

    def start(j):
        for cp in weight_copies(j):
            cp.start()

    def wait(j):
        for cp in weight_copies(j):
            cp.wait()

    def hidden(rows, slot):
        a = jnp.maximum(jnp.dot(h_sc[rows, :], w1_buf[slot], preferred_element_type=F32), 0.0)
        return (a * a).astype(BF16)

    @pl.when(step == 0)
    def _():
        start(0)

    for j in range(n_chunks):
        slot = j % 2
        if j + 1 < n_chunks:
            start(j + 1)
        else:
            @pl.when(step + 1 < n_steps)
            def _():
                start(0)
        wait(j)
        if j == 0:
            parts = []
            for rows in chunks:
                _normmod_rows(x_ref, g_ref, sh_ref, sc_ref, h_sc, rows.start, rows.stop)
                parts.append(hidden(rows, slot))
            acc_sc[...] = jnp.dot(jnp.concatenate(parts, axis=0), w2_buf[slot], preferred_element_type=F32)
        elif j + 1 < n_chunks:
            acc_sc[...] += jnp.dot(hidden(slice(None), slot), w2_buf[slot], preferred_element_type=F32)
        else:
            gate = gate_ref[0]
            a = hidden(slice(None), slot)
            for rows in chunks:
                upd = acc_sc[rows, :] + jnp.dot(a[rows, :], w2_buf[slot], preferred_element_type=F32)
                for r in range(0, MLP_ROWS, NORM_ROWS):
                    y = x_ref[0, rows.start + r:rows.start + r + NORM_ROWS, :] + gate * upd[r:r + NORM_ROWS, :]
                    if final:
                        y = _rms(y, rest[0][...])
                    o_ref[0, rows.start + r:rows.start + r + NORM_ROWS, :] = y


def _mlp(x, g, sh, sc, gate, w1, w2, layer, final_g=None):
    b, n, d = x.shape
    tm = min(MLP_TM, n)
    tf = MLP_TF
    assert (D_FF // tf) % 2 == 0 and D_FF // tf >= 2
    vec = pl.BlockSpec((1, 1, d), lambda bi, i: (bi, 0, 0))
    row = pl.BlockSpec((1, d), lambda bi, i: (0, 0))
    hbm = pl.BlockSpec(memory_space=pl.ANY)
    in_specs = [pl.BlockSpec((1, tm, d), lambda bi, i: (bi, i, 0)), row, vec, vec, vec, hbm, hbm]
    args = [x, g.reshape(1, d), sh, sc, gate, w1, w2]
    if final_g is not None:
        in_specs.append(row)
        args.append(final_g.reshape(1, d))
    return pl.pallas_call(
        functools.partial(_mlp_kernel, layer=layer, final=final_g is not None),
        grid=(b, n // tm),
        in_specs=in_specs,
        out_specs=pl.BlockSpec((1, tm, d), lambda bi, i: (bi, i, 0)),
        out_shape=jax.ShapeDtypeStruct((b, n, d), F32),
        scratch_shapes=[pltpu.VMEM((tm, d), BF16), pltpu.VMEM((tm, d), F32),
                        pltpu.VMEM((2, d, tf), BF16), pltpu.VMEM((2, tf, d), BF16),
                        pltpu.SemaphoreType.DMA((2, 2))],
        compiler_params=_params("arbitrary", "arbitrary"),
        name="mlp",
    )(*args)


def _rotate_half_axial(x):
    xr = x.reshape(x.shape[:-1] + (2, 2, ROPE_FREQS))
    return jnp.concatenate([-xr[..., 1:, :], xr[..., :1, :]], axis=-2).reshape(x.shape)


def _rope_tables(n):
    rows = n // GRID_W
    r = jnp.broadcast_to(jnp.arange(rows, dtype=F32)[:, None], (rows, GRID_W)).reshape(n)
    col = jnp.broadcast_to(jnp.arange(GRID_W, dtype=F32)[None, :], (rows, GRID_W)).reshape(n)
    inv = ROPE_THETA ** (-2.0 * jnp.arange(ROPE_FREQS, dtype=F32) / ROPE_AXIS)
    ang = jnp.stack([r[:, None] * inv, col[:, None] * inv], axis=1)
    ang = jnp.broadcast_to(ang[:, :, None, :], (n, 2, 2, ROPE_FREQS)).reshape(n, QK_ROPE)
    zeros = jnp.zeros((n, QK_ROPE), F32)
    return (jnp.concatenate([jnp.cos(ang), zeros], axis=1), jnp.concatenate([jnp.sin(ang), zeros], axis=1))


def _even_weights(w_in, w_uq, w_ukv):
    w_kr = w_in[:, Q_RANK + KV_RANK:Q_RANK + KV_RANK + QK_ROPE]
    w_in2 = jnp.concatenate([w_in[:, Q_RANK + KV_RANK + QK_ROPE:], w_in[:, :Q_RANK],
                             w_in[:, Q_RANK:Q_RANK + KV_RANK], w_kr, _rotate_half_axial(w_kr)], axis=1)
    wq = w_uq.reshape(Q_RANK, N_HEADS, QK_HEAD)
    wq = jnp.concatenate([wq, _rotate_half_axial(wq[..., QK_NOPE:])], axis=-1)
    wq = wq.reshape(Q_RANK, N_HEADS * (QK_HEAD + QK_ROPE))
    wkv = w_ukv.reshape(KV_RANK, N_HEADS, QK_NOPE + V_HEAD)
    wk = wkv[..., :QK_NOPE].reshape(KV_RANK, N_HEADS * QK_NOPE)
    wvt = jnp.transpose(wkv[..., QK_NOPE:], (1, 2, 0)).reshape(N_HEADS * V_HEAD, KV_RANK)
    return w_in2.astype(BF16), wq.astype(BF16), wk.astype(BF16), wvt.astype(BF16)


def kernel(x, c, ctx, c_ctx, w_mod, b_mod, norm1, norm2, w_in, q_norm, w_uq, kv_norm, w_ukv, w_pool,
           pool_scale, w_out_even, w_out_odd, w_mlp1, w_mlp2, final_norm):
    b, n, d = x.shape
    n_ctx = ctx.shape[1]
    assert n == DFT_N1 * DFT_N2 and d == D_MODEL and n % GRID_W == 0
    n_keys = n + n_ctx
    assert n % 512 == 0 and n % n_ctx == 0

    cos2, sin2 = _rope_tables(n)
    dft = _dft_tables(n_ctx)
    w1 = w_mlp1.astype(BF16)
    w2 = w_mlp2.astype(BF16)

    cc = jnp.concatenate([c, c_ctx[None], jnp.zeros((8 - b - 1, d), F32)], axis=0)
    mods = _modulation(cc, w_mod, b_mod)

    updates_ctx = [False] * DEPTH
    for l in reversed(range(DEPTH - 1)):
        updates_ctx[l] = (l + 1) % 2 == 0 or updates_ctx[l + 1]

    for l in range(DEPTH):
        even = l % 2 == 0
        i = l // 2
        mx = [mods[l, :b, k * d:(k + 1) * d].reshape(b, 1, d) for k in range(6)]
        mc = [jnp.broadcast_to(mods[l, b, k * d:(k + 1) * d].reshape(1, 1, d), (b, 1, d)) for k in range(6)]
        ctx_update = updates_ctx[l]

        if even:
            w_in2, wq, wk, wvt = _even_weights(w_in[i], w_uq[i], w_ukv[i])
            w_out = w_out_even[i].astype(BF16)
            wp = w_pool[i].astype(BF16)
            ux, kvx, q = _evenproj(x, norm1[l], mx[0], mx[1], w_in2, q_norm[i], wq, cos2, sin2, rope=True)
            uc, kvc, cq = _evenproj(ctx, norm1[l], mc[0], mc[1], w_in2, q_norm[i], wq, cos2, sin2, rope=False)
            k_all, vt_all, kmax = _kvproj(kvx, kvc, kv_norm[i], wk, wvt, cos2, sin2)
            attn_x = _attention(q, k_all, vt_all, kmax, n_keys, 0)
            pool_x = _pool(ux, wp, pool_scale[i])
            x = _rowmm([attn_x, pool_x], [w_out[:MLA_OUT], w_out[MLA_OUT:]], F32, res=x, gate=mx[2])
            if ctx_update:
                attn_c = _attention(cq, k_all, vt_all, kmax, n_ctx, n // n_ctx)
                pool_c = _pool(uc, wp, pool_scale[i])
                ctx = _rowmm([attn_c, pool_c], [w_out[:MLA_OUT], w_out[MLA_OUT:]], F32, res=ctx, gate=mc[2])
        else:
            w_out = w_out_odd[i].astype(BF16)
            if ctx_update:
                hc = _normmod(ctx, norm1[l], mc[0], mc[1])
            y1 = _dft_stage1(x, norm1[l], mx[0], mx[1], dft['a_kron'])
            fx = _dft_stage2(y1, dft['m'], dft['cc'], dft['sc'])
            x = _rowmm_strided(fx, w_out, x, mx[2])
            if ctx_update:
                fc = _dft_small(hc, dft['m_ctx'], dft['cc'], dft['sc'])
                ctx = _rowmm([fc], [w_out], F32, res=ctx, gate=mc[2])
        x = _mlp(x, norm2[l], mx[3], mx[4], mx[5], w1, w2, l, final_norm if l == DEPTH - 1 else None)
        if ctx_update:
            ctx = _mlp(ctx.reshape(1, b * n_ctx, d), norm2[l], mc[3][:1], mc[4][:1], mc[5][:1],
                       w1, w2, l).reshape(b, n_ctx, d)
    return x
```

```python
import functools
import math

import jax
import jax.numpy as jnp
from jax import lax
from jax.experimental import pallas as pl
from jax.experimental.pallas import tpu as pltpu

D_MODEL = 2048
DEPTH = 4
GRID_W = 64
N_HEADS = 8
Q_RANK = 512
KV_RANK = 256
QK_NOPE = 128
QK_ROPE = 64
V_HEAD = 128
QK_HEAD = QK_NOPE + QK_ROPE
ATTN_SCALE = QK_HEAD ** -0.5
Q_SCALE = ATTN_SCALE * math.log2(math.e)
MLA_OUT = N_HEADS * V_HEAD
ROPE_AXIS = QK_ROPE // 2
ROPE_FREQS = ROPE_AXIS // 2
ROPE_THETA = 10000.0
POOL_WIDTH = D_MODEL // 2
POOL_WINDOWS = (2, 4, 8, 16)
POOL_GROUP = POOL_WIDTH // len(POOL_WINDOWS)
POOL_HALO = 8
FOURIER_GROUPS = 4
FOURIER_GROUP = D_MODEL // FOURIER_GROUPS
D_FF = 4 * D_MODEL
EPS = 1e-6

P_POOL = 0
P_QLAT = POOL_WIDTH
P_KV = POOL_WIDTH + Q_RANK
P_WIDTH = POOL_WIDTH + Q_RANK + KV_RANK + 2 * QK_ROPE
KV_BLOCK = KV_RANK + 2 * QK_ROPE

DFT_N1 = 64
DFT_N2 = 128
DFT_ROWS = 16
DFT2_K1 = 8
DFT_STRIDE_BLOCK = 8
DFT_COLS = 512

NORM_ROWS = 16
MLP_TM = 512
MLP_TF = 1024
MLP_ROWS = 256
ATTN_TQ = 1024
ATTN_TK_FIXED = 2816
ATTN_TK_ONLINE = 384
ATTN_BOUND_SLACK = 1.02
ATTN_MIN_DENOM = 2.0 ** -60

V7X_VMEM_LIMIT = 56 * 1024 * 1024

F32 = jnp.float32
BF16 = jnp.bfloat16


def _params(*sem):
    return pltpu.CompilerParams(dimension_semantics=sem, vmem_limit_bytes=V7X_VMEM_LIMIT)


def _resident(shape):
    nd = len(shape)
    return pl.BlockSpec(shape, lambda *_: (0,) * nd, pipeline_mode=pl.Buffered(1))


def _rms(x, g):
    return x * lax.rsqrt(jnp.mean(x * x, axis=-1, keepdims=True) + EPS) * g


def _mod_kernel(c_ref, w_ref, b_ref, o_ref):
    c = c_ref[...]
    s = c * (1.0 / (1.0 + jnp.exp(-c)))
    o_ref[0] = jnp.dot(s, w_ref[0], preferred_element_type=F32, precision=lax.Precision.HIGHEST) + b_ref[0]


def _modulation(cc, w_mod, b_mod):
    rows = cc.shape[0]
    tn = 1536
    n6 = 6 * D_MODEL
    return pl.pallas_call(
        _mod_kernel,
        grid=(DEPTH, n6 // tn),
        in_specs=[
            pl.BlockSpec((rows, D_MODEL), lambda l, j: (0, 0)),
            pl.BlockSpec((1, D_MODEL, tn), lambda l, j: (l, 0, j)),
            pl.BlockSpec((1, 1, tn), lambda l, j: (l, 0, j)),
        ],
        out_specs=pl.BlockSpec((1, rows, tn), lambda l, j: (l, 0, j)),
        out_shape=jax.ShapeDtypeStruct((DEPTH, rows, n6), F32),
        compiler_params=_params("parallel", "parallel"),
        name="modulation",
    )(cc, w_mod, b_mod.reshape(DEPTH, 1, n6))


def _normmod_rows(x_ref, g_ref, sh_ref, sc_ref, dst_ref, start=0, stop=None):
    gs = g_ref[...] * (1.0 + sc_ref[0])
    sh = sh_ref[0]
    for r in range(start, x_ref.shape[1] if stop is None else stop, NORM_ROWS):
        x = x_ref[0, r:r + NORM_ROWS, :]
        inv = lax.rsqrt(jnp.mean(x * x, axis=-1, keepdims=True) + EPS)
        dst_ref[r:r + NORM_ROWS, :] = (x * inv * gs + sh).astype(dst_ref.dtype)


def _normmod_kernel(x_ref, g_ref, sh_ref, sc_ref, o_ref):
    _normmod_rows(x_ref, g_ref, sh_ref, sc_ref, o_ref.at[0])


def _normmod(x, g, sh, sc, out_dtype=BF16):
    b, n, d = x.shape
    tm = min(512, n)
    vec = pl.BlockSpec((1, 1, d), lambda bi, i: (bi, 0, 0))
    return pl.pallas_call(
        _normmod_kernel,
        grid=(b, n // tm),
        in_specs=[pl.BlockSpec((1, tm, d), lambda bi, i: (bi, i, 0)),
                  pl.BlockSpec((1, d), lambda bi, i: (0, 0)), vec, vec],
        out_specs=pl.BlockSpec((1, tm, d), lambda bi, i: (bi, i, 0)),
        out_shape=jax.ShapeDtypeStruct((b, n, d), out_dtype),
        compiler_params=_params("parallel", "parallel"),
        name="normmod",
    )(x, g.reshape(1, d), sh, sc)


def _rope_cols(t, cos2, sin2, rope):
    if not rope:
        return t
    return t * cos2 + pltpu.roll(t, QK_ROPE, 1) * sin2


def _evenproj_kernel(x_ref, g_ref, sh_ref, sc_ref, w_ref, qg_ref, wq_ref, c_ref, s_ref,
                     u_ref, kv_ref, q_ref, h_sc, *, rope):
    width = QK_NOPE + 2 * QK_ROPE
    for r in range(0, x_ref.shape[1], MLP_ROWS):
        rows = slice(r, r + MLP_ROWS)
        _normmod_rows(x_ref, g_ref, sh_ref, sc_ref, h_sc, r, r + MLP_ROWS)
        p = jnp.dot(h_sc[rows, :], w_ref[...], preferred_element_type=F32)
        u_ref[0, rows, :] = p[:, P_POOL:P_POOL + POOL_WIDTH]
        kv_ref[0, rows, :] = p[:, P_KV:]
        z = _rms(p[:, P_QLAT:P_QLAT + Q_RANK], qg_ref[...]).astype(BF16)
        acc = jnp.dot(z, wq_ref[...], preferred_element_type=F32)
        for h in range(N_HEADS):
            q_ref[0, h, rows, :QK_NOPE] = (acc[:, h * width:h * width + QK_NOPE] * Q_SCALE).astype(q_ref.dtype)
            t = _rope_cols(acc[:, h * width + QK_NOPE:(h + 1) * width], c_ref[rows, :], s_ref[rows, :], rope)
            q_ref[0, h, rows, QK_NOPE:] = (t[:, :QK_ROPE] * Q_SCALE).astype(q_ref.dtype)


def _evenproj(x, g, sh, sc, w_in2, q_norm, wq, cos2, sin2, rope):
    b, n, d = x.shape
    tm = min(512, n)
    vec = pl.BlockSpec((1, 1, d), lambda bi, i: (bi, 0, 0))
    tab = pl.BlockSpec((tm, 2 * QK_ROPE), lambda bi, i: (i, 0))
    return pl.pallas_call(
        functools.partial(_evenproj_kernel, rope=rope),
        grid=(b, n // tm),
        in_specs=[pl.BlockSpec((1, tm, d), lambda bi, i: (bi, i, 0)),
                  pl.BlockSpec((1, d), lambda bi, i: (0, 0)), vec, vec, _resident(w_in2.shape),
                  pl.BlockSpec((1, Q_RANK), lambda bi, i: (0, 0)), _resident(wq.shape), tab, tab],
        out_specs=[pl.BlockSpec((1, tm, POOL_WIDTH), lambda bi, i: (bi, i, 0)),
                   pl.BlockSpec((1, tm, KV_BLOCK), lambda bi, i: (bi, i, 0)),
                   pl.BlockSpec((1, N_HEADS, tm, QK_HEAD), lambda bi, i: (bi, 0, i, 0))],
        out_shape=[jax.ShapeDtypeStruct((b, n, POOL_WIDTH), F32),
                   jax.ShapeDtypeStruct((b, n, KV_BLOCK), F32),
                   jax.ShapeDtypeStruct((b, N_HEADS, n, QK_HEAD), BF16)],
        scratch_shapes=[pltpu.VMEM((tm, d), BF16)],
        compiler_params=_params("parallel", "parallel"),
        name="evenproj",
    )(x, g.reshape(1, d), sh, sc, w_in2, q_norm.reshape(1, Q_RANK), wq, cos2, sin2)


def _rowmm_kernel(*refs, n_in, has_res):
    a_refs, w_refs = refs[:n_in], refs[n_in:2 * n_in]
    o_ref = refs[-1]
    acc = None
    for a, w in zip(a_refs, w_refs):
        d = jnp.dot(a[0], w[...], preferred_element_type=F32)
        acc = d if acc is None else acc + d
    if has_res:
        res_ref, gate_ref = refs[2 * n_in], refs[2 * n_in + 1]
        acc = res_ref[0] + gate_ref[0] * acc
    o_ref[0] = acc.astype(o_ref.dtype)


def _rowmm(a_list, w_list, out_dtype, res=None, gate=None, tm=512):
    b, n, _ = a_list[0].shape
    nout = w_list[0].shape[1]
    tm = min(tm, n)
    in_specs = [pl.BlockSpec((1, tm, a.shape[2]), lambda bi, i: (bi, i, 0)) for a in a_list]
    in_specs += [_resident(w.shape) for w in w_list]
    args = list(a_list) + list(w_list)
    if res is not None:
        in_specs += [pl.BlockSpec((1, tm, nout), lambda bi, i: (bi, i, 0)),
                     pl.BlockSpec((1, 1, nout), lambda bi, i: (bi, 0, 0))]
        args += [res, gate]
    return pl.pallas_call(
        functools.partial(_rowmm_kernel, n_in=len(a_list), has_res=res is not None),
        grid=(b, n // tm),
        in_specs=in_specs,
        out_specs=pl.BlockSpec((1, tm, nout), lambda bi, i: (bi, i, 0)),
        out_shape=jax.ShapeDtypeStruct((b, n, nout), out_dtype),
        compiler_params=_params("parallel", "parallel"),
        name="rowmm",
    )(*args)


def _rowmm_strided_kernel(f_ref, w_ref, res_ref, gate_ref, o_ref):
    gate = gate_ref[0]
    d = w_ref.shape[0]
    n2 = f_ref.shape[1]
    steps = res_ref.shape[2]
    rows = jnp.concatenate([f_ref[0, :, s * d:(s + 1) * d] for s in range(steps)], axis=0)
    for c in range(0, d, DFT_COLS):
        cols = slice(c, c + DFT_COLS)
        y = jnp.dot(rows, w_ref[:, cols], preferred_element_type=F32)
        for s in range(steps):
            o_ref[0, :, s, cols] = res_ref[0, :, s, cols] + gate[:, cols] * y[s * n2:(s + 1) * n2, :]


def _rowmm_strided(f, w, res, gate):
    b, n, d = res.shape
    res4 = res.reshape(b, DFT_N2, DFT_N1, d)
    blk = pl.BlockSpec((1, DFT_N2, DFT_STRIDE_BLOCK, d), lambda bi, i: (bi, 0, i, 0))
    out = pl.pallas_call(
        _rowmm_strided_kernel,
        grid=(b, DFT_N1 // DFT_STRIDE_BLOCK),
        in_specs=[pl.BlockSpec((1, DFT_N2, DFT_STRIDE_BLOCK * d), lambda bi, i: (bi, 0, i)),
                  _resident(w.shape), blk,
                  pl.BlockSpec((1, 1, d), lambda bi, i: (bi, 0, 0))],
        out_specs=blk,
        out_shape=jax.ShapeDtypeStruct(res4.shape, F32),
        compiler_params=_params("parallel", "parallel"),
        name="rowmm_strided",
    )(f, w, res4, gate)
    return out.reshape(b, n, d)


def _kvproj_kernel(px_ref, pc_ref, g_ref, wk_ref, wvt_ref, c_ref, s_ref, k_ref, vt_ref, kmax_ref):
    @pl.when(pl.program_id(1) == 0)
    def _():
        kmax_ref[...] = jnp.zeros(kmax_ref.shape, F32)

    def emit(blk, rope):
        z = _rms(blk[:, :KV_RANK], g_ref[...]).astype(BF16)
        kr = _rope_cols(blk[:, KV_RANK:], c_ref[...], s_ref[...], rope)[:, :QK_ROPE].astype(k_ref.dtype)
        kn = jnp.dot(z, wk_ref[...], preferred_element_type=F32).astype(k_ref.dtype)
        vt = lax.dot_general(wvt_ref[...], z, (((1,), (1,)), ((), ())),
                             preferred_element_type=F32).astype(vt_ref.dtype)
        kn2 = kn.astype(F32) * kn.astype(F32)
        kr2 = jnp.sum(kr.astype(F32) * kr.astype(F32), axis=1, keepdims=True)
        for h in range(N_HEADS):
            k_ref[0, h, :, :QK_NOPE] = kn[:, h * QK_NOPE:(h + 1) * QK_NOPE]
            k_ref[0, h, :, QK_NOPE:] = kr
            vt_ref[0, h] = vt[h * V_HEAD:(h + 1) * V_HEAD, :]
            norm2 = jnp.sum(kn2[:, h * QK_NOPE:(h + 1) * QK_NOPE], axis=1, keepdims=True) + kr2
            kmax_ref[0, h] = jnp.maximum(kmax_ref[0, h], jnp.max(norm2, axis=0, keepdims=True))

    is_ctx = pl.program_id(1) == pl.num_programs(1) - 1

    @pl.when(jnp.logical_not(is_ctx))
    def _():
        emit(px_ref[0], True)

    @pl.when(is_ctx)
    def _():
        emit(pc_ref[0], False)


def _kvproj(px, pc, kv_norm, wk, wvt, cos2, sin2):
    b, n, _ = px.shape
    tm = pc.shape[1]
    assert n % tm == 0
    nt = n // tm
    tab = pl.BlockSpec((tm, 2 * QK_ROPE), lambda bi, i: (jnp.minimum(i, nt - 1), 0))
    return pl.pallas_call(
        _kvproj_kernel,
        grid=(b, nt + 1),
        in_specs=[pl.BlockSpec((1, tm, KV_BLOCK), lambda bi, i: (bi, jnp.minimum(i, nt - 1), 0)),
                  pl.BlockSpec((1, tm, KV_BLOCK), lambda bi, i: (bi, 0, 0)),
                  pl.BlockSpec((1, KV_RANK), lambda bi, i: (0, 0)),
                  _resident(wk.shape), _resident(wvt.shape), tab, tab],
        out_specs=[pl.BlockSpec((1, N_HEADS, tm, QK_HEAD), lambda bi, i: (bi, 0, i, 0)),
                   pl.BlockSpec((1, N_HEADS, V_HEAD, tm), lambda bi, i: (bi, 0, 0, i)),
                   pl.BlockSpec((1, N_HEADS, 8, 128), lambda bi, i: (bi, 0, 0, 0))],
        out_shape=[jax.ShapeDtypeStruct((b, N_HEADS, n + tm, QK_HEAD), BF16),
                   jax.ShapeDtypeStruct((b, N_HEADS, V_HEAD, n + tm), BF16),
                   jax.ShapeDtypeStruct((b, N_HEADS, 8, 128), F32)],
        compiler_params=_params("parallel", "arbitrary"),
        name="kvproj",
    )(px, pc, kv_norm.reshape(1, KV_RANK), wk, wvt, cos2, sin2)


def _attn_kernel(q_ref, k_ref, vt_ref, kmax_ref, o_ref, *, tk_fixed, tk_online):
    q = q_ref[0, 0]
    tq = q.shape[0]
    kv_len = k_ref.shape[2]
    nt = (((1,), (1,)), ((), ()))

    def scores(j, tk):
        return lax.dot_general(k_ref[0, 0, j * tk:(j + 1) * tk, :], q, nt, preferred_element_type=F32)

    def weighted_values(j, tk, p):
        return jnp.dot(vt_ref[0, 0, :, j * tk:(j + 1) * tk], p.astype(BF16), preferred_element_type=F32)

    def store(acc, l):
        o_ref[0] = jnp.transpose(acc / l).astype(o_ref.dtype)

    qf = q.astype(F32)
    qn2 = lax.dot_general(jnp.ones((8, q.shape[1]), BF16), (qf * qf).astype(BF16), nt,
                          preferred_element_type=F32)[:1]
    ref = jnp.sqrt(qn2 * kmax_ref[0, 0, :1, :1]) * ATTN_BOUND_SLACK
    l = jnp.zeros((1, tq), F32)
    acc = jnp.zeros((V_HEAD, tq), F32)
    n_fixed = kv_len // tk_fixed
    st_next = scores(0, tk_fixed)
    for j in range(n_fixed):
        st = st_next
        if j + 1 < n_fixed:
            st_next = scores(j + 1, tk_fixed)
        p = jnp.exp2(st - ref)
        l = l + jnp.sum(p, axis=0, keepdims=True)
        acc = acc + weighted_values(j, tk_fixed, p)
    store(acc, l)

    @pl.when(jnp.logical_not(jnp.min(l) >= ATTN_MIN_DENOM))
    def _():
        n_blocks = kv_len // tk_online
        m = jnp.full((1, tq), -jnp.inf, F32)
        l = jnp.zeros((1, tq), F32)
        acc = jnp.zeros((V_HEAD, tq), F32)
        st_next = scores(0, tk_online)
        for j in range(n_blocks):
            st = st_next
            if j + 1 < n_blocks:
                st_next = scores(j + 1, tk_online)
            m_new = jnp.maximum(m, jnp.max(st, axis=0, keepdims=True))
            alpha = jnp.exp2(m - m_new)
            p = jnp.exp2(st - m_new)
            l = alpha * l + jnp.sum(p, axis=0, keepdims=True)
            acc = alpha * acc + weighted_values(j, tk_online, p)
            m = m_new
        store(acc, l)


def _attention(q, k_all, vt_all, kmax, kv_len, kv_block0):
    b, h, n, _ = q.shape
    tq = min(ATTN_TQ, n)
    tk_fixed, tk_online = min(ATTN_TK_FIXED, kv_len), min(ATTN_TK_ONLINE, kv_len)
    assert kv_len % tk_fixed == 0 and kv_len % tk_online == 0
    return pl.pallas_call(
        functools.partial(_attn_kernel, tk_fixed=tk_fixed, tk_online=tk_online),
        grid=(b, h, n // tq),
        in_specs=[pl.BlockSpec((1, 1, tq, QK_HEAD), lambda bi, hi, i: (bi, hi, i, 0)),
                  pl.BlockSpec((1, 1, kv_len, QK_HEAD), lambda bi, hi, i: (bi, hi, kv_block0, 0)),
                  pl.BlockSpec((1, 1, V_HEAD, kv_len), lambda bi, hi, i: (bi, hi, 0, kv_block0)),
                  pl.BlockSpec((1, 1, 8, 128), lambda bi, hi, i: (bi, hi, 0, 0))],
        out_specs=pl.BlockSpec((1, tq, V_HEAD), lambda bi, hi, i: (bi, i, hi)),
        out_shape=jax.ShapeDtypeStruct((b, n, h * V_HEAD), BF16),
        compiler_params=_params("parallel", "parallel", "parallel"),
        name="attention",
    )(q, k_all, vt_all, kmax)


def _pool_kernel(u_ref, prev_ref, next_ref, w_ref, s_ref, o_ref, ext_sc, *, n, tm):
    i = pl.program_id(1)
    keep_prev = jnp.where(i > 0, 1.0, 0.0)
    keep_next = jnp.where(i < pl.num_programs(1) - 1, 1.0, 0.0)
    ext_sc[0:POOL_HALO] = prev_ref[0] * keep_prev
    ext_sc[POOL_HALO:POOL_HALO + tm] = u_ref[0]
    ext_sc[POOL_HALO + tm:] = next_ref[0] * keep_next
    t = i * tm + lax.broadcasted_iota(jnp.int32, (tm, 1), 0)
    for g, win in enumerate(POOL_WINDOWS):
        half = win // 2
        cols = slice(g * POOL_GROUP, (g + 1) * POOL_GROUP)
        tot = ext_sc[POOL_HALO - half:POOL_HALO - half + tm, cols]
        for d in range(1 - half, half):
            tot = tot + ext_sc[POOL_HALO + d:POOL_HALO + d + tm, cols]
        cnt = (jnp.minimum(t + half, n) - jnp.maximum(t - half, 0)).astype(F32)
        pooled = (tot / cnt - ext_sc[POOL_HALO:POOL_HALO + tm, cols]).astype(BF16)
        y = jnp.dot(pooled, w_ref[g], preferred_element_type=F32) * s_ref[:, cols]
        o_ref[0, :, cols] = y.astype(o_ref.dtype)


def _pool(p, w_pool, pool_scale):
    b, n, _ = p.shape
    tm = min(512, n)
    hb = tm // POOL_HALO
    last_hb = n // POOL_HALO - 1
    return pl.pallas_call(
        functools.partial(_pool_kernel, n=n, tm=tm),
        grid=(b, n // tm),
        in_specs=[pl.BlockSpec((1, tm, POOL_WIDTH), lambda bi, i: (bi, i, 0)),
                  pl.BlockSpec((1, POOL_HALO, POOL_WIDTH), lambda bi, i: (bi, jnp.maximum(i * hb - 1, 0), 0)),
                  pl.BlockSpec((1, POOL_HALO, POOL_WIDTH),
                               lambda bi, i: (bi, jnp.minimum((i + 1) * hb, last_hb), 0)),
                  _resident(w_pool.shape),
                  pl.BlockSpec((1, POOL_WIDTH), lambda bi, i: (0, 0))],
        out_specs=pl.BlockSpec((1, tm, POOL_WIDTH), lambda bi, i: (bi, i, 0)),
        out_shape=jax.ShapeDtypeStruct((b, n, POOL_WIDTH), BF16),
        scratch_shapes=[pltpu.VMEM((tm + 2 * POOL_HALO, POOL_WIDTH), F32)],
        compiler_params=_params("parallel", "parallel"),
        name="pool",
    )(p, p, p, w_pool, pool_scale.reshape(1, POOL_WIDTH))


def _dft1_kernel(x_ref, g_ref, sh_ref, sc_ref, a_ref, o_ref, h_sc):
    gs = g_ref[...] * (1.0 + sc_ref[0])
    sh = sh_ref[0]
    for t1 in range(DFT_N1):
        x = x_ref[0, t1]
        inv = lax.rsqrt(jnp.mean(x * x, axis=-1, keepdims=True) + EPS)
        h_sc[t1 * DFT_ROWS:(t1 + 1) * DFT_ROWS, :] = (x * inv * gs + sh).astype(h_sc.dtype)
    for c in range(0, D_MODEL, DFT_COLS):
        y = jnp.dot(a_ref[...], h_sc[:, c:c + DFT_COLS], preferred_element_type=F32)
        o_ref[0, :, :, :, c:c + DFT_COLS] = y.astype(o_ref.dtype).reshape(2, DFT_N1, DFT_ROWS, DFT_COLS)


def _dft_stage1(x, g, sh, sc, a_kron):
    b, n, d = x.shape
    vec = pl.BlockSpec((1, 1, d), lambda bi, r: (bi, 0, 0))
    return pl.pallas_call(
        _dft1_kernel,
        grid=(b, DFT_N2 // DFT_ROWS),
        in_specs=[pl.BlockSpec((1, DFT_N1, DFT_ROWS, d), lambda bi, r: (bi, 0, r, 0)),
                  pl.BlockSpec((1, d), lambda bi, r: (0, 0)), vec, vec,
                  _resident(a_kron.shape)],
        out_specs=pl.BlockSpec((1, 2, DFT_N1, DFT_ROWS, d), lambda bi, r: (bi, 0, 0, r, 0)),
        out_shape=jax.ShapeDtypeStruct((b, 2, DFT_N1, DFT_N2, d), BF16),
        scratch_shapes=[pltpu.VMEM((DFT_N1 * DFT_ROWS, d), BF16)],
        compiler_params=_params("parallel", "parallel"),
        name="dft_stage1",
    )(x.reshape(b, DFT_N1, DFT_N2, d), g.reshape(1, d), sh, sc, a_kron)


def _dft2_rows(y, m, cc_ref, sc_ref, rows_out, seq_len, store):
    pq = jnp.dot(m, y, preferred_element_type=F32).astype(BF16)
    ortho = 1.0 / math.sqrt(seq_len * FOURIER_GROUP)
    for g in range(FOURIER_GROUPS):
        cols = slice(g * FOURIER_GROUP, (g + 1) * FOURIER_GROUP)
        f = (jnp.dot(pq[:rows_out, cols], cc_ref[...], preferred_element_type=F32)
             - jnp.dot(pq[rows_out:, cols], sc_ref[...], preferred_element_type=F32))
        store(cols, f * ortho)


def _dft2_kernel(y_ref, m_ref, cc_ref, sc_ref, o_ref):
    for s in range(y_ref.shape[2]):
        y = jnp.concatenate([y_ref[0, 0, s], y_ref[0, 1, s]], axis=0)

        def store(cols, f, s=s):
            o_ref[0, :, s * D_MODEL + cols.start:s * D_MODEL + cols.stop] = f.astype(o_ref.dtype)

        _dft2_rows(y, m_ref[s], cc_ref, sc_ref, DFT_N2, DFT_N1 * DFT_N2, store)


def _dft_small_kernel(h_ref, m_ref, cc_ref, sc_ref, o_ref):
    def store(cols, f):
        o_ref[0, :, cols] = f.astype(o_ref.dtype)

    n = h_ref.shape[1]
    _dft2_rows(h_ref[0], m_ref[0], cc_ref, sc_ref, n, n, store)


def _dft_stage2(y, m, cc, sc):
    b = y.shape[0]
    return pl.pallas_call(
        _dft2_kernel,
        grid=(b, DFT_N1 // DFT2_K1),
        in_specs=[pl.BlockSpec((1, 2, DFT2_K1, DFT_N2, D_MODEL), lambda bi, k: (bi, 0, k, 0, 0)),
                  pl.BlockSpec((DFT2_K1, 2 * DFT_N2, 2 * DFT_N2), lambda bi, k: (k, 0, 0)),
                  _resident(cc.shape), _resident(sc.shape)],
        out_specs=pl.BlockSpec((1, DFT_N2, DFT2_K1 * D_MODEL), lambda bi, k: (bi, 0, k)),
        out_shape=jax.ShapeDtypeStruct((b, DFT_N2, DFT_N1 * D_MODEL), BF16),
        compiler_params=_params("parallel", "parallel"),
        name="dft_stage2",
    )(y, m, cc, sc)


def _dft_small(h, m, cc, sc):
    b, n, d = h.shape
    return pl.pallas_call(
        _dft_small_kernel,
        grid=(b,),
        in_specs=[pl.BlockSpec((1, n, d), lambda bi: (bi, 0, 0)),
                  _resident(m.shape), _resident(cc.shape), _resident(sc.shape)],
        out_specs=pl.BlockSpec((1, n, d), lambda bi: (bi, 0, 0)),
        out_shape=jax.ShapeDtypeStruct((b, n, d), BF16),
        compiler_params=_params("parallel"),
        name="dft_small",
    )(h, m, cc, sc)


def _cos_sin(num, den):
    ang = (num % den).astype(F32) * (2.0 * math.pi / den)
    return jnp.cos(ang), jnp.sin(ang)


def _dft_tables(n_ctx):
    i1 = jnp.arange(DFT_N1, dtype=jnp.int32)
    c1, s1 = _cos_sin(i1[:, None] * i1[None, :], DFT_N1)
    a = jnp.concatenate([c1, -s1], axis=0)
    eye = jnp.eye(DFT_ROWS, dtype=F32)
    a_kron = jnp.einsum('pt,jk->pjtk', a, eye).reshape(2 * DFT_N1 * DFT_ROWS, DFT_N1 * DFT_ROWS)
    n = DFT_N1 * DFT_N2
    t2 = jnp.arange(DFT_N2, dtype=jnp.int32)
    k = i1[:, None, None] + DFT_N1 * t2[None, :, None]
    ck, sk = _cos_sin(k * t2[None, None, :], n)
    m = jnp.concatenate([jnp.concatenate([ck, sk], axis=2),
                         jnp.concatenate([sk, -ck], axis=2)], axis=1)
    ic = jnp.arange(FOURIER_GROUP, dtype=jnp.int32)
    cc, sc = _cos_sin(ic[:, None] * ic[None, :], FOURIER_GROUP)
    il = jnp.arange(n_ctx, dtype=jnp.int32)
    cl, sl = _cos_sin(il[:, None] * il[None, :], n_ctx)
    m_ctx = jnp.concatenate([cl, sl], axis=0)[None]

    return dict(a_kron=a_kron.astype(BF16), m=m.astype(BF16), m_ctx=m_ctx.astype(BF16),
                cc=cc.astype(BF16), sc=sc.astype(BF16))


def _mlp_kernel(x_ref, g_ref, sh_ref, sc_ref, gate_ref, w1_ref, w2_ref, *rest, final):
    o_ref, h_sc, acc_sc = rest[-3:]
    j = pl.program_id(2)
    last = pl.num_programs(2) - 1
    tm = x_ref.shape[1]
    chunks = [slice(r, r + MLP_ROWS) for r in range(0, tm, MLP_ROWS)]

    def hidden(rows):
        a = jnp.maximum(jnp.dot(h_sc[rows, :], w1_ref[...], preferred_element_type=F32), 0.0)
        return (a * a).astype(BF16)

    @pl.when(j == 0)
    def _():
        parts = []
        for rows in chunks:
            _normmod_rows(x_ref, g_ref, sh_ref, sc_ref, h_sc, rows.start, rows.stop)
            parts.append(hidden(rows))
        acc_sc[...] = jnp.dot(jnp.concatenate(parts, axis=0), w2_ref[...], preferred_element_type=F32)

    @pl.when(jnp.logical_and(j > 0, j < last))
    def _():
        acc_sc[...] += jnp.dot(hidden(slice(None)), w2_ref[...], preferred_element_type=F32)

    @pl.when(j == last)
    def _():
        gate = gate_ref[0]
        a = hidden(slice(None))
        for rows in chunks:
            upd = acc_sc[rows, :] + jnp.dot(a[rows, :], w2_ref[...], preferred_element_type=F32)
            for r in range(0, MLP_ROWS, NORM_ROWS):
                y = x_ref[0, rows.start + r:rows.start + r + NORM_ROWS, :] + gate * upd[r:r + NORM_ROWS, :]
                if final:
                    y = _rms(y, rest[0][...])
                o_ref[0, rows.start + r:rows.start + r + NORM_ROWS, :] = y


def _mlp(x, g, sh, sc, gate, w1, w2, layer, final_g=None):
    b, n, d = x.shape
    tm = min(MLP_TM, n)
    tf = MLP_TF
    vec = pl.BlockSpec((1, 1, d), lambda bi, i, j: (bi, 0, 0))
    row = pl.BlockSpec((1, d), lambda bi, i, j: (0, 0))
    in_specs = [pl.BlockSpec((1, tm, d), lambda bi, i, j: (bi, i, 0)), row, vec, vec, vec,
                pl.BlockSpec((None, d, tf), lambda bi, i, j: (layer, 0, j)),
                pl.BlockSpec((None, tf, d), lambda bi, i, j: (layer, j, 0))]
    args = [x, g.reshape(1, d), sh, sc, gate, w1, w2]
    if final_g is not None:
        in_specs.append(row)
        args.append(final_g.reshape(1, d))
    return pl.pallas_call(
        functools.partial(_mlp_kernel, final=final_g is not None),
        grid=(b, n // tm, D_FF // tf),
        in_specs=in_specs,
        out_specs=pl.BlockSpec((1, tm, d), lambda bi, i, j: (bi, i, 0)),
        out_shape=jax.ShapeDtypeStruct((b, n, d), F32),
        scratch_shapes=[pltpu.VMEM((tm, d), BF16), pltpu.VMEM((tm, d), F32)],
        compiler_params=_params("parallel", "parallel", "arbitrary"),
        name="mlp",
    )(*args)


def _rotate_half_axial(x):
    xr = x.reshape(x.shape[:-1] + (2, 2, ROPE_FREQS))
    return jnp.concatenate([-xr[..., 1:, :], xr[..., :1, :]], axis=-2).reshape(x.shape)


def _rope_tables(n):
    rows = n // GRID_W
    r = jnp.broadcast_to(jnp.arange(rows, dtype=F32)[:, None], (rows, GRID_W)).reshape(n)
    col = jnp.broadcast_to(jnp.arange(GRID_W, dtype=F32)[None, :], (rows, GRID_W)).reshape(n)
    inv = ROPE_THETA ** (-2.0 * jnp.arange(ROPE_FREQS, dtype=F32) / ROPE_AXIS)
    ang = jnp.stack([r[:, None] * inv, col[:, None] * inv], axis=1)
    ang = jnp.broadcast_to(ang[:, :, None, :], (n, 2, 2, ROPE_FREQS)).reshape(n, QK_ROPE)
    zeros = jnp.zeros((n, QK_ROPE), F32)
    return (jnp.concatenate([jnp.cos(ang), zeros], axis=1), jnp.concatenate([jnp.sin(ang), zeros], axis=1))


def _even_weights(w_in, w_uq, w_ukv):
    w_kr = w_in[:, Q_RANK + KV_RANK:Q_RANK + KV_RANK + QK_ROPE]
    w_in2 = jnp.concatenate([w_in[:, Q_RANK + KV_RANK + QK_ROPE:], w_in[:, :Q_RANK],
                             w_in[:, Q_RANK:Q_RANK + KV_RANK], w_kr, _rotate_half_axial(w_kr)], axis=1)
    wq = w_uq.reshape(Q_RANK, N_HEADS, QK_HEAD)
    wq = jnp.concatenate([wq, _rotate_half_axial(wq[..., QK_NOPE:])], axis=-1)
    wq = wq.reshape(Q_RANK, N_HEADS * (QK_HEAD + QK_ROPE))
    wkv = w_ukv.reshape(KV_RANK, N_HEADS, QK_NOPE + V_HEAD)
    wk = wkv[..., :QK_NOPE].reshape(KV_RANK, N_HEADS * QK_NOPE)
    wvt = jnp.transpose(wkv[..., QK_NOPE:], (1, 2, 0)).reshape(N_HEADS * V_HEAD, KV_RANK)
    return w_in2.astype(BF16), wq.astype(BF16), wk.astype(BF16), wvt.astype(BF16)


def kernel(x, c, ctx, c_ctx, w_mod, b_mod, norm1, norm2, w_in, q_norm, w_uq, kv_norm, w_ukv, w_pool,
           pool_scale, w_out_even, w_out_odd, w_mlp1, w_mlp2, final_norm):
    b, n, d = x.shape
    n_ctx = ctx.shape[1]
    assert n == DFT_N1 * DFT_N2 and d == D_MODEL and n % GRID_W == 0
    n_keys = n + n_ctx
    assert n % 512 == 0 and n % n_ctx == 0

    cos2, sin2 = _rope_tables(n)
    dft = _dft_tables(n_ctx)
    w1 = w_mlp1.astype(BF16)
    w2 = w_mlp2.astype(BF16)

    cc = jnp.concatenate([c, c_ctx[None], jnp.zeros((8 - b - 1, d), F32)], axis=0)
    mods = _modulation(cc, w_mod, b_mod)

    updates_ctx = [False] * DEPTH
    for l in reversed(range(DEPTH - 1)):
        updates_ctx[l] = (l + 1) % 2 == 0 or updates_ctx[l + 1]

    for l in range(DEPTH):
        even = l % 2 == 0
        i = l // 2
        mx = [mods[l, :b, k * d:(k + 1) * d].reshape(b, 1, d) for k in range(6)]
        mc = [jnp.broadcast_to(mods[l, b, k * d:(k + 1) * d].reshape(1, 1, d), (b, 1, d)) for k in range(6)]
        ctx_update = updates_ctx[l]

        if even:
            w_in2, wq, wk, wvt = _even_weights(w_in[i], w_uq[i], w_ukv[i])
            w_out = w_out_even[i].astype(BF16)
            wp = w_pool[i].astype(BF16)
            ux, kvx, q = _evenproj(x, norm1[l], mx[0], mx[1], w_in2, q_norm[i], wq, cos2, sin2, rope=True)
            uc, kvc, cq = _evenproj(ctx, norm1[l], mc[0], mc[1], w_in2, q_norm[i], wq, cos2, sin2, rope=False)
            k_all, vt_all, kmax = _kvproj(kvx, kvc, kv_norm[i], wk, wvt, cos2, sin2)
            attn_x = _attention(q, k_all, vt_all, kmax, n_keys, 0)
            pool_x = _pool(ux, wp, pool_scale[i])
            x = _rowmm([attn_x, pool_x], [w_out[:MLA_OUT], w_out[MLA_OUT:]], F32, res=x, gate=mx[2])
            if ctx_update:
                attn_c = _attention(cq, k_all, vt_all, kmax, n_ctx, n // n_ctx)
                pool_c = _pool(uc, wp, pool_scale[i])
                ctx = _rowmm([attn_c, pool_c], [w_out[:MLA_OUT], w_out[MLA_OUT:]], F32, res=ctx, gate=mc[2])
        else:
            w_out = w_out_odd[i].astype(BF16)
            if ctx_update:
                hc = _normmod(ctx, norm1[l], mc[0], mc[1])
            y1 = _dft_stage1(x, norm1[l], mx[0], mx[1], dft['a_kron'])
            fx = _dft_stage2(y1, dft['m'], dft['cc'], dft['sc'])
            x = _rowmm_strided(fx, w_out, x, mx[2])
            if ctx_update:
                fc = _dft_small(hc, dft['m_ctx'], dft['cc'], dft['sc'])
                ctx = _rowmm([fc], [w_out], F32, res=ctx, gate=mc[2])
        x = _mlp(x, norm2[l], mx[3], mx[4], mx[5], w1, w2, l, final_norm if l == DEPTH - 1 else None)
        if ctx_update:
            ctx = _mlp(ctx.reshape(1, b * n_ctx, d), norm2[l], mc[3][:1], mc[4][:1], mc[5][:1],
                       w1, w2, l).reshape(b, n_ctx, d)
    return x
```

```python
import functools
import math

import jax
import jax.numpy as jnp
from jax import lax
from jax.experimental import pallas as pl
from jax.experimental.pallas import tpu as pltpu

D_MODEL = 2048
DEPTH = 4
GRID_W = 64
N_HEADS = 8
Q_RANK = 512
KV_RANK = 256
QK_NOPE = 128
QK_ROPE = 64
V_HEAD = 128
QK_HEAD = QK_NOPE + QK_ROPE
ATTN_SCALE = QK_HEAD ** -0.5
Q_SCALE = ATTN_SCALE * math.log2(math.e)
MLA_OUT = N_HEADS * V_HEAD
ROPE_AXIS = QK_ROPE // 2
ROPE_FREQS = ROPE_AXIS // 2
ROPE_THETA = 10000.0
POOL_WIDTH = D_MODEL // 2
POOL_WINDOWS = (2, 4, 8, 16)
POOL_GROUP = POOL_WIDTH // len(POOL_WINDOWS)
POOL_HALO = 8
FOURIER_GROUPS = 4
FOURIER_GROUP = D_MODEL // FOURIER_GROUPS
D_FF = 4 * D_MODEL
EPS = 1e-6

P_POOL = 0
P_QLAT = POOL_WIDTH
P_KV = POOL_WIDTH + Q_RANK
P_WIDTH = POOL_WIDTH + Q_RANK + KV_RANK + 2 * QK_ROPE

DFT_N1 = 64
DFT_N2 = 128
DFT_ROWS = 16
DFT2_K1 = 8
DFT_STRIDE_BLOCK = 8
DFT_COLS = 512

NORM_ROWS = 16
MLP_TM = 512
MLP_TF = 1024
MLP_ROWS = 256
ATTN_TQ = 1024
ATTN_TK_FIXED = 2048
ATTN_TK_ONLINE = 512
ATTN_BOUND_SLACK = 1.02
ATTN_MIN_DENOM = 2.0 ** -60

V7X_VMEM_LIMIT = 56 * 1024 * 1024

F32 = jnp.float32
BF16 = jnp.bfloat16


def _params(*sem):
    return pltpu.CompilerParams(dimension_semantics=sem, vmem_limit_bytes=V7X_VMEM_LIMIT)


def _resident(shape):
    nd = len(shape)
    return pl.BlockSpec(shape, lambda *_: (0,) * nd, pipeline_mode=pl.Buffered(1))


def _rms(x, g):
    return x * lax.rsqrt(jnp.mean(x * x, axis=-1, keepdims=True) + EPS) * g


def _mod_kernel(c_ref, w_ref, b_ref, o_ref):
    c = c_ref[...]
    s = c * (1.0 / (1.0 + jnp.exp(-c)))
    o_ref[0] = jnp.dot(s, w_ref[0], preferred_element_type=F32, precision=lax.Precision.HIGHEST) + b_ref[0]


def _modulation(cc, w_mod, b_mod):
    rows = cc.shape[0]
    tn = 1536
    n6 = 6 * D_MODEL
    return pl.pallas_call(
        _mod_kernel,
        grid=(DEPTH, n6 // tn),
        in_specs=[
            pl.BlockSpec((rows, D_MODEL), lambda l, j: (0, 0)),
            pl.BlockSpec((1, D_MODEL, tn), lambda l, j: (l, 0, j)),
            pl.BlockSpec((1, 1, tn), lambda l, j: (l, 0, j)),
        ],
        out_specs=pl.BlockSpec((1, rows, tn), lambda l, j: (l, 0, j)),
        out_shape=jax.ShapeDtypeStruct((DEPTH, rows, n6), F32),
        compiler_params=_params("parallel", "parallel"),
        name="modulation",
    )(cc, w_mod, b_mod.reshape(DEPTH, 1, n6))


def _normmod_rows(x_ref, g_ref, sh_ref, sc_ref, dst_ref, start=0, stop=None):
    gs = g_ref[...] * (1.0 + sc_ref[0])
    sh = sh_ref[0]
    for r in range(start, x_ref.shape[1] if stop is None else stop, NORM_ROWS):
        x = x_ref[0, r:r + NORM_ROWS, :]
        inv = lax.rsqrt(jnp.mean(x * x, axis=-1, keepdims=True) + EPS)
        dst_ref[r:r + NORM_ROWS, :] = (x * inv * gs + sh).astype(dst_ref.dtype)


def _normmod_kernel(x_ref, g_ref, sh_ref, sc_ref, o_ref):
    _normmod_rows(x_ref, g_ref, sh_ref, sc_ref, o_ref.at[0])


def _normmod(x, g, sh, sc, out_dtype=BF16):
    b, n, d = x.shape
    tm = min(512, n)
    vec = pl.BlockSpec((1, 1, d), lambda bi, i: (bi, 0, 0))
    return pl.pallas_call(
        _normmod_kernel,
        grid=(b, n // tm),
        in_specs=[pl.BlockSpec((1, tm, d), lambda bi, i: (bi, i, 0)),
                  pl.BlockSpec((1, d), lambda bi, i: (0, 0)), vec, vec],
        out_specs=pl.BlockSpec((1, tm, d), lambda bi, i: (bi, i, 0)),
        out_shape=jax.ShapeDtypeStruct((b, n, d), out_dtype),
        compiler_params=_params("parallel", "parallel"),
        name="normmod",
    )(x, g.reshape(1, d), sh, sc)


def _rope_cols(t, cos2, sin2, rope):
    if not rope:
        return t
    return t * cos2 + pltpu.roll(t, QK_ROPE, 1) * sin2


def _evenproj_kernel(x_ref, g_ref, sh_ref, sc_ref, w_ref, qg_ref, wq_ref, kvg_ref, wk_ref, wvt_ref, c_ref, s_ref,
                     u_ref, q_ref, k_ref, vt_ref, kmax_ref, h_sc, *, rope):
    @pl.when(pl.program_id(1) == 0)
    def _():
        kmax_ref[...] = jnp.zeros(kmax_ref.shape, F32)

    width = QK_NOPE + 2 * QK_ROPE
    for r in range(0, x_ref.shape[1], MLP_ROWS):
        rows = slice(r, r + MLP_ROWS)
        _normmod_rows(x_ref, g_ref, sh_ref, sc_ref, h_sc, r, r + MLP_ROWS)
        p = jnp.dot(h_sc[rows, :], w_ref[...], preferred_element_type=F32)
        u_ref[0, rows, :] = p[:, P_POOL:P_POOL + POOL_WIDTH]
        cos2, sin2 = c_ref[rows, :], s_ref[rows, :]
        z = _rms(p[:, P_QLAT:P_QLAT + Q_RANK], qg_ref[...]).astype(BF16)
        acc = jnp.dot(z, wq_ref[...], preferred_element_type=F32)
        for h in range(N_HEADS):
            q_ref[0, h, rows, :QK_NOPE] = (acc[:, h * width:h * width + QK_NOPE] * Q_SCALE).astype(q_ref.dtype)
            t = _rope_cols(acc[:, h * width + QK_NOPE:(h + 1) * width], cos2, sin2, rope)
            q_ref[0, h, rows, QK_NOPE:] = (t[:, :QK_ROPE] * Q_SCALE).astype(q_ref.dtype)
        zk = _rms(p[:, P_KV:P_KV + KV_RANK], kvg_ref[...]).astype(BF16)
        kr = _rope_cols(p[:, P_KV + KV_RANK:], cos2, sin2, rope)[:, :QK_ROPE].astype(k_ref.dtype)
        kn = jnp.dot(zk, wk_ref[...], preferred_element_type=F32).astype(k_ref.dtype)
        vt = lax.dot_general(wvt_ref[...], zk, (((1,), (1,)), ((), ())),
                             preferred_element_type=F32).astype(vt_ref.dtype)
        kn2 = kn.astype(F32) * kn.astype(F32)
        kr2 = jnp.sum(kr.astype(F32) * kr.astype(F32), axis=1, keepdims=True)
        for h in range(N_HEADS):
            k_ref[0, h, rows, :QK_NOPE] = kn[:, h * QK_NOPE:(h + 1) * QK_NOPE]
            k_ref[0, h, rows, QK_NOPE:] = kr
            vt_ref[0, h, :, rows] = vt[h * V_HEAD:(h + 1) * V_HEAD, :]
            norm2 = jnp.sum(kn2[:, h * QK_NOPE:(h + 1) * QK_NOPE], axis=1, keepdims=True) + kr2
            kmax_ref[0, h] = jnp.maximum(kmax_ref[0, h], jnp.max(norm2, axis=0, keepdims=True))


def _evenproj(x, g, sh, sc, w_in2, q_norm, wq, kv_norm, wk, wvt, cos2, sin2, rope):
    b, n, d = x.shape
    tm = min(512, n)
    vec = pl.BlockSpec((1, 1, d), lambda bi, i: (bi, 0, 0))
    tab = pl.BlockSpec((tm, 2 * QK_ROPE), lambda bi, i: (i, 0))
    heads = pl.BlockSpec((1, N_HEADS, tm, QK_HEAD), lambda bi, i: (bi, 0, i, 0))
    return pl.pallas_call(
        functools.partial(_evenproj_kernel, rope=rope),
        grid=(b, n // tm),
        in_specs=[pl.BlockSpec((1, tm, d), lambda bi, i: (bi, i, 0)),
                  pl.BlockSpec((1, d), lambda bi, i: (0, 0)), vec, vec, _resident(w_in2.shape),
                  pl.BlockSpec((1, Q_RANK), lambda bi, i: (0, 0)), _resident(wq.shape),
                  pl.BlockSpec((1, KV_RANK), lambda bi, i: (0, 0)), _resident(wk.shape), _resident(wvt.shape),
                  tab, tab],
        out_specs=[pl.BlockSpec((1, tm, POOL_WIDTH), lambda bi, i: (bi, i, 0)), heads, heads,
                   pl.BlockSpec((1, N_HEADS, V_HEAD, tm), lambda bi, i: (bi, 0, 0, i)),
                   pl.BlockSpec((1, N_HEADS, 8, 128), lambda bi, i: (bi, 0, 0, 0))],
        out_shape=[jax.ShapeDtypeStruct((b, n, POOL_WIDTH), F32),
                   jax.ShapeDtypeStruct((b, N_HEADS, n, QK_HEAD), BF16),
                   jax.ShapeDtypeStruct((b, N_HEADS, n, QK_HEAD), BF16),
                   jax.ShapeDtypeStruct((b, N_HEADS, V_HEAD, n), BF16),
                   jax.ShapeDtypeStruct((b, N_HEADS, 8, 128), F32)],
        scratch_shapes=[pltpu.VMEM((tm, d), BF16)],
        compiler_params=_params("parallel", "arbitrary"),
        name="evenproj",
    )(x, g.reshape(1, d), sh, sc, w_in2, q_norm.reshape(1, Q_RANK), wq, kv_norm.reshape(1, KV_RANK), wk, wvt,
      cos2, sin2)


def _rowmm_kernel(*refs, n_in, has_res):
    a_refs, w_refs = refs[:n_in], refs[n_in:2 * n_in]
    o_ref = refs[-1]
    acc = None
    for a, w in zip(a_refs, w_refs):
        d = jnp.dot(a[0], w[...], preferred_element_type=F32)
        acc = d if acc is None else acc + d
    if has_res:
        res_ref, gate_ref = refs[2 * n_in], refs[2 * n_in + 1]
        acc = res_ref[0] + gate_ref[0] * acc
    o_ref[0] = acc.astype(o_ref.dtype)


def _rowmm(a_list, w_list, out_dtype, res=None, gate=None, tm=512):
    b, n, _ = a_list[0].shape
    nout = w_list[0].shape[1]
    tm = min(tm, n)
    in_specs = [pl.BlockSpec((1, tm, a.shape[2]), lambda bi, i: (bi, i, 0)) for a in a_list]
    in_specs += [_resident(w.shape) for w in w_list]
    args = list(a_list) + list(w_list)
    if res is not None:
        in_specs += [pl.BlockSpec((1, tm, nout), lambda bi, i: (bi, i, 0)),
                     pl.BlockSpec((1, 1, nout), lambda bi, i: (bi, 0, 0))]
        args += [res, gate]
    return pl.pallas_call(
        functools.partial(_rowmm_kernel, n_in=len(a_list), has_res=res is not None),
        grid=(b, n // tm),
        in_specs=in_specs,
        out_specs=pl.BlockSpec((1, tm, nout), lambda bi, i: (bi, i, 0)),
        out_shape=jax.ShapeDtypeStruct((b, n, nout), out_dtype),
        compiler_params=_params("parallel", "parallel"),
        name="rowmm",
    )(*args)


def _rowmm_strided_kernel(f_ref, w_ref, res_ref, gate_ref, o_ref):
    gate = gate_ref[0]
    d = w_ref.shape[0]
    n2 = f_ref.shape[1]
    steps = res_ref.shape[2]
    rows = jnp.concatenate([f_ref[0, :, s * d:(s + 1) * d] for s in range(steps)], axis=0)
    for c in range(0, d, DFT_COLS):
        cols = slice(c, c + DFT_COLS)
        y = jnp.dot(rows, w_ref[:, cols], preferred_element_type=F32)
        for s in range(steps):
            o_ref[0, :, s, cols] = res_ref[0, :, s, cols] + gate[:, cols] * y[s * n2:(s + 1) * n2, :]


def _rowmm_strided(f, w, res, gate):
    b, n, d = res.shape
    res4 = res.reshape(b, DFT_N2, DFT_N1, d)
    blk = pl.BlockSpec((1, DFT_N2, DFT_STRIDE_BLOCK, d), lambda bi, i: (bi, 0, i, 0))
    out = pl.pallas_call(
        _rowmm_strided_kernel,
        grid=(b, DFT_N1 // DFT_STRIDE_BLOCK),
        in_specs=[pl.BlockSpec((1, DFT_N2, DFT_STRIDE_BLOCK * d), lambda bi, i: (bi, 0, i)),
                  _resident(w.shape), blk,
                  pl.BlockSpec((1, 1, d), lambda bi, i: (bi, 0, 0))],
        out_specs=blk,
        out_shape=jax.ShapeDtypeStruct(res4.shape, F32),
        compiler_params=_params("parallel", "parallel"),
        name="rowmm_strided",
    )(f, w, res4, gate)
    return out.reshape(b, n, d)


def _attn_kernel(q_ref, *refs, n_sources):
    o_ref = refs[-1]
    sources = [refs[3 * s:3 * s + 3] for s in range(n_sources)]
    q = q_ref[0, 0]
    tq = q.shape[0]
    nt = (((1,), (1,)), ((), ()))

    def key_blocks(tk):
        blocks = []
        for k_ref, vt_ref, _ in sources:
            n_keys = k_ref.shape[2]
            size = min(tk, n_keys)
            assert n_keys % size == 0
            blocks += [(k_ref, vt_ref, s, size) for s in range(0, n_keys, size)]
        return blocks

    def scores(blk):
        k_ref, _, s, size = blk
        return lax.dot_general(k_ref[0, 0, s:s + size, :], q, nt, preferred_element_type=F32)

    def weighted_values(blk, p):
        _, vt_ref, s, size = blk
        return jnp.dot(vt_ref[0, 0, :, s:s + size], p.astype(BF16), preferred_element_type=F32)

    def store(acc, l):
        o_ref[0] = jnp.transpose(acc / l).astype(o_ref.dtype)

    qf = q.astype(F32)
    qn2 = lax.dot_general(jnp.ones((8, q.shape[1]), BF16), (qf * qf).astype(BF16), nt,
                          preferred_element_type=F32)[:1]
    kmax = functools.reduce(jnp.maximum, [kmax_ref[0, 0, :1, :1] for _, _, kmax_ref in sources])
    ref = jnp.sqrt(qn2 * kmax) * ATTN_BOUND_SLACK
    l = jnp.zeros((1, tq), F32)
    acc = jnp.zeros((V_HEAD, tq), F32)
    blocks = key_blocks(ATTN_TK_FIXED)
    st_next = scores(blocks[0])
    for j, blk in enumerate(blocks):
        st = st_next
        if j + 1 < len(blocks):
            st_next = scores(blocks[j + 1])
        p = jnp.exp2(st - ref)
        l = l + jnp.sum(p, axis=0, keepdims=True)
        acc = acc + weighted_values(blk, p)
    store(acc, l)

    @pl.when(jnp.logical_not(jnp.min(l) >= ATTN_MIN_DENOM))
    def _():
        blocks = key_blocks(ATTN_TK_ONLINE)
        m = jnp.full((1, tq), -jnp.inf, F32)
        l = jnp.zeros((1, tq), F32)
        acc = jnp.zeros((V_HEAD, tq), F32)
        st_next = scores(blocks[0])
        for j, blk in enumerate(blocks):
            st = st_next
            if j + 1 < len(blocks):
                st_next = scores(blocks[j + 1])
            m_new = jnp.maximum(m, jnp.max(st, axis=0, keepdims=True))
            alpha = jnp.exp2(m - m_new)
            p = jnp.exp2(st - m_new)
            l = alpha * l + jnp.sum(p, axis=0, keepdims=True)
            acc = alpha * acc + weighted_values(blk, p)
            m = m_new
        store(acc, l)


def _attention(q, sources):
    b, h, n, _ = q.shape
    tq = min(ATTN_TQ, n)
    in_specs = [pl.BlockSpec((1, 1, tq, QK_HEAD), lambda bi, hi, i: (bi, hi, i, 0))]
    args = [q]
    for k, vt, kmax in sources:
        m = k.shape[2]
        in_specs += [pl.BlockSpec((1, 1, m, QK_HEAD), lambda bi, hi, i: (bi, hi, 0, 0)),
                     pl.BlockSpec((1, 1, V_HEAD, m), lambda bi, hi, i: (bi, hi, 0, 0)),
                     pl.BlockSpec((1, 1, 8, 128), lambda bi, hi, i: (bi, hi, 0, 0))]
        args += [k, vt, kmax]
    return pl.pallas_call(
        functools.partial(_attn_kernel, n_sources=len(sources)),
        grid=(b, h, n // tq),
        in_specs=in_specs,
        out_specs=pl.BlockSpec((1, tq, V_HEAD), lambda bi, hi, i: (bi, i, hi)),
        out_shape=jax.ShapeDtypeStruct((b, n, h * V_HEAD), BF16),
        compiler_params=_params("parallel", "parallel", "parallel"),
        name="attention",
    )(*args)


def _pool_kernel(u_ref, prev_ref, next_ref, w_ref, s_ref, o_ref, ext_sc, *, n, tm):
    i = pl.program_id(1)
    keep_prev = jnp.where(i > 0, 1.0, 0.0)
    keep_next = jnp.where(i < pl.num_programs(1) - 1, 1.0, 0.0)
    ext_sc[0:POOL_HALO] = prev_ref[0] * keep_prev
    ext_sc[POOL_HALO:POOL_HALO + tm] = u_ref[0]
    ext_sc[POOL_HALO + tm:] = next_ref[0] * keep_next
    t = i * tm + lax.broadcasted_iota(jnp.int32, (tm, 1), 0)
    for g, win in enumerate(POOL_WINDOWS):
        half = win // 2
        cols = slice(g * POOL_GROUP, (g + 1) * POOL_GROUP)
        tot = ext_sc[POOL_HALO - half:POOL_HALO - half + tm, cols]
        for d in range(1 - half, half):
            tot = tot + ext_sc[POOL_HALO + d:POOL_HALO + d + tm, cols]
        cnt = (jnp.minimum(t + half, n) - jnp.maximum(t - half, 0)).astype(F32)
        pooled = (tot / cnt - ext_sc[POOL_HALO:POOL_HALO + tm, cols]).astype(BF16)
        y = jnp.dot(pooled, w_ref[g], preferred_element_type=F32) * s_ref[:, cols]
        o_ref[0, :, cols] = y.astype(o_ref.dtype)


def _pool(p, w_pool, pool_scale):
    b, n, _ = p.shape
    tm = min(512, n)
    hb = tm // POOL_HALO
    last_hb = n // POOL_HALO - 1
    return pl.pallas_call(
        functools.partial(_pool_kernel, n=n, tm=tm),
        grid=(b, n // tm),
        in_specs=[pl.BlockSpec((1, tm, POOL_WIDTH), lambda bi, i: (bi, i, 0)),
                  pl.BlockSpec((1, POOL_HALO, POOL_WIDTH), lambda bi, i: (bi, jnp.maximum(i * hb - 1, 0), 0)),
                  pl.BlockSpec((1, POOL_HALO, POOL_WIDTH),
                               lambda bi, i: (bi, jnp.minimum((i + 1) * hb, last_hb), 0)),
                  _resident(w_pool.shape),
                  pl.BlockSpec((1, POOL_WIDTH), lambda bi, i: (0, 0))],
        out_specs=pl.BlockSpec((1, tm, POOL_WIDTH), lambda bi, i: (bi, i, 0)),
        out_shape=jax.ShapeDtypeStruct((b, n, POOL_WIDTH), BF16),
        scratch_shapes=[pltpu.VMEM((tm + 2 * POOL_HALO, POOL_WIDTH), F32)],
        compiler_params=_params("parallel", "parallel"),
        name="pool",
    )(p, p, p, w_pool, pool_scale.reshape(1, POOL_WIDTH))


def _dft1_kernel(x_ref, g_ref, sh_ref, sc_ref, a_ref, o_ref, h_sc):
    gs = g_ref[...] * (1.0 + sc_ref[0])
    sh = sh_ref[0]
    for t1 in range(DFT_N1):
        x = x_ref[0, t1]
        inv = lax.rsqrt(jnp.mean(x * x, axis=-1, keepdims=True) + EPS)
        h_sc[t1 * DFT_ROWS:(t1 + 1) * DFT_ROWS, :] = (x * inv * gs + sh).astype(h_sc.dtype)
    for c in range(0, D_MODEL, DFT_COLS):
        y = jnp.dot(a_ref[...], h_sc[:, c:c + DFT_COLS], preferred_element_type=F32)
        o_ref[0, :, :, :, c:c + DFT_COLS] = y.astype(o_ref.dtype).reshape(2, DFT_N1, DFT_ROWS, DFT_COLS)


def _dft_stage1(x, g, sh, sc, a_kron):
    b, n, d = x.shape
    vec = pl.BlockSpec((1, 1, d), lambda bi, r: (bi, 0, 0))
    return pl.pallas_call(
        _dft1_kernel,
        grid=(b, DFT_N2 // DFT_ROWS),
        in_specs=[pl.BlockSpec((1, DFT_N1, DFT_ROWS, d), lambda bi, r: (bi, 0, r, 0)),
                  pl.BlockSpec((1, d), lambda bi, r: (0, 0)), vec, vec,
                  _resident(a_kron.shape)],
        out_specs=pl.BlockSpec((1, 2, DFT_N1, DFT_ROWS, d), lambda bi, r: (bi, 0, 0, r, 0)),
        out_shape=jax.ShapeDtypeStruct((b, 2, DFT_N1, DFT_N2, d), BF16),
        scratch_shapes=[pltpu.VMEM((DFT_N1 * DFT_ROWS, d), BF16)],
        compiler_params=_params("parallel", "parallel"),
        name="dft_stage1",
    )(x.reshape(b, DFT_N1, DFT_N2, d), g.reshape(1, d), sh, sc, a_kron)


def _dft2_rows(y, m, cc_ref, sc_ref, rows_out, seq_len, store):
    pq = jnp.dot(m, y, preferred_element_type=F32).astype(BF16)
    ortho = 1.0 / math.sqrt(seq_len * FOURIER_GROUP)
    for g in range(FOURIER_GROUPS):
        cols = slice(g * FOURIER_GROUP, (g + 1) * FOURIER_GROUP)
        f = (jnp.dot(pq[:rows_out, cols], cc_ref[...], preferred_element_type=F32)
             - jnp.dot(pq[rows_out:, cols], sc_ref[...], preferred_element_type=F32))
        store(cols, f * ortho)


def _dft2_kernel(y_ref, m_ref, cc_ref, sc_ref, o_ref):
    for s in range(y_ref.shape[2]):
        y = jnp.concatenate([y_ref[0, 0, s], y_ref[0, 1, s]], axis=0)

        def store(cols, f, s=s):
            o_ref[0, :, s * D_MODEL + cols.start:s * D_MODEL + cols.stop] = f.astype(o_ref.dtype)

        _dft2_rows(y, m_ref[s], cc_ref, sc_ref, DFT_N2, DFT_N1 * DFT_N2, store)


def _dft_small_kernel(h_ref, m_ref, cc_ref, sc_ref, o_ref):
    def store(cols, f):
        o_ref[0, :, cols] = f.astype(o_ref.dtype)

    n = h_ref.shape[1]
    _dft2_rows(h_ref[0], m_ref[0], cc_ref, sc_ref, n, n, store)


def _dft_stage2(y, m, cc, sc):
    b = y.shape[0]
    return pl.pallas_call(
        _dft2_kernel,
        grid=(b, DFT_N1 // DFT2_K1),
        in_specs=[pl.BlockSpec((1, 2, DFT2_K1, DFT_N2, D_MODEL), lambda bi, k: (bi, 0, k, 0, 0)),
                  pl.BlockSpec((DFT2_K1, 2 * DFT_N2, 2 * DFT_N2), lambda bi, k: (k, 0, 0)),
                  _resident(cc.shape), _resident(sc.shape)],
        out_specs=pl.BlockSpec((1, DFT_N2, DFT2_K1 * D_MODEL), lambda bi, k: (bi, 0, k)),
        out_shape=jax.ShapeDtypeStruct((b, DFT_N2, DFT_N1 * D_MODEL), BF16),
        compiler_params=_params("parallel", "parallel"),
        name="dft_stage2",
    )(y, m, cc, sc)


def _dft_small(h, m, cc, sc):
    b, n, d = h.shape
    return pl.pallas_call(
        _dft_small_kernel,
        grid=(b,),
        in_specs=[pl.BlockSpec((1, n, d), lambda bi: (bi, 0, 0)),
                  _resident(m.shape), _resident(cc.shape), _resident(sc.shape)],
        out_specs=pl.BlockSpec((1, n, d), lambda bi: (bi, 0, 0)),
        out_shape=jax.ShapeDtypeStruct((b, n, d), BF16),
        compiler_params=_params("parallel"),
        name="dft_small",
    )(h, m, cc, sc)


def _cos_sin(num, den):
    ang = (num % den).astype(F32) * (2.0 * math.pi / den)
    return jnp.cos(ang), jnp.sin(ang)


def _dft_tables(n_ctx):
    i1 = jnp.arange(DFT_N1, dtype=jnp.int32)
    c1, s1 = _cos_sin(i1[:, None] * i1[None, :], DFT_N1)
    a = jnp.concatenate([c1, -s1], axis=0)
    eye = jnp.eye(DFT_ROWS, dtype=F32)
    a_kron = jnp.einsum('pt,jk->pjtk', a, eye).reshape(2 * DFT_N1 * DFT_ROWS, DFT_N1 * DFT_ROWS)
    n = DFT_N1 * DFT_N2
    t2 = jnp.arange(DFT_N2, dtype=jnp.int32)
    k = i1[:, None, None] + DFT_N1 * t2[None, :, None]
    ck, sk = _cos_sin(k * t2[None, None, :], n)
    m = jnp.concatenate([jnp.concatenate([ck, sk], axis=2),
                         jnp.concatenate([sk, -ck], axis=2)], axis=1)
    ic = jnp.arange(FOURIER_GROUP, dtype=jnp.int32)
    cc, sc = _cos_sin(ic[:, None] * ic[None, :], FOURIER_GROUP)
    il = jnp.arange(n_ctx, dtype=jnp.int32)
    cl, sl = _cos_sin(il[:, None] * il[None, :], n_ctx)
    m_ctx = jnp.concatenate([cl, sl], axis=0)[None]

    return dict(a_kron=a_kron.astype(BF16), m=m.astype(BF16), m_ctx=m_ctx.astype(BF16),
                cc=cc.astype(BF16), sc=sc.astype(BF16))


def _mlp_kernel(x_ref, g_ref, sh_ref, sc_ref, gate_ref, w1_ref, w2_ref, *rest, final):
    o_ref, h_sc, acc_sc = rest[-3:]
    j = pl.program_id(2)
    last = pl.num_programs(2) - 1
    tm = x_ref.shape[1]
    chunks = [slice(r, r + MLP_ROWS) for r in range(0, tm, MLP_ROWS)]

    def hidden(rows):
        a = jnp.maximum(jnp.dot(h_sc[rows, :], w1_ref[...], preferred_element_type=F32), 0.0)
        return (a * a).astype(BF16)

    @pl.when(j == 0)
    def _():
        parts = []
        for rows in chunks:
            _normmod_rows(x_ref, g_ref, sh_ref, sc_ref, h_sc, rows.start, rows.stop)
            parts.append(hidden(rows))
        acc_sc[...] = jnp.dot(jnp.concatenate(parts, axis=0), w2_ref[...], preferred_element_type=F32)

    @pl.when(jnp.logical_and(j > 0, j < last))
    def _():
        acc_sc[...] += jnp.dot(hidden(slice(None)), w2_ref[...], preferred_element_type=F32)

    @pl.when(j == last)
    def _():
        gate = gate_ref[0]
        a = hidden(slice(None))
        for rows in chunks:
            upd = acc_sc[rows, :] + jnp.dot(a[rows, :], w2_ref[...], preferred_element_type=F32)
            for r in range(0, MLP_ROWS, NORM_ROWS):
                y = x_ref[0, rows.start + r:rows.start + r + NORM_ROWS, :] + gate * upd[r:r + NORM_ROWS, :]
                if final:
                    y = _rms(y, rest[0][...])
                o_ref[0, rows.start + r:rows.start + r + NORM_ROWS, :] = y


def _mlp(x, g, sh, sc, gate, w1, w2, layer, final_g=None):
    b, n, d = x.shape
    tm = min(MLP_TM, n)
    tf = MLP_TF
    vec = pl.BlockSpec((1, 1, d), lambda bi, i, j: (bi, 0, 0))
    row = pl.BlockSpec((1, d), lambda bi, i, j: (0, 0))
    in_specs = [pl.BlockSpec((1, tm, d), lambda bi, i, j: (bi, i, 0)), row, vec, vec, vec,
                pl.BlockSpec((None, d, tf), lambda bi, i, j: (layer, 0, j)),
                pl.BlockSpec((None, tf, d), lambda bi, i, j: (layer, j, 0))]
    args = [x, g.reshape(1, d), sh, sc, gate, w1, w2]
    if final_g is not None:
        in_specs.append(row)
        args.append(final_g.reshape(1, d))
    return pl.pallas_call(
        functools.partial(_mlp_kernel, final=final_g is not None),
        grid=(b, n // tm, D_FF // tf),
        in_specs=in_specs,
        out_specs=pl.BlockSpec((1, tm, d), lambda bi, i, j: (bi, i, 0)),
        out_shape=jax.ShapeDtypeStruct((b, n, d), F32),
        scratch_shapes=[pltpu.VMEM((tm, d), BF16), pltpu.VMEM((tm, d), F32)],
        compiler_params=_params("parallel", "parallel", "arbitrary"),
        name="mlp",
    )(*args)


def _rotate_half_axial(x):
    xr = x.reshape(x.shape[:-1] + (2, 2, ROPE_FREQS))
    return jnp.concatenate([-xr[..., 1:, :], xr[..., :1, :]], axis=-2).reshape(x.shape)


def _rope_tables(n):
    rows = n // GRID_W
    r = jnp.broadcast_to(jnp.arange(rows, dtype=F32)[:, None], (rows, GRID_W)).reshape(n)
    col = jnp.broadcast_to(jnp.arange(GRID_W, dtype=F32)[None, :], (rows, GRID_W)).reshape(n)
    inv = ROPE_THETA ** (-2.0 * jnp.arange(ROPE_FREQS, dtype=F32) / ROPE_AXIS)
    ang = jnp.stack([r[:, None] * inv, col[:, None] * inv], axis=1)
    ang = jnp.broadcast_to(ang[:, :, None, :], (n, 2, 2, ROPE_FREQS)).reshape(n, QK_ROPE)
    zeros = jnp.zeros((n, QK_ROPE), F32)
    return (jnp.concatenate([jnp.cos(ang), zeros], axis=1), jnp.concatenate([jnp.sin(ang), zeros], axis=1))


def _even_weights(w_in, w_uq, w_ukv):
    w_kr = w_in[:, Q_RANK + KV_RANK:Q_RANK + KV_RANK + QK_ROPE]
    w_in2 = jnp.concatenate([w_in[:, Q_RANK + KV_RANK + QK_ROPE:], w_in[:, :Q_RANK],
                             w_in[:, Q_RANK:Q_RANK + KV_RANK], w_kr, _rotate_half_axial(w_kr)], axis=1)
    wq = w_uq.reshape(Q_RANK, N_HEADS, QK_HEAD)
    wq = jnp.concatenate([wq, _rotate_half_axial(wq[..., QK_NOPE:])], axis=-1)
    wq = wq.reshape(Q_RANK, N_HEADS * (QK_HEAD + QK_ROPE))
    wkv = w_ukv.reshape(KV_RANK, N_HEADS, QK_NOPE + V_HEAD)
    wk = wkv[..., :QK_NOPE].reshape(KV_RANK, N_HEADS * QK_NOPE)
    wvt = jnp.transpose(wkv[..., QK_NOPE:], (1, 2, 0)).reshape(N_HEADS * V_HEAD, KV_RANK)
    return w_in2.astype(BF16), wq.astype(BF16), wk.astype(BF16), wvt.astype(BF16)


def kernel(x, c, ctx, c_ctx, w_mod, b_mod, norm1, norm2, w_in, q_norm, w_uq, kv_norm, w_ukv, w_pool,
           pool_scale, w_out_even, w_out_odd, w_mlp1, w_mlp2, final_norm):
    b, n, d = x.shape
    n_ctx = ctx.shape[1]
    assert n == DFT_N1 * DFT_N2 and d == D_MODEL and n % GRID_W == 0
    assert n % 512 == 0 and n % n_ctx == 0

    cos2, sin2 = _rope_tables(n)
    dft = _dft_tables(n_ctx)
    w1 = w_mlp1.astype(BF16)
    w2 = w_mlp2.astype(BF16)

    cc = jnp.concatenate([c, c_ctx[None], jnp.zeros((8 - b - 1, d), F32)], axis=0)
    mods = _modulation(cc, w_mod, b_mod)

    updates_ctx = [False] * DEPTH
    for l in reversed(range(DEPTH - 1)):
        updates_ctx[l] = (l + 1) % 2 == 0 or updates_ctx[l + 1]

    for l in range(DEPTH):
        even = l % 2 == 0
        i = l // 2
        mx = [mods[l, :b, k * d:(k + 1) * d].reshape(b, 1, d) for k in range(6)]
        mc = [jnp.broadcast_to(mods[l, b, k * d:(k + 1) * d].reshape(1, 1, d), (b, 1, d)) for k in range(6)]
        ctx_update = updates_ctx[l]

        if even:
            w_in2, wq, wk, wvt = _even_weights(w_in[i], w_uq[i], w_ukv[i])
            w_out = w_out_even[i].astype(BF16)
            wp = w_pool[i].astype(BF16)
            ux, q, *keys_x = _evenproj(x, norm1[l], mx[0], mx[1], w_in2, q_norm[i], wq, kv_norm[i], wk, wvt,
                                       cos2, sin2, rope=True)
            uc, cq, *keys_c = _evenproj(ctx, norm1[l], mc[0], mc[1], w_in2, q_norm[i], wq, kv_norm[i], wk, wvt,
                                        cos2, sin2, rope=False)
            attn_x = _attention(q, [keys_x, keys_c])
            pool_x = _pool(ux, wp, pool_scale[i])
            x = _rowmm([attn_x, pool_x], [w_out[:MLA_OUT], w_out[MLA_OUT:]], F32, res=x, gate=mx[2])
            if ctx_update:
                attn_c = _attention(cq, [keys_c])
                pool_c = _pool(uc, wp, pool_scale[i])
                ctx = _rowmm([attn_c, pool_c], [w_out[:MLA_OUT], w_out[MLA_OUT:]], F32, res=ctx, gate=mc[2])
        else:
            w_out = w_out_odd[i].astype(BF16)
            if ctx_update:
                hc = _normmod(ctx, norm1[l], mc[0], mc[1])
            y1 = _dft_stage1(x, norm1[l], mx[0], mx[1], dft['a_kron'])
            fx = _dft_stage2(y1, dft['m'], dft['cc'], dft['sc'])
            x = _rowmm_strided(fx, w_out, x, mx[2])
            if ctx_update:
                fc = _dft_small(hc, dft['m_ctx'], dft['cc'], dft['sc'])
                ctx = _rowmm([fc], [w_out], F32, res=ctx, gate=mc[2])
        x = _mlp(x, norm2[l], mx[3], mx[4], mx[5], w1, w2, l, final_norm if l == DEPTH - 1 else None)
        if ctx_update:
            ctx = _mlp(ctx.reshape(1, b * n_ctx, d), norm2[l], mc[3][:1], mc[4][:1], mc[5][:1],
                       w1, w2, l).reshape(b, n_ctx, d)
    return x
```

```python
import functools
import math

import jax
import jax.numpy as jnp
from jax import lax
from jax.experimental import pallas as pl
from jax.experimental.pallas import tpu as pltpu

D_MODEL = 2048
DEPTH = 4
GRID_W = 64
N_HEADS = 8
Q_RANK = 512
KV_RANK = 256
QK_NOPE = 128
QK_ROPE = 64
V_HEAD = 128
QK_HEAD = QK_NOPE + QK_ROPE
ATTN_SCALE = QK_HEAD ** -0.5
Q_SCALE = ATTN_SCALE * math.log2(math.e)
MLA_OUT = N_HEADS * V_HEAD
ROPE_AXIS = QK_ROPE // 2
ROPE_FREQS = ROPE_AXIS // 2
ROPE_THETA = 10000.0
POOL_WIDTH = D_MODEL // 2
POOL_WINDOWS = (2, 4, 8, 16)
POOL_GROUP = POOL_WIDTH // len(POOL_WINDOWS)
POOL_HALO = 8
FOURIER_GROUPS = 4
FOURIER_GROUP = D_MODEL // FOURIER_GROUPS
D_FF = 4 * D_MODEL
EPS = 1e-6

P_POOL = 0
P_QLAT = POOL_WIDTH
P_KV = POOL_WIDTH + Q_RANK
P_WIDTH = POOL_WIDTH + Q_RANK + KV_RANK + 2 * QK_ROPE

DFT_N1 = 64
DFT_N2 = 128
DFT_ROWS = 16
DFT2_K1 = 8
DFT_STRIDE_BLOCK = 8
DFT_COLS = 512

NORM_ROWS = 16
MLP_TM = 512
MLP_TF = 1024
MLP_ROWS = 256
ATTN_TQ = 1024
ATTN_TK_FIXED = 4096
ATTN_TK_ONLINE = 256
ATTN_BOUND_SLACK = 1.02
ATTN_MIN_DENOM = 2.0 ** -60

V7X_VMEM_LIMIT = 56 * 1024 * 1024

F32 = jnp.float32
BF16 = jnp.bfloat16


def _params(*sem):
    return pltpu.CompilerParams(dimension_semantics=sem, vmem_limit_bytes=V7X_VMEM_LIMIT)


def _resident(shape):
    nd = len(shape)
    return pl.BlockSpec(shape, lambda *_: (0,) * nd, pipeline_mode=pl.Buffered(1))


def _rms(x, g):
    return x * lax.rsqrt(jnp.mean(x * x, axis=-1, keepdims=True) + EPS) * g


def _mod_kernel(c_ref, w_ref, b_ref, o_ref):
    c = c_ref[...]
    s = c * (1.0 / (1.0 + jnp.exp(-c)))
    o_ref[0] = jnp.dot(s, w_ref[0], preferred_element_type=F32, precision=lax.Precision.HIGHEST) + b_ref[0]


def _modulation(cc, w_mod, b_mod):
    rows = cc.shape[0]
    tn = 1536
    n6 = 6 * D_MODEL
    return pl.pallas_call(
        _mod_kernel,
        grid=(DEPTH, n6 // tn),
        in_specs=[
            pl.BlockSpec((rows, D_MODEL), lambda l, j: (0, 0)),
            pl.BlockSpec((1, D_MODEL, tn), lambda l, j: (l, 0, j)),
            pl.BlockSpec((1, 1, tn), lambda l, j: (l, 0, j)),
        ],
        out_specs=pl.BlockSpec((1, rows, tn), lambda l, j: (l, 0, j)),
        out_shape=jax.ShapeDtypeStruct((DEPTH, rows, n6), F32),
        compiler_params=_params("parallel", "parallel"),
        name="modulation",
    )(cc, w_mod, b_mod.reshape(DEPTH, 1, n6))


def _normmod_rows(x_ref, g_ref, sh_ref, sc_ref, dst_ref, start=0, stop=None):
    gs = g_ref[...] * (1.0 + sc_ref[0])
    sh = sh_ref[0]
    for r in range(start, x_ref.shape[1] if stop is None else stop, NORM_ROWS):
        x = x_ref[0, r:r + NORM_ROWS, :]
        inv = lax.rsqrt(jnp.mean(x * x, axis=-1, keepdims=True) + EPS)
        dst_ref[r:r + NORM_ROWS, :] = (x * inv * gs + sh).astype(dst_ref.dtype)


def _normmod_kernel(x_ref, g_ref, sh_ref, sc_ref, o_ref):
    _normmod_rows(x_ref, g_ref, sh_ref, sc_ref, o_ref.at[0])


def _normmod(x, g, sh, sc, out_dtype=BF16):
    b, n, d = x.shape
    tm = min(512, n)
    vec = pl.BlockSpec((1, 1, d), lambda bi, i: (bi, 0, 0))
    return pl.pallas_call(
        _normmod_kernel,
        grid=(b, n // tm),
        in_specs=[pl.BlockSpec((1, tm, d), lambda bi, i: (bi, i, 0)),
                  pl.BlockSpec((1, d), lambda bi, i: (0, 0)), vec, vec],
        out_specs=pl.BlockSpec((1, tm, d), lambda bi, i: (bi, i, 0)),
        out_shape=jax.ShapeDtypeStruct((b, n, d), out_dtype),
        compiler_params=_params("parallel", "parallel"),
        name="normmod",
    )(x, g.reshape(1, d), sh, sc)


def _rope_cols(t, cos2, sin2, rope):
    if not rope:
        return t
    return t * cos2 + pltpu.roll(t, QK_ROPE, 1) * sin2


def _evenproj_kernel(x_ref, g_ref, sh_ref, sc_ref, w_ref, qg_ref, wq_ref, kvg_ref, wk_ref, wvt_ref, c_ref, s_ref,
                     u_ref, q_ref, k_ref, vt_ref, kmax_ref, h_sc, *, rope):
    @pl.when(pl.program_id(1) == 0)
    def _():
        kmax_ref[...] = jnp.zeros(kmax_ref.shape, F32)

    width = QK_NOPE + 2 * QK_ROPE
    for r in range(0, x_ref.shape[1], MLP_ROWS):
        rows = slice(r, r + MLP_ROWS)
        _normmod_rows(x_ref, g_ref, sh_ref, sc_ref, h_sc, r, r + MLP_ROWS)
        p = jnp.dot(h_sc[rows, :], w_ref[...], preferred_element_type=F32)
        u_ref[0, rows, :] = p[:, P_POOL:P_POOL + POOL_WIDTH]
        cos2, sin2 = c_ref[rows, :], s_ref[rows, :]
        z = _rms(p[:, P_QLAT:P_QLAT + Q_RANK], qg_ref[...]).astype(BF16)
        acc = jnp.dot(z, wq_ref[...], preferred_element_type=F32)
        for h in range(N_HEADS):
            q_ref[0, h, rows, :QK_NOPE] = (acc[:, h * width:h * width + QK_NOPE] * Q_SCALE).astype(q_ref.dtype)
            t = _rope_cols(acc[:, h * width + QK_NOPE:(h + 1) * width], cos2, sin2, rope)
            q_ref[0, h, rows, QK_NOPE:] = (t[:, :QK_ROPE] * Q_SCALE).astype(q_ref.dtype)
        zk = _rms(p[:, P_KV:P_KV + KV_RANK], kvg_ref[...]).astype(BF16)
        kr = _rope_cols(p[:, P_KV + KV_RANK:], cos2, sin2, rope)[:, :QK_ROPE].astype(k_ref.dtype)
        kn = jnp.dot(zk, wk_ref[...], preferred_element_type=F32).astype(k_ref.dtype)
        vt = lax.dot_general(wvt_ref[...], zk, (((1,), (1,)), ((), ())),
                             preferred_element_type=F32).astype(vt_ref.dtype)
        kn2 = kn.astype(F32) * kn.astype(F32)
        kr2 = jnp.sum(kr.astype(F32) * kr.astype(F32), axis=1, keepdims=True)
        for h in range(N_HEADS):
            k_ref[0, h, rows, :QK_NOPE] = kn[:, h * QK_NOPE:(h + 1) * QK_NOPE]
            k_ref[0, h, rows, QK_NOPE:] = kr
            vt_ref[0, h, :, rows] = vt[h * V_HEAD:(h + 1) * V_HEAD, :]
            norm2 = jnp.sum(kn2[:, h * QK_NOPE:(h + 1) * QK_NOPE], axis=1, keepdims=True) + kr2
            kmax_ref[0, h] = jnp.maximum(kmax_ref[0, h], jnp.max(norm2, axis=0, keepdims=True))


def _evenproj(x, g, sh, sc, w_in2, q_norm, wq, kv_norm, wk, wvt, cos2, sin2, rope):
    b, n, d = x.shape
    tm = min(512, n)
    vec = pl.BlockSpec((1, 1, d), lambda bi, i: (bi, 0, 0))
    tab = pl.BlockSpec((tm, 2 * QK_ROPE), lambda bi, i: (i, 0))
    heads = pl.BlockSpec((1, N_HEADS, tm, QK_HEAD), lambda bi, i: (bi, 0, i, 0))
    return pl.pallas_call(
        functools.partial(_evenproj_kernel, rope=rope),
        grid=(b, n // tm),
        in_specs=[pl.BlockSpec((1, tm, d), lambda bi, i: (bi, i, 0)),
                  pl.BlockSpec((1, d), lambda bi, i: (0, 0)), vec, vec, _resident(w_in2.shape),
                  pl.BlockSpec((1, Q_RANK), lambda bi, i: (0, 0)), _resident(wq.shape),
                  pl.BlockSpec((1, KV_RANK), lambda bi, i: (0, 0)), _resident(wk.shape), _resident(wvt.shape),
                  tab, tab],
        out_specs=[pl.BlockSpec((1, tm, POOL_WIDTH), lambda bi, i: (bi, i, 0)), heads, heads,
                   pl.BlockSpec((1, N_HEADS, V_HEAD, tm), lambda bi, i: (bi, 0, 0, i)),
                   pl.BlockSpec((1, N_HEADS, 8, 128), lambda bi, i: (bi, 0, 0, 0))],
        out_shape=[jax.ShapeDtypeStruct((b, n, POOL_WIDTH), F32),
                   jax.ShapeDtypeStruct((b, N_HEADS, n, QK_HEAD), BF16),
                   jax.ShapeDtypeStruct((b, N_HEADS, n, QK_HEAD), BF16),
                   jax.ShapeDtypeStruct((b, N_HEADS, V_HEAD, n), BF16),
                   jax.ShapeDtypeStruct((b, N_HEADS, 8, 128), F32)],
        scratch_shapes=[pltpu.VMEM((tm, d), BF16)],
        compiler_params=_params("parallel", "arbitrary"),
        name="evenproj",
    )(x, g.reshape(1, d), sh, sc, w_in2, q_norm.reshape(1, Q_RANK), wq, kv_norm.reshape(1, KV_RANK), wk, wvt,
      cos2, sin2)


def _rowmm_kernel(*refs, n_in, has_res):
    a_refs, w_refs = refs[:n_in], refs[n_in:2 * n_in]
    o_ref = refs[-1]
    acc = None
    for a, w in zip(a_refs, w_refs):
        d = jnp.dot(a[0], w[...], preferred_element_type=F32)
        acc = d if acc is None else acc + d
    if has_res:
        res_ref, gate_ref = refs[2 * n_in], refs[2 * n_in + 1]
        acc = res_ref[0] + gate_ref[0] * acc
    o_ref[0] = acc.astype(o_ref.dtype)


def _rowmm(a_list, w_list, out_dtype, res=None, gate=None, tm=512):
    b, n, _ = a_list[0].shape
    nout = w_list[0].shape[1]
    tm = min(tm, n)
    in_specs = [pl.BlockSpec((1, tm, a.shape[2]), lambda bi, i: (bi, i, 0)) for a in a_list]
    in_specs += [_resident(w.shape) for w in w_list]
    args = list(a_list) + list(w_list)
    if res is not None:
        in_specs += [pl.BlockSpec((1, tm, nout), lambda bi, i: (bi, i, 0)),
                     pl.BlockSpec((1, 1, nout), lambda bi, i: (bi, 0, 0))]
        args += [res, gate]
    return pl.pallas_call(
        functools.partial(_rowmm_kernel, n_in=len(a_list), has_res=res is not None),
        grid=(b, n // tm),
        in_specs=in_specs,
        out_specs=pl.BlockSpec((1, tm, nout), lambda bi, i: (bi, i, 0)),
        out_shape=jax.ShapeDtypeStruct((b, n, nout), out_dtype),
        compiler_params=_params("parallel", "parallel"),
        name="rowmm",
    )(*args)


def _rowmm_strided_kernel(f_ref, w_ref, res_ref, gate_ref, o_ref):
    gate = gate_ref[0]
    d = w_ref.shape[0]
    n2 = f_ref.shape[1]
    steps = res_ref.shape[2]
    rows = jnp.concatenate([f_ref[0, :, s * d:(s + 1) * d] for s in range(steps)], axis=0)
    for c in range(0, d, DFT_COLS):
        cols = slice(c, c + DFT_COLS)
        y = jnp.dot(rows, w_ref[:, cols], preferred_element_type=F32)
        for s in range(steps):
            o_ref[0, :, s, cols] = res_ref[0, :, s, cols] + gate[:, cols] * y[s * n2:(s + 1) * n2, :]


def _rowmm_strided(f, w, res, gate):
    b, n, d = res.shape
    res4 = res.reshape(b, DFT_N2, DFT_N1, d)
    blk = pl.BlockSpec((1, DFT_N2, DFT_STRIDE_BLOCK, d), lambda bi, i: (bi, 0, i, 0))
    out = pl.pallas_call(
        _rowmm_strided_kernel,
        grid=(b, DFT_N1 // DFT_STRIDE_BLOCK),
        in_specs=[pl.BlockSpec((1, DFT_N2, DFT_STRIDE_BLOCK * d), lambda bi, i: (bi, 0, i)),
                  _resident(w.shape), blk,
                  pl.BlockSpec((1, 1, d), lambda bi, i: (bi, 0, 0))],
        out_specs=blk,
        out_shape=jax.ShapeDtypeStruct(res4.shape, F32),
        compiler_params=_params("parallel", "parallel"),
        name="rowmm_strided",
    )(f, w, res4, gate)
    return out.reshape(b, n, d)


def _attn_kernel(q_ref, *refs, n_sources):
    o_ref = refs[-1]
    sources = [refs[3 * s:3 * s + 3] for s in range(n_sources)]
    q = q_ref[0, 0]
    tq = q.shape[0]
    nt = (((1,), (1,)), ((), ()))

    def key_blocks(tk):
        blocks = []
        for k_ref, vt_ref, _ in sources:
            n_keys = k_ref.shape[2]
            size = min(tk, n_keys)
            assert n_keys % size == 0
            blocks += [(k_ref, vt_ref, s, size) for s in range(0, n_keys, size)]
        return blocks

    def scores(blk):
        k_ref, _, s, size = blk
        return lax.dot_general(k_ref[0, 0, s:s + size, :], q, nt, preferred_element_type=F32)

    def weighted_values(blk, p):
        _, vt_ref, s, size = blk
        return jnp.dot(vt_ref[0, 0, :, s:s + size], p.astype(BF16), preferred_element_type=F32)

    def store(acc, l):
        o_ref[0] = jnp.transpose(acc / l).astype(o_ref.dtype)

    qf = q.astype(F32)
    qn2 = lax.dot_general(jnp.ones((8, q.shape[1]), BF16), (qf * qf).astype(BF16), nt,
                          preferred_element_type=F32)[:1]
    kmax = functools.reduce(jnp.maximum, [kmax_ref[0, 0, :1, :1] for _, _, kmax_ref in sources])
    ref = jnp.sqrt(qn2 * kmax) * ATTN_BOUND_SLACK
    l = jnp.zeros((1, tq), F32)
    acc = jnp.zeros((V_HEAD, tq), F32)
    blocks = key_blocks(ATTN_TK_FIXED)
    st_next = scores(blocks[0])
    for j, blk in enumerate(blocks):
        st = st_next
        if j + 1 < len(blocks):
            st_next = scores(blocks[j + 1])
        p = jnp.exp2(st - ref)
        l = l + jnp.sum(p, axis=0, keepdims=True)
        acc = acc + weighted_values(blk, p)
    store(acc, l)

    @pl.when(jnp.logical_not(jnp.min(l) >= ATTN_MIN_DENOM))
    def _():
        blocks = key_blocks(ATTN_TK_ONLINE)
        m = jnp.full((1, tq), -jnp.inf, F32)
        l = jnp.zeros((1, tq), F32)
        acc = jnp.zeros((V_HEAD, tq), F32)
        st_next = scores(blocks[0])
        for j, blk in enumerate(blocks):
            st = st_next
            if j + 1 < len(blocks):
                st_next = scores(blocks[j + 1])
            m_new = jnp.maximum(m, jnp.max(st, axis=0, keepdims=True))
            alpha = jnp.exp2(m - m_new)
            p = jnp.exp2(st - m_new)
            l = alpha * l + jnp.sum(p, axis=0, keepdims=True)
            acc = alpha * acc + weighted_values(blk, p)
            m = m_new
        store(acc, l)


def _attention(q, sources):
    b, h, n, _ = q.shape
    tq = min(ATTN_TQ, n)
    in_specs = [pl.BlockSpec((1, 1, tq, QK_HEAD), lambda bi, hi, i: (bi, hi, i, 0))]
    args = [q]
    for k, vt, kmax in sources:
        m = k.shape[2]
        in_specs += [pl.BlockSpec((1, 1, m, QK_HEAD), lambda bi, hi, i: (bi, hi, 0, 0)),
                     pl.BlockSpec((1, 1, V_HEAD, m), lambda bi, hi, i: (bi, hi, 0, 0)),
                     pl.BlockSpec((1, 1, 8, 128), lambda bi, hi, i: (bi, hi, 0, 0))]
        args += [k, vt, kmax]
    return pl.pallas_call(
        functools.partial(_attn_kernel, n_sources=len(sources)),
        grid=(b, h, n // tq),
        in_specs=in_specs,
        out_specs=pl.BlockSpec((1, tq, V_HEAD), lambda bi, hi, i: (bi, i, hi)),
        out_shape=jax.ShapeDtypeStruct((b, n, h * V_HEAD), BF16),
        compiler_params=_params("parallel", "parallel", "parallel"),
        name="attention",
    )(*args)


def _pool_kernel(u_ref, prev_ref, next_ref, w_ref, s_ref, o_ref, ext_sc, *, n, tm):
    i = pl.program_id(1)
    keep_prev = jnp.where(i > 0, 1.0, 0.0)
    keep_next = jnp.where(i < pl.num_programs(1) - 1, 1.0, 0.0)
    ext_sc[0:POOL_HALO] = prev_ref[0] * keep_prev
    ext_sc[POOL_HALO:POOL_HALO + tm] = u_ref[0]
    ext_sc[POOL_HALO + tm:] = next_ref[0] * keep_next
    t = i * tm + lax.broadcasted_iota(jnp.int32, (tm, 1), 0)
    for g, win in enumerate(POOL_WINDOWS):
        half = win // 2
        cols = slice(g * POOL_GROUP, (g + 1) * POOL_GROUP)
        tot = ext_sc[POOL_HALO - half:POOL_HALO - half + tm, cols]
        for d in range(1 - half, half):
            tot = tot + ext_sc[POOL_HALO + d:POOL_HALO + d + tm, cols]
        cnt = (jnp.minimum(t + half, n) - jnp.maximum(t - half, 0)).astype(F32)
        pooled = (tot / cnt - ext_sc[POOL_HALO:POOL_HALO + tm, cols]).astype(BF16)
        y = jnp.dot(pooled, w_ref[g], preferred_element_type=F32) * s_ref[:, cols]
        o_ref[0, :, cols] = y.astype(o_ref.dtype)


def _pool(p, w_pool, pool_scale):
    b, n, _ = p.shape
    tm = min(512, n)
    hb = tm // POOL_HALO
    last_hb = n // POOL_HALO - 1
    return pl.pallas_call(
        functools.partial(_pool_kernel, n=n, tm=tm),
        grid=(b, n // tm),
        in_specs=[pl.BlockSpec((1, tm, POOL_WIDTH), lambda bi, i: (bi, i, 0)),
                  pl.BlockSpec((1, POOL_HALO, POOL_WIDTH), lambda bi, i: (bi, jnp.maximum(i * hb - 1, 0), 0)),
                  pl.BlockSpec((1, POOL_HALO, POOL_WIDTH),
                               lambda bi, i: (bi, jnp.minimum((i + 1) * hb, last_hb), 0)),
                  _resident(w_pool.shape),
                  pl.BlockSpec((1, POOL_WIDTH), lambda bi, i: (0, 0))],
        out_specs=pl.BlockSpec((1, tm, POOL_WIDTH), lambda bi, i: (bi, i, 0)),
        out_shape=jax.ShapeDtypeStruct((b, n, POOL_WIDTH), BF16),
        scratch_shapes=[pltpu.VMEM((tm + 2 * POOL_HALO, POOL_WIDTH), F32)],
        compiler_params=_params("parallel", "parallel"),
        name="pool",
    )(p, p, p, w_pool, pool_scale.reshape(1, POOL_WIDTH))


def _dft1_kernel(x_ref, g_ref, sh_ref, sc_ref, a_ref, o_ref, h_sc):
    gs = g_ref[...] * (1.0 + sc_ref[0])
    sh = sh_ref[0]
    for t1 in range(DFT_N1):
        x = x_ref[0, t1]
        inv = lax.rsqrt(jnp.mean(x * x, axis=-1, keepdims=True) + EPS)
        h_sc[t1 * DFT_ROWS:(t1 + 1) * DFT_ROWS, :] = (x * inv * gs + sh).astype(h_sc.dtype)
    for c in range(0, D_MODEL, DFT_COLS):
        y = jnp.dot(a_ref[...], h_sc[:, c:c + DFT_COLS], preferred_element_type=F32)
        o_ref[0, :, :, :, c:c + DFT_COLS] = y.astype(o_ref.dtype).reshape(2, DFT_N1, DFT_ROWS, DFT_COLS)


def _dft_stage1(x, g, sh, sc, a_kron):
    b, n, d = x.shape
    vec = pl.BlockSpec((1, 1, d), lambda bi, r: (bi, 0, 0))
    return pl.pallas_call(
        _dft1_kernel,
        grid=(b, DFT_N2 // DFT_ROWS),
        in_specs=[pl.BlockSpec((1, DFT_N1, DFT_ROWS, d), lambda bi, r: (bi, 0, r, 0)),
                  pl.BlockSpec((1, d), lambda bi, r: (0, 0)), vec, vec,
                  _resident(a_kron.shape)],
        out_specs=pl.BlockSpec((1, 2, DFT_N1, DFT_ROWS, d), lambda bi, r: (bi, 0, 0, r, 0)),
        out_shape=jax.ShapeDtypeStruct((b, 2, DFT_N1, DFT_N2, d), BF16),
        scratch_shapes=[pltpu.VMEM((DFT_N1 * DFT_ROWS, d), BF16)],
        compiler_params=_params("parallel", "parallel"),
        name="dft_stage1",
    )(x.reshape(b, DFT_N1, DFT_N2, d), g.reshape(1, d), sh, sc, a_kron)


def _dft2_rows(y, m, cc_ref, sc_ref, rows_out, seq_len, store):
    pq = jnp.dot(m, y, preferred_element_type=F32).astype(BF16)
    ortho = 1.0 / math.sqrt(seq_len * FOURIER_GROUP)
    for g in range(FOURIER_GROUPS):
        cols = slice(g * FOURIER_GROUP, (g + 1) * FOURIER_GROUP)
        f = (jnp.dot(pq[:rows_out, cols], cc_ref[...], preferred_element_type=F32)
             - jnp.dot(pq[rows_out:, cols], sc_ref[...], preferred_element_type=F32))
        store(cols, f * ortho)


def _dft2_kernel(y_ref, m_ref, cc_ref, sc_ref, o_ref):
    for s in range(y_ref.shape[2]):
        y = jnp.concatenate([y_ref[0, 0, s], y_ref[0, 1, s]], axis=0)

        def store(cols, f, s=s):
            o_ref[0, :, s * D_MODEL + cols.start:s * D_MODEL + cols.stop] = f.astype(o_ref.dtype)

        _dft2_rows(y, m_ref[s], cc_ref, sc_ref, DFT_N2, DFT_N1 * DFT_N2, store)


def _dft_small_kernel(h_ref, m_ref, cc_ref, sc_ref, o_ref):
    def store(cols, f):
        o_ref[0, :, cols] = f.astype(o_ref.dtype)

    n = h_ref.shape[1]
    _dft2_rows(h_ref[0], m_ref[0], cc_ref, sc_ref, n, n, store)


def _dft_stage2(y, m, cc, sc):
    b = y.shape[0]
    return pl.pallas_call(
        _dft2_kernel,
        grid=(b, DFT_N1 // DFT2_K1),
        in_specs=[pl.BlockSpec((1, 2, DFT2_K1, DFT_N2, D_MODEL), lambda bi, k: (bi, 0, k, 0, 0)),
                  pl.BlockSpec((DFT2_K1, 2 * DFT_N2, 2 * DFT_N2), lambda bi, k: (k, 0, 0)),
                  _resident(cc.shape), _resident(sc.shape)],
        out_specs=pl.BlockSpec((1, DFT_N2, DFT2_K1 * D_MODEL), lambda bi, k: (bi, 0, k)),
        out_shape=jax.ShapeDtypeStruct((b, DFT_N2, DFT_N1 * D_MODEL), BF16),
        compiler_params=_params("parallel", "parallel"),
        name="dft_stage2",
    )(y, m, cc, sc)


def _dft_small(h, m, cc, sc):
    b, n, d = h.shape
    return pl.pallas_call(
        _dft_small_kernel,
        grid=(b,),
        in_specs=[pl.BlockSpec((1, n, d), lambda bi: (bi, 0, 0)),
                  _resident(m.shape), _resident(cc.shape), _resident(sc.shape)],
        out_specs=pl.BlockSpec((1, n, d), lambda bi: (bi, 0, 0)),
        out_shape=jax.ShapeDtypeStruct((b, n, d), BF16),
        compiler_params=_params("parallel"),
        name="dft_small",
    )(h, m, cc, sc)


def _cos_sin(num, den):
    ang = (num % den).astype(F32) * (2.0 * math.pi / den)
    return jnp.cos(ang), jnp.sin(ang)


def _dft_tables(n_ctx):
    i1 = jnp.arange(DFT_N1, dtype=jnp.int32)
    c1, s1 = _cos_sin(i1[:, None] * i1[None, :], DFT_N1)
    a = jnp.concatenate([c1, -s1], axis=0)
    eye = jnp.eye(DFT_ROWS, dtype=F32)
    a_kron = jnp.einsum('pt,jk->pjtk', a, eye).reshape(2 * DFT_N1 * DFT_ROWS, DFT_N1 * DFT_ROWS)
    n = DFT_N1 * DFT_N2
    t2 = jnp.arange(DFT_N2, dtype=jnp.int32)
    k = i1[:, None, None] + DFT_N1 * t2[None, :, None]
    ck, sk = _cos_sin(k * t2[None, None, :], n)
    m = jnp.concatenate([jnp.concatenate([ck, sk], axis=2),
                         jnp.concatenate([sk, -ck], axis=2)], axis=1)
    ic = jnp.arange(FOURIER_GROUP, dtype=jnp.int32)
    cc, sc = _cos_sin(ic[:, None] * ic[None, :], FOURIER_GROUP)
    il = jnp.arange(n_ctx, dtype=jnp.int32)
    cl, sl = _cos_sin(il[:, None] * il[None, :], n_ctx)
    m_ctx = jnp.concatenate([cl, sl], axis=0)[None]

    return dict(a_kron=a_kron.astype(BF16), m=m.astype(BF16), m_ctx=m_ctx.astype(BF16),
                cc=cc.astype(BF16), sc=sc.astype(BF16))


def _mlp_kernel(x_ref, g_ref, sh_ref, sc_ref, gate_ref, w1_ref, w2_ref, *rest, final):
    o_ref, h_sc, acc_sc = rest[-3:]
    j = pl.program_id(2)
    last = pl.num_programs(2) - 1
    tm = x_ref.shape[1]
    chunks = [slice(r, r + MLP_ROWS) for r in range(0, tm, MLP_ROWS)]

    def hidden(rows):
        a = jnp.maximum(jnp.dot(h_sc[rows, :], w1_ref[...], preferred_element_type=F32), 0.0)
        return (a * a).astype(BF16)

    @pl.when(j == 0)
    def _():
        parts = []
        for rows in chunks:
            _normmod_rows(x_ref, g_ref, sh_ref, sc_ref, h_sc, rows.start, rows.stop)
            parts.append(hidden(rows))
        acc_sc[...] = jnp.dot(jnp.concatenate(parts, axis=0), w2_ref[...], preferred_element_type=F32)

    @pl.when(jnp.logical_and(j > 0, j < last))
    def _():
        acc_sc[...] += jnp.dot(hidden(slice(None)), w2_ref[...], preferred_element_type=F32)

    @pl.when(j == last)
    def _():
        gate = gate_ref[0]
        a = hidden(slice(None))
        for rows in chunks:
            upd = acc_sc[rows, :] + jnp.dot(a[rows, :], w2_ref[...], preferred_element_type=F32)
            for r in range(0, MLP_ROWS, NORM_ROWS):
                y = x_ref[0, rows.start + r:rows.start + r + NORM_ROWS, :] + gate * upd[r:r + NORM_ROWS, :]
                if final:
                    y = _rms(y, rest[0][...])
                o_ref[0, rows.start + r:rows.start + r + NORM_ROWS, :] = y


def _mlp(x, g, sh, sc, gate, w1, w2, layer, final_g=None):
    b, n, d = x.shape
    tm = min(MLP_TM, n)
    tf = MLP_TF
    vec = pl.BlockSpec((1, 1, d), lambda bi, i, j: (bi, 0, 0))
    row = pl.BlockSpec((1, d), lambda bi, i, j: (0, 0))
    in_specs = [pl.BlockSpec((1, tm, d), lambda bi, i, j: (bi, i, 0)), row, vec, vec, vec,
                pl.BlockSpec((None, d, tf), lambda bi, i, j: (layer, 0, j)),
                pl.BlockSpec((None, tf, d), lambda bi, i, j: (layer, j, 0))]
    args = [x, g.reshape(1, d), sh, sc, gate, w1, w2]
    if final_g is not None:
        in_specs.append(row)
        args.append(final_g.reshape(1, d))
    return pl.pallas_call(
        functools.partial(_mlp_kernel, final=final_g is not None),
        grid=(b, n // tm, D_FF // tf),
        in_specs=in_specs,
        out_specs=pl.BlockSpec((1, tm, d), lambda bi, i, j: (bi, i, 0)),
        out_shape=jax.ShapeDtypeStruct((b, n, d), F32),
        scratch_shapes=[pltpu.VMEM((tm, d), BF16), pltpu.VMEM((tm, d), F32)],
        compiler_params=_params("parallel", "parallel", "arbitrary"),
        name="mlp",
    )(*args)


def _rotate_half_axial(x):
    xr = x.reshape(x.shape[:-1] + (2, 2, ROPE_FREQS))
    return jnp.concatenate([-xr[..., 1:, :], xr[..., :1, :]], axis=-2).reshape(x.shape)


def _rope_tables(n):
    rows = n // GRID_W
    r = jnp.broadcast_to(jnp.arange(rows, dtype=F32)[:, None], (rows, GRID_W)).reshape(n)
    col = jnp.broadcast_to(jnp.arange(GRID_W, dtype=F32)[None, :], (rows, GRID_W)).reshape(n)
    inv = ROPE_THETA ** (-2.0 * jnp.arange(ROPE_FREQS, dtype=F32) / ROPE_AXIS)
    ang = jnp.stack([r[:, None] * inv, col[:, None] * inv], axis=1)
    ang = jnp.broadcast_to(ang[:, :, None, :], (n, 2, 2, ROPE_FREQS)).reshape(n, QK_ROPE)
    zeros = jnp.zeros((n, QK_ROPE), F32)
    return (jnp.concatenate([jnp.cos(ang), zeros], axis=1), jnp.concatenate([jnp.sin(ang), zeros], axis=1))


def _even_weights(w_in, w_uq, w_ukv):
    w_kr = w_in[:, Q_RANK + KV_RANK:Q_RANK + KV_RANK + QK_ROPE]
    w_in2 = jnp.concatenate([w_in[:, Q_RANK + KV_RANK + QK_ROPE:], w_in[:, :Q_RANK],
                             w_in[:, Q_RANK:Q_RANK + KV_RANK], w_kr, _rotate_half_axial(w_kr)], axis=1)
    wq = w_uq.reshape(Q_RANK, N_HEADS, QK_HEAD)
    wq = jnp.concatenate([wq, _rotate_half_axial(wq[..., QK_NOPE:])], axis=-1)
    wq = wq.reshape(Q_RANK, N_HEADS * (QK_HEAD + QK_ROPE))
    wkv = w_ukv.reshape(KV_RANK, N_HEADS, QK_NOPE + V_HEAD)
    wk = wkv[..., :QK_NOPE].reshape(KV_RANK, N_HEADS * QK_NOPE)
    wvt = jnp.transpose(wkv[..., QK_NOPE:], (1, 2, 0)).reshape(N_HEADS * V_HEAD, KV_RANK)
    return w_in2.astype(BF16), wq.astype(BF16), wk.astype(BF16), wvt.astype(BF16)


def kernel(x, c, ctx, c_ctx, w_mod, b_mod, norm1, norm2, w_in, q_norm, w_uq, kv_norm, w_ukv, w_pool,
           pool_scale, w_out_even, w_out_odd, w_mlp1, w_mlp2, final_norm):
    b, n, d = x.shape
    n_ctx = ctx.shape[1]
    assert n == DFT_N1 * DFT_N2 and d == D_MODEL and n % GRID_W == 0
    assert n % 512 == 0 and n % n_ctx == 0

    cos2, sin2 = _rope_tables(n)
    dft = _dft_tables(n_ctx)
    w1 = w_mlp1.astype(BF16)
    w2 = w_mlp2.astype(BF16)

    cc = jnp.concatenate([c, c_ctx[None], jnp.zeros((8 - b - 1, d), F32)], axis=0)
    mods = _modulation(cc, w_mod, b_mod)

    updates_ctx = [False] * DEPTH
    for l in reversed(range(DEPTH - 1)):
        updates_ctx[l] = (l + 1) % 2 == 0 or updates_ctx[l + 1]

    for l in range(DEPTH):
        even = l % 2 == 0
        i = l // 2
        mx = [mods[l, :b, k * d:(k + 1) * d].reshape(b, 1, d) for k in range(6)]
        mc = [jnp.broadcast_to(mods[l, b, k * d:(k + 1) * d].reshape(1, 1, d), (b, 1, d)) for k in range(6)]
        ctx_update = updates_ctx[l]

        if even:
            w_in2, wq, wk, wvt = _even_weights(w_in[i], w_uq[i], w_ukv[i])
            w_out = w_out_even[i].astype(BF16)
            wp = w_pool[i].astype(BF16)
            ux, q, *keys_x = _evenproj(x, norm1[l], mx[0], mx[1], w_in2, q_norm[i], wq, kv_norm[i], wk, wvt,
                                       cos2, sin2, rope=True)
            uc, cq, *keys_c = _evenproj(ctx, norm1[l], mc[0], mc[1], w_in2, q_norm[i], wq, kv_norm[i], wk, wvt,
                                        cos2, sin2, rope=False)
            attn_x = _attention(q, [keys_x, keys_c])
            pool_x = _pool(ux, wp, pool_scale[i])
            x = _rowmm([attn_x, pool_x], [w_out[:MLA_OUT], w_out[MLA_OUT:]], F32, res=x, gate=mx[2])
            if ctx_update:
                attn_c = _attention(cq, [keys_c])
                pool_c = _pool(uc, wp, pool_scale[i])
                ctx = _rowmm([attn_c, pool_c], [w_out[:MLA_OUT], w_out[MLA_OUT:]], F32, res=ctx, gate=mc[2])
        else:
            w_out = w_out_odd[i].astype(BF16)
            if ctx_update:
                hc = _normmod(ctx, norm1[l], mc[0], mc[1])
            y1 = _dft_stage1(x, norm1[l], mx[0], mx[1], dft['a_kron'])
            fx = _dft_stage2(y1, dft['m'], dft['cc'], dft['sc'])
            x = _rowmm_strided(fx, w_out, x, mx[2])
            if ctx_update:
                fc = _dft_small(hc, dft['m_ctx'], dft['cc'], dft['sc'])
                ctx = _rowmm([fc], [w_out], F32, res=ctx, gate=mc[2])
        x = _mlp(x, norm2[l], mx[3], mx[4], mx[5], w1, w2, l, final_norm if l == DEPTH - 1 else None)
        if ctx_update:
            ctx = _mlp(ctx.reshape(1, b * n_ctx, d), norm2[l], mc[3][:1], mc[4][:1], mc[5][:1],
                       w1, w2, l).reshape(b, n_ctx, d)
    return x
```

```python
import functools
import math

import jax
import jax.numpy as jnp
from jax import lax
from jax.experimental import pallas as pl
from jax.experimental.pallas import tpu as pltpu

D_MODEL = 2048
DEPTH = 4
GRID_W = 64
N_HEADS = 8
Q_RANK = 512
KV_RANK = 256
QK_NOPE = 128
QK_ROPE = 64
V_HEAD = 128
QK_HEAD = QK_NOPE + QK_ROPE
ATTN_SCALE = QK_HEAD ** -0.5
Q_SCALE = ATTN_SCALE * math.log2(math.e)
MLA_OUT = N_HEADS * V_HEAD
ROPE_AXIS = QK_ROPE // 2
ROPE_FREQS = ROPE_AXIS // 2
ROPE_THETA = 10000.0
POOL_WIDTH = D_MODEL // 2
POOL_WINDOWS = (2, 4, 8, 16)
POOL_GROUP = POOL_WIDTH // len(POOL_WINDOWS)
POOL_HALO = 8
FOURIER_GROUPS = 4
FOURIER_GROUP = D_MODEL // FOURIER_GROUPS
D_FF = 4 * D_MODEL
EPS = 1e-6

P_POOL = 0
P_QLAT = POOL_WIDTH
P_KV = POOL_WIDTH + Q_RANK
P_WIDTH = POOL_WIDTH + Q_RANK + KV_RANK + 2 * QK_ROPE

DFT_N1 = 64
DFT_N2 = 128
DFT_ROWS = 16
DFT2_K1 = 8
DFT_STRIDE_BLOCK = 8
DFT_COLS = 512

NORM_ROWS = 16
MLP_TM = 512
MLP_TF = 1024
MLP_ROWS = 256
ATTN_TQ = 1024
ATTN_TK_FIXED = 4096
ATTN_TK_ONLINE = 256
ATTN_BOUND_SLACK = 1.02
ATTN_MIN_DENOM = 2.0 ** -60

V7X_VMEM_LIMIT = 56 * 1024 * 1024

F32 = jnp.float32
BF16 = jnp.bfloat16


def _params(*sem):
    return pltpu.CompilerParams(dimension_semantics=sem, vmem_limit_bytes=V7X_VMEM_LIMIT)


def _resident(shape):
    nd = len(shape)
    return pl.BlockSpec(shape, lambda *_: (0,) * nd, pipeline_mode=pl.Buffered(1))


def _rms(x, g):
    return x * lax.rsqrt(jnp.mean(x * x, axis=-1, keepdims=True) + EPS) * g


def _split_bf16(a):
    hi = a.astype(BF16)
    return hi, (a - hi.astype(F32)).astype(BF16)


def _mod_kernel(c_ref, w_ref, b_ref, o_ref):
    c = c_ref[...]
    s = c * (1.0 / (1.0 + jnp.exp(-c)))
    s_hi, s_lo = _split_bf16(s)
    w_hi, w_lo = _split_bf16(w_ref[0])
    rows = s.shape[0]
    top = jnp.dot(jnp.concatenate([s_hi, s_lo], axis=0), w_hi, preferred_element_type=F32)
    o_ref[0] = top[:rows] + top[rows:] + jnp.dot(s_hi, w_lo, preferred_element_type=F32) + b_ref[0]


def _modulation(cc, w_mod, b_mod):
    rows = cc.shape[0]
    tn = 1536
    n6 = 6 * D_MODEL
    return pl.pallas_call(
        _mod_kernel,
        grid=(DEPTH, n6 // tn),
        in_specs=[
            pl.BlockSpec((rows, D_MODEL), lambda l, j: (0, 0)),
            pl.BlockSpec((1, D_MODEL, tn), lambda l, j: (l, 0, j)),
            pl.BlockSpec((1, 1, tn), lambda l, j: (l, 0, j)),
        ],
        out_specs=pl.BlockSpec((1, rows, tn), lambda l, j: (l, 0, j)),
        out_shape=jax.ShapeDtypeStruct((DEPTH, rows, n6), F32),
        compiler_params=_params("parallel", "parallel"),
        name="modulation",
    )(cc, w_mod, b_mod.reshape(DEPTH, 1, n6))


def _normmod_rows(x_ref, g_ref, sh_ref, sc_ref, dst_ref, start=0, stop=None):
    gs = g_ref[...] * (1.0 + sc_ref[0])
    sh = sh_ref[0]
    for r in range(start, x_ref.shape[1] if stop is None else stop, NORM_ROWS):
        x = x_ref[0, r:r + NORM_ROWS, :]
        inv = lax.rsqrt(jnp.mean(x * x, axis=-1, keepdims=True) + EPS)
        dst_ref[r:r + NORM_ROWS, :] = (x * inv * gs + sh).astype(dst_ref.dtype)


def _normmod_kernel(x_ref, g_ref, sh_ref, sc_ref, o_ref):
    _normmod_rows(x_ref, g_ref, sh_ref, sc_ref, o_ref.at[0])


def _normmod(x, g, sh, sc, out_dtype=BF16):
    b, n, d = x.shape
    tm = min(512, n)
    vec = pl.BlockSpec((1, 1, d), lambda bi, i: (bi, 0, 0))
    return pl.pallas_call(
        _normmod_kernel,
        grid=(b, n // tm),
        in_specs=[pl.BlockSpec((1, tm, d), lambda bi, i: (bi, i, 0)),
                  pl.BlockSpec((1, d), lambda bi, i: (0, 0)), vec, vec],
        out_specs=pl.BlockSpec((1, tm, d), lambda bi, i: (bi, i, 0)),
        out_shape=jax.ShapeDtypeStruct((b, n, d), out_dtype),
        compiler_params=_params("parallel", "parallel"),
        name="normmod",
    )(x, g.reshape(1, d), sh, sc)


def _rope_cols(t, cos2, sin2, rope):
    if not rope:
        return t
    return t * cos2 + pltpu.roll(t, QK_ROPE, 1) * sin2


def _evenproj_kernel(x_ref, g_ref, sh_ref, sc_ref, w_ref, qg_ref, wq_ref, kvg_ref, wk_ref, wvt_ref, c_ref, s_ref,
                     u_ref, q_ref, k_ref, vt_ref, kmax_ref, h_sc, *, rope):
    @pl.when(pl.program_id(1) == 0)
    def _():
        kmax_ref[...] = jnp.zeros(kmax_ref.shape, F32)

    width = QK_NOPE + 2 * QK_ROPE
    for r in range(0, x_ref.shape[1], MLP_ROWS):
        rows = slice(r, r + MLP_ROWS)
        _normmod_rows(x_ref, g_ref, sh_ref, sc_ref, h_sc, r, r + MLP_ROWS)
        p = jnp.dot(h_sc[rows, :], w_ref[...], preferred_element_type=F32)
        u_ref[0, rows, :] = p[:, P_POOL:P_POOL + POOL_WIDTH]
        cos2, sin2 = c_ref[rows, :], s_ref[rows, :]
        z = _rms(p[:, P_QLAT:P_QLAT + Q_RANK], qg_ref[...]).astype(BF16)
        acc = jnp.dot(z, wq_ref[...], preferred_element_type=F32)
        for h in range(N_HEADS):
            q_ref[0, h, rows, :QK_NOPE] = (acc[:, h * width:h * width + QK_NOPE] * Q_SCALE).astype(q_ref.dtype)
            t = _rope_cols(acc[:, h * width + QK_NOPE:(h + 1) * width], cos2, sin2, rope)
            q_ref[0, h, rows, QK_NOPE:] = (t[:, :QK_ROPE] * Q_SCALE).astype(q_ref.dtype)
        zk = _rms(p[:, P_KV:P_KV + KV_RANK], kvg_ref[...]).astype(BF16)
        kr = _rope_cols(p[:, P_KV + KV_RANK:], cos2, sin2, rope)[:, :QK_ROPE].astype(k_ref.dtype)
        kn = jnp.dot(zk, wk_ref[...], preferred_element_type=F32).astype(k_ref.dtype)
        vt = lax.dot_general(wvt_ref[...], zk, (((1,), (1,)), ((), ())),
                             preferred_element_type=F32).astype(vt_ref.dtype)
        kn2 = kn.astype(F32) * kn.astype(F32)
        kr2 = jnp.sum(kr.astype(F32) * kr.astype(F32), axis=1, keepdims=True)
        for h in range(N_HEADS):
            k_ref[0, h, rows, :QK_NOPE] = kn[:, h * QK_NOPE:(h + 1) * QK_NOPE]
            k_ref[0, h, rows, QK_NOPE:] = kr
            vt_ref[0, h, :, rows] = vt[h * V_HEAD:(h + 1) * V_HEAD, :]
            norm2 = jnp.sum(kn2[:, h * QK_NOPE:(h + 1) * QK_NOPE], axis=1, keepdims=True) + kr2
            kmax_ref[0, h] = jnp.maximum(kmax_ref[0, h], jnp.max(norm2, axis=0, keepdims=True))


def _evenproj(x, g, sh, sc, w_in2, q_norm, wq, kv_norm, wk, wvt, cos2, sin2, rope):
    b, n, d = x.shape
    tm = min(512, n)
    vec = pl.BlockSpec((1, 1, d), lambda bi, i: (bi, 0, 0))
    tab = pl.BlockSpec((tm, 2 * QK_ROPE), lambda bi, i: (i, 0))
    heads = pl.BlockSpec((1, N_HEADS, tm, QK_HEAD), lambda bi, i: (bi, 0, i, 0))
    return pl.pallas_call(
        functools.partial(_evenproj_kernel, rope=rope),
        grid=(b, n // tm),
        in_specs=[pl.BlockSpec((1, tm, d), lambda bi, i: (bi, i, 0)),
                  pl.BlockSpec((1, d), lambda bi, i: (0, 0)), vec, vec, _resident(w_in2.shape),
                  pl.BlockSpec((1, Q_RANK), lambda bi, i: (0, 0)), _resident(wq.shape),
                  pl.BlockSpec((1, KV_RANK), lambda bi, i: (0, 0)), _resident(wk.shape), _resident(wvt.shape),
                  tab, tab],
        out_specs=[pl.BlockSpec((1, tm, POOL_WIDTH), lambda bi, i: (bi, i, 0)), heads, heads,
                   pl.BlockSpec((1, N_HEADS, V_HEAD, tm), lambda bi, i: (bi, 0, 0, i)),
                   pl.BlockSpec((1, N_HEADS, 8, 128), lambda bi, i: (bi, 0, 0, 0))],
        out_shape=[jax.ShapeDtypeStruct((b, n, POOL_WIDTH), F32),
                   jax.ShapeDtypeStruct((b, N_HEADS, n, QK_HEAD), BF16),
                   jax.ShapeDtypeStruct((b, N_HEADS, n, QK_HEAD), BF16),
                   jax.ShapeDtypeStruct((b, N_HEADS, V_HEAD, n), BF16),
                   jax.ShapeDtypeStruct((b, N_HEADS, 8, 128), F32)],
        scratch_shapes=[pltpu.VMEM((tm, d), BF16)],
        compiler_params=_params("parallel", "arbitrary"),
        name="evenproj",
    )(x, g.reshape(1, d), sh, sc, w_in2, q_norm.reshape(1, Q_RANK), wq, kv_norm.reshape(1, KV_RANK), wk, wvt,
      cos2, sin2)


def _rowmm_kernel(*refs, n_in, has_res):
    a_refs, w_refs = refs[:n_in], refs[n_in:2 * n_in]
    o_ref = refs[-1]
    acc = None
    for a, w in zip(a_refs, w_refs):
        d = jnp.dot(a[0], w[...], preferred_element_type=F32)
        acc = d if acc is None else acc + d
    if has_res:
        res_ref, gate_ref = refs[2 * n_in], refs[2 * n_in + 1]
        acc = res_ref[0] + gate_ref[0] * acc
    o_ref[0] = acc.astype(o_ref.dtype)


def _rowmm(a_list, w_list, out_dtype, res=None, gate=None, tm=512):
    b, n, _ = a_list[0].shape
    nout = w_list[0].shape[1]
    tm = min(tm, n)
    in_specs = [pl.BlockSpec((1, tm, a.shape[2]), lambda bi, i: (bi, i, 0)) for a in a_list]
    in_specs += [_resident(w.shape) for w in w_list]
    args = list(a_list) + list(w_list)
    if res is not None:
        in_specs += [pl.BlockSpec((1, tm, nout), lambda bi, i: (bi, i, 0)),
                     pl.BlockSpec((1, 1, nout), lambda bi, i: (bi, 0, 0))]
        args += [res, gate]
    return pl.pallas_call(
        functools.partial(_rowmm_kernel, n_in=len(a_list), has_res=res is not None),
        grid=(b, n // tm),
        in_specs=in_specs,
        out_specs=pl.BlockSpec((1, tm, nout), lambda bi, i: (bi, i, 0)),
        out_shape=jax.ShapeDtypeStruct((b, n, nout), out_dtype),
        compiler_params=_params("parallel", "parallel"),
        name="rowmm",
    )(*args)


def _rowmm_strided_kernel(f_ref, w_ref, res_ref, gate_ref, o_ref):
    gate = gate_ref[0]
    d = w_ref.shape[0]
    n2 = f_ref.shape[1]
    steps = res_ref.shape[2]
    rows = jnp.concatenate([f_ref[0, :, s * d:(s + 1) * d] for s in range(steps)], axis=0)
    for c in range(0, d, DFT_COLS):
        cols = slice(c, c + DFT_COLS)
        y = jnp.dot(rows, w_ref[:, cols], preferred_element_type=F32)
        for s in range(steps):
            o_ref[0, :, s, cols] = res_ref[0, :, s, cols] + gate[:, cols] * y[s * n2:(s + 1) * n2, :]


def _rowmm_strided(f, w, res, gate):
    b, n, d = res.shape
    res4 = res.reshape(b, DFT_N2, DFT_N1, d)
    blk = pl.BlockSpec((1, DFT_N2, DFT_STRIDE_BLOCK, d), lambda bi, i: (bi, 0, i, 0))
    out = pl.pallas_call(
        _rowmm_strided_kernel,
        grid=(b, DFT_N1 // DFT_STRIDE_BLOCK),
        in_specs=[pl.BlockSpec((1, DFT_N2, DFT_STRIDE_BLOCK * d), lambda bi, i: (bi, 0, i)),
                  _resident(w.shape), blk,
                  pl.BlockSpec((1, 1, d), lambda bi, i: (bi, 0, 0))],
        out_specs=blk,
        out_shape=jax.ShapeDtypeStruct(res4.shape, F32),
        compiler_params=_params("parallel", "parallel"),
        name="rowmm_strided",
    )(f, w, res4, gate)
    return out.reshape(b, n, d)


def _attn_kernel(q_ref, *refs, n_sources):
    o_ref = refs[-1]
    sources = [refs[3 * s:3 * s + 3] for s in range(n_sources)]
    q = q_ref[0, 0]
    tq = q.shape[0]
    nt = (((1,), (1,)), ((), ()))

    def key_blocks(tk):
        blocks = []
        for k_ref, vt_ref, _ in sources:
            n_keys = k_ref.shape[2]
            size = min(tk, n_keys)
            assert n_keys % size == 0
            blocks += [(k_ref, vt_ref, s, size) for s in range(0, n_keys, size)]
        return blocks

    def scores(blk):
        k_ref, _, s, size = blk
        return lax.dot_general(k_ref[0, 0, s:s + size, :], q, nt, preferred_element_type=F32)

    def weighted_values(blk, p):
        _, vt_ref, s, size = blk
        return jnp.dot(vt_ref[0, 0, :, s:s + size], p.astype(BF16), preferred_element_type=F32)

    def store(acc, l):
        o_ref[0] = jnp.transpose(acc / l).astype(o_ref.dtype)

    qf = q.astype(F32)
    qn2 = lax.dot_general(jnp.ones((8, q.shape[1]), BF16), (qf * qf).astype(BF16), nt,
                          preferred_element_type=F32)[:1]
    kmax = functools.reduce(jnp.maximum, [kmax_ref[0, 0, :1, :1] for _, _, kmax_ref in sources])
    ref = jnp.sqrt(qn2 * kmax) * ATTN_BOUND_SLACK
    l = jnp.zeros((1, tq), F32)
    acc = jnp.zeros((V_HEAD, tq), F32)
    blocks = key_blocks(ATTN_TK_FIXED)
    st_next = scores(blocks[0])
    for j, blk in enumerate(blocks):
        st = st_next
        if j + 1 < len(blocks):
            st_next = scores(blocks[j + 1])
        p = jnp.exp2(st - ref)
        l = l + jnp.sum(p, axis=0, keepdims=True)
        acc = acc + weighted_values(blk, p)
    store(acc, l)

    @pl.when(jnp.logical_not(jnp.min(l) >= ATTN_MIN_DENOM))
    def _():
        blocks = key_blocks(ATTN_TK_ONLINE)
        m = jnp.full((1, tq), -jnp.inf, F32)
        l = jnp.zeros((1, tq), F32)
        acc = jnp.zeros((V_HEAD, tq), F32)
        st_next = scores(blocks[0])
        for j, blk in enumerate(blocks):
            st = st_next
            if j + 1 < len(blocks):
                st_next = scores(blocks[j + 1])
            m_new = jnp.maximum(m, jnp.max(st, axis=0, keepdims=True))
            alpha = jnp.exp2(m - m_new)
            p = jnp.exp2(st - m_new)
            l = alpha * l + jnp.sum(p, axis=0, keepdims=True)
            acc = alpha * acc + weighted_values(blk, p)
            m = m_new
        store(acc, l)


def _attention(q, sources):
    b, h, n, _ = q.shape
    tq = min(ATTN_TQ, n)
    in_specs = [pl.BlockSpec((1, 1, tq, QK_HEAD), lambda bi, hi, i: (bi, hi, i, 0))]
    args = [q]
    for k, vt, kmax in sources:
        m = k.shape[2]
        in_specs += [pl.BlockSpec((1, 1, m, QK_HEAD), lambda bi, hi, i: (bi, hi, 0, 0)),
                     pl.BlockSpec((1, 1, V_HEAD, m), lambda bi, hi, i: (bi, hi, 0, 0)),
                     pl.BlockSpec((1, 1, 8, 128), lambda bi, hi, i: (bi, hi, 0, 0))]
        args += [k, vt, kmax]
    return pl.pallas_call(
        functools.partial(_attn_kernel, n_sources=len(sources)),
        grid=(b, h, n // tq),
        in_specs=in_specs,
        out_specs=pl.BlockSpec((1, tq, V_HEAD), lambda bi, hi, i: (bi, i, hi)),
        out_shape=jax.ShapeDtypeStruct((b, n, h * V_HEAD), BF16),
        compiler_params=_params("parallel", "parallel", "parallel"),
        name="attention",
    )(*args)


def _pool_kernel(u_ref, prev_ref, next_ref, w_ref, s_ref, o_ref, ext_sc, *, n, tm):
    i = pl.program_id(1)
    keep_prev = jnp.where(i > 0, 1.0, 0.0)
    keep_next = jnp.where(i < pl.num_programs(1) - 1, 1.0, 0.0)
    ext_sc[0:POOL_HALO] = prev_ref[0] * keep_prev
    ext_sc[POOL_HALO:POOL_HALO + tm] = u_ref[0]
    ext_sc[POOL_HALO + tm:] = next_ref[0] * keep_next
    t = i * tm + lax.broadcasted_iota(jnp.int32, (tm, 1), 0)
    for g, win in enumerate(POOL_WINDOWS):
        half = win // 2
        cols = slice(g * POOL_GROUP, (g + 1) * POOL_GROUP)
        tot = ext_sc[POOL_HALO - half:POOL_HALO - half + tm, cols]
        for d in range(1 - half, half):
            tot = tot + ext_sc[POOL_HALO + d:POOL_HALO + d + tm, cols]
        cnt = (jnp.minimum(t + half, n) - jnp.maximum(t - half, 0)).astype(F32)
        pooled = (tot / cnt - ext_sc[POOL_HALO:POOL_HALO + tm, cols]).astype(BF16)
        y = jnp.dot(pooled, w_ref[g], preferred_element_type=F32) * s_ref[:, cols]
        o_ref[0, :, cols] = y.astype(o_ref.dtype)


def _pool(p, w_pool, pool_scale):
    b, n, _ = p.shape
    tm = min(512, n)
    hb = tm // POOL_HALO
    last_hb = n // POOL_HALO - 1
    return pl.pallas_call(
        functools.partial(_pool_kernel, n=n, tm=tm),
        grid=(b, n // tm),
        in_specs=[pl.BlockSpec((1, tm, POOL_WIDTH), lambda bi, i: (bi, i, 0)),
                  pl.BlockSpec((1, POOL_HALO, POOL_WIDTH), lambda bi, i: (bi, jnp.maximum(i * hb - 1, 0), 0)),
                  pl.BlockSpec((1, POOL_HALO, POOL_WIDTH),
                               lambda bi, i: (bi, jnp.minimum((i + 1) * hb, last_hb), 0)),
                  _resident(w_pool.shape),
                  pl.BlockSpec((1, POOL_WIDTH), lambda bi, i: (0, 0))],
        out_specs=pl.BlockSpec((1, tm, POOL_WIDTH), lambda bi, i: (bi, i, 0)),
        out_shape=jax.ShapeDtypeStruct((b, n, POOL_WIDTH), BF16),
        scratch_shapes=[pltpu.VMEM((tm + 2 * POOL_HALO, POOL_WIDTH), F32)],
        compiler_params=_params("parallel", "parallel"),
        name="pool",
    )(p, p, p, w_pool, pool_scale.reshape(1, POOL_WIDTH))


def _dft1_kernel(x_ref, g_ref, sh_ref, sc_ref, a_ref, o_ref, h_sc):
    gs = g_ref[...] * (1.0 + sc_ref[0])
    sh = sh_ref[0]
    for t1 in range(DFT_N1):
        x = x_ref[0, t1]
        inv = lax.rsqrt(jnp.mean(x * x, axis=-1, keepdims=True) + EPS)
        h_sc[t1 * DFT_ROWS:(t1 + 1) * DFT_ROWS, :] = (x * inv * gs + sh).astype(h_sc.dtype)
    for c in range(0, D_MODEL, DFT_COLS):
        y = jnp.dot(a_ref[...], h_sc[:, c:c + DFT_COLS], preferred_element_type=F32)
        o_ref[0, :, :, :, c:c + DFT_COLS] = y.astype(o_ref.dtype).reshape(2, DFT_N1, DFT_ROWS, DFT_COLS)


def _dft_stage1(x, g, sh, sc, a_kron):
    b, n, d = x.shape
    vec = pl.BlockSpec((1, 1, d), lambda bi, r: (bi, 0, 0))
    return pl.pallas_call(
        _dft1_kernel,
        grid=(b, DFT_N2 // DFT_ROWS),
        in_specs=[pl.BlockSpec((1, DFT_N1, DFT_ROWS, d), lambda bi, r: (bi, 0, r, 0)),
                  pl.BlockSpec((1, d), lambda bi, r: (0, 0)), vec, vec,
                  _resident(a_kron.shape)],
        out_specs=pl.BlockSpec((1, 2, DFT_N1, DFT_ROWS, d), lambda bi, r: (bi, 0, 0, r, 0)),
        out_shape=jax.ShapeDtypeStruct((b, 2, DFT_N1, DFT_N2, d), BF16),
        scratch_shapes=[pltpu.VMEM((DFT_N1 * DFT_ROWS, d), BF16)],
        compiler_params=_params("parallel", "parallel"),
        name="dft_stage1",
    )(x.reshape(b, DFT_N1, DFT_N2, d), g.reshape(1, d), sh, sc, a_kron)


def _dft2_rows(y, m, cc_ref, sc_ref, rows_out, seq_len, store):
    pq = jnp.dot(m, y, preferred_element_type=F32).astype(BF16)
    ortho = 1.0 / math.sqrt(seq_len * FOURIER_GROUP)
    for g in range(FOURIER_GROUPS):
        cols = slice(g * FOURIER_GROUP, (g + 1) * FOURIER_GROUP)
        f = (jnp.dot(pq[:rows_out, cols], cc_ref[...], preferred_element_type=F32)
             - jnp.dot(pq[rows_out:, cols], sc_ref[...], preferred_element_type=F32))
        store(cols, f * ortho)


def _dft2_kernel(y_ref, m_ref, cc_ref, sc_ref, o_ref):
    for s in range(y_ref.shape[2]):
        y = jnp.concatenate([y_ref[0, 0, s], y_ref[0, 1, s]], axis=0)

        def store(cols, f, s=s):
            o_ref[0, :, s * D_MODEL + cols.start:s * D_MODEL + cols.stop] = f.astype(o_ref.dtype)

        _dft2_rows(y, m_ref[s], cc_ref, sc_ref, DFT_N2, DFT_N1 * DFT_N2, store)


def _dft_small_kernel(h_ref, m_ref, cc_ref, sc_ref, o_ref):
    def store(cols, f):
        o_ref[0, :, cols] = f.astype(o_ref.dtype)

    n = h_ref.shape[1]
    _dft2_rows(h_ref[0], m_ref[0], cc_ref, sc_ref, n, n, store)


def _dft_stage2(y, m, cc, sc):
    b = y.shape[0]
    return pl.pallas_call(
        _dft2_kernel,
        grid=(b, DFT_N1 // DFT2_K1),
        in_specs=[pl.BlockSpec((1, 2, DFT2_K1, DFT_N2, D_MODEL), lambda bi, k: (bi, 0, k, 0, 0)),
                  pl.BlockSpec((DFT2_K1, 2 * DFT_N2, 2 * DFT_N2), lambda bi, k: (k, 0, 0)),
                  _resident(cc.shape), _resident(sc.shape)],
        out_specs=pl.BlockSpec((1, DFT_N2, DFT2_K1 * D_MODEL), lambda bi, k: (bi, 0, k)),
        out_shape=jax.ShapeDtypeStruct((b, DFT_N2, DFT_N1 * D_MODEL), BF16),
        compiler_params=_params("parallel", "parallel"),
        name="dft_stage2",
    )(y, m, cc, sc)


def _dft_small(h, m, cc, sc):
    b, n, d = h.shape
    return pl.pallas_call(
        _dft_small_kernel,
        grid=(b,),
        in_specs=[pl.BlockSpec((1, n, d), lambda bi: (bi, 0, 0)),
                  _resident(m.shape), _resident(cc.shape), _resident(sc.shape)],
        out_specs=pl.BlockSpec((1, n, d), lambda bi: (bi, 0, 0)),
        out_shape=jax.ShapeDtypeStruct((b, n, d), BF16),
        compiler_params=_params("parallel"),
        name="dft_small",
    )(h, m, cc, sc)


def _cos_sin(num, den):
    ang = (num % den).astype(F32) * (2.0 * math.pi / den)
    return jnp.cos(ang), jnp.sin(ang)


def _dft_tables(n_ctx):
    i1 = jnp.arange(DFT_N1, dtype=jnp.int32)
    c1, s1 = _cos_sin(i1[:, None] * i1[None, :], DFT_N1)
    a = jnp.concatenate([c1, -s1], axis=0)
    eye = jnp.eye(DFT_ROWS, dtype=F32)
    a_kron = jnp.einsum('pt,jk->pjtk', a, eye).reshape(2 * DFT_N1 * DFT_ROWS, DFT_N1 * DFT_ROWS)
    n = DFT_N1 * DFT_N2
    t2 = jnp.arange(DFT_N2, dtype=jnp.int32)
    k = i1[:, None, None] + DFT_N1 * t2[None, :, None]
    ck, sk = _cos_sin(k * t2[None, None, :], n)
    m = jnp.concatenate([jnp.concatenate([ck, sk], axis=2),
                         jnp.concatenate([sk, -ck], axis=2)], axis=1)
    ic = jnp.arange(FOURIER_GROUP, dtype=jnp.int32)
    cc, sc = _cos_sin(ic[:, None] * ic[None, :], FOURIER_GROUP)
    il = jnp.arange(n_ctx, dtype=jnp.int32)
    cl, sl = _cos_sin(il[:, None] * il[None, :], n_ctx)
    m_ctx = jnp.concatenate([cl, sl], axis=0)[None]

    return dict(a_kron=a_kron.astype(BF16), m=m.astype(BF16), m_ctx=m_ctx.astype(BF16),
                cc=cc.astype(BF16), sc=sc.astype(BF16))


def _mlp_kernel(x_ref, g_ref, sh_ref, sc_ref, gate_ref, w1_ref, w2_ref, *rest, final):
    o_ref, h_sc, acc_sc = rest[-3:]
    j = pl.program_id(2)
    last = pl.num_programs(2) - 1
    tm = x_ref.shape[1]
    chunks = [slice(r, r + MLP_ROWS) for r in range(0, tm, MLP_ROWS)]

    def hidden(rows):
        a = jnp.maximum(jnp.dot(h_sc[rows, :], w1_ref[...], preferred_element_type=F32), 0.0)
        return (a * a).astype(BF16)

    @pl.when(j == 0)
    def _():
        parts = []
        for rows in chunks:
            _normmod_rows(x_ref, g_ref, sh_ref, sc_ref, h_sc, rows.start, rows.stop)
            parts.append(hidden(rows))
        acc_sc[...] = jnp.dot(jnp.concatenate(parts, axis=0), w2_ref[...], preferred_element_type=F32)

    @pl.when(jnp.logical_and(j > 0, j < last))
    def _():
        acc_sc[...] += jnp.dot(hidden(slice(None)), w2_ref[...], preferred_element_type=F32)

    @pl.when(j == last)
    def _():
        gate = gate_ref[0]
        a = hidden(slice(None))
        for rows in chunks:
            upd = acc_sc[rows, :] + jnp.dot(a[rows, :], w2_ref[...], preferred_element_type=F32)
            for r in range(0, MLP_ROWS, NORM_ROWS):
                y = x_ref[0, rows.start + r:rows.start + r + NORM_ROWS, :] + gate * upd[r:r + NORM_ROWS, :]
                if final:
                    y = _rms(y, rest[0][...])
                o_ref[0, rows.start + r:rows.start + r + NORM_ROWS, :] = y


def _mlp(x, g, sh, sc, gate, w1, w2, layer, final_g=None):
    b, n, d = x.shape
    tm = min(MLP_TM, n)
    tf = MLP_TF
    vec = pl.BlockSpec((1, 1, d), lambda bi, i, j: (bi, 0, 0))
    row = pl.BlockSpec((1, d), lambda bi, i, j: (0, 0))
    in_specs = [pl.BlockSpec((1, tm, d), lambda bi, i, j: (bi, i, 0)), row, vec, vec, vec,
                pl.BlockSpec((None, d, tf), lambda bi, i, j: (layer, 0, j)),
                pl.BlockSpec((None, tf, d), lambda bi, i, j: (layer, j, 0))]
    args = [x, g.reshape(1, d), sh, sc, gate, w1, w2]
    if final_g is not None:
        in_specs.append(row)
        args.append(final_g.reshape(1, d))
    return pl.pallas_call(
        functools.partial(_mlp_kernel, final=final_g is not None),
        grid=(b, n // tm, D_FF // tf),
        in_specs=in_specs,
        out_specs=pl.BlockSpec((1, tm, d), lambda bi, i, j: (bi, i, 0)),
        out_shape=jax.ShapeDtypeStruct((b, n, d), F32),
        scratch_shapes=[pltpu.VMEM((tm, d), BF16), pltpu.VMEM((tm, d), F32)],
        compiler_params=_params("parallel", "parallel", "arbitrary"),
        name="mlp",
    )(*args)


def _rotate_half_axial(x):
    xr = x.reshape(x.shape[:-1] + (2, 2, ROPE_FREQS))
    return jnp.concatenate([-xr[..., 1:, :], xr[..., :1, :]], axis=-2).reshape(x.shape)


def _rope_tables(n):
    rows = n // GRID_W
    r = jnp.broadcast_to(jnp.arange(rows, dtype=F32)[:, None], (rows, GRID_W)).reshape(n)
    col = jnp.broadcast_to(jnp.arange(GRID_W, dtype=F32)[None, :], (rows, GRID_W)).reshape(n)
    inv = ROPE_THETA ** (-2.0 * jnp.arange(ROPE_FREQS, dtype=F32) / ROPE_AXIS)
    ang = jnp.stack([r[:, None] * inv, col[:, None] * inv], axis=1)
    ang = jnp.broadcast_to(ang[:, :, None, :], (n, 2, 2, ROPE_FREQS)).reshape(n, QK_ROPE)
    zeros = jnp.zeros((n, QK_ROPE), F32)
    return (jnp.concatenate([jnp.cos(ang), zeros], axis=1), jnp.concatenate([jnp.sin(ang), zeros], axis=1))


def _even_weights(w_in, w_uq, w_ukv):
    w_kr = w_in[:, Q_RANK + KV_RANK:Q_RANK + KV_RANK + QK_ROPE]
    w_in2 = jnp.concatenate([w_in[:, Q_RANK + KV_RANK + QK_ROPE:], w_in[:, :Q_RANK],
                             w_in[:, Q_RANK:Q_RANK + KV_RANK], w_kr, _rotate_half_axial(w_kr)], axis=1)
    wq = w_uq.reshape(Q_RANK, N_HEADS, QK_HEAD)
    wq = jnp.concatenate([wq, _rotate_half_axial(wq[..., QK_NOPE:])], axis=-1)
    wq = wq.reshape(Q_RANK, N_HEADS * (QK_HEAD + QK_ROPE))
    wkv = w_ukv.reshape(KV_RANK, N_HEADS, QK_NOPE + V_HEAD)
    wk = wkv[..., :QK_NOPE].reshape(KV_RANK, N_HEADS * QK_NOPE)
    wvt = jnp.transpose(wkv[..., QK_NOPE:], (1, 2, 0)).reshape(N_HEADS * V_HEAD, KV_RANK)
    return w_in2.astype(BF16), wq.astype(BF16), wk.astype(BF16), wvt.astype(BF16)


def kernel(x, c, ctx, c_ctx, w_mod, b_mod, norm1, norm2, w_in, q_norm, w_uq, kv_norm, w_ukv, w_pool,
           pool_scale, w_out_even, w_out_odd, w_mlp1, w_mlp2, final_norm):
    b, n, d = x.shape
    n_ctx = ctx.shape[1]
    assert n == DFT_N1 * DFT_N2 and d == D_MODEL and n % GRID_W == 0
    assert n % 512 == 0 and n % n_ctx == 0

    cos2, sin2 = _rope_tables(n)
    dft = _dft_tables(n_ctx)
    w1 = w_mlp1.astype(BF16)
    w2 = w_mlp2.astype(BF16)

    cc = jnp.concatenate([c, c_ctx[None], jnp.zeros((8 - b - 1, d), F32)], axis=0)
    mods = _modulation(cc, w_mod, b_mod)

    updates_ctx = [False] * DEPTH
    for l in reversed(range(DEPTH - 1)):
        updates_ctx[l] = (l + 1) % 2 == 0 or updates_ctx[l + 1]

    for l in range(DEPTH):
        even = l % 2 == 0
        i = l // 2
        mx = [mods[l, :b, k * d:(k + 1) * d].reshape(b, 1, d) for k in range(6)]
        mc = [jnp.broadcast_to(mods[l, b, k * d:(k + 1) * d].reshape(1, 1, d), (b, 1, d)) for k in range(6)]
        ctx_update = updates_ctx[l]

        if even:
            w_in2, wq, wk, wvt = _even_weights(w_in[i], w_uq[i], w_ukv[i])
            w_out = w_out_even[i].astype(BF16)
            wp = w_pool[i].astype(BF16)
            ux, q, *keys_x = _evenproj(x, norm1[l], mx[0], mx[1], w_in2, q_norm[i], wq, kv_norm[i], wk, wvt,
                                       cos2, sin2, rope=True)
            uc, cq, *keys_c = _evenproj(ctx, norm1[l], mc[0], mc[1], w_in2, q_norm[i], wq, kv_norm[i], wk, wvt,
                                        cos2, sin2, rope=False)
            attn_x = _attention(q, [keys_x, keys_c])
            pool_x = _pool(ux, wp, pool_scale[i])
            x = _rowmm([attn_x, pool_x], [w_out[:MLA_OUT], w_out[MLA_OUT:]], F32, res=x, gate=mx[2])
            if ctx_update:
                attn_c = _attention(cq, [keys_c])
                pool_c = _pool(uc, wp, pool_scale[i])
                ctx = _rowmm([attn_c, pool_c], [w_out[:MLA_OUT], w_out[MLA_OUT:]], F32, res=ctx, gate=mc[2])
        else:
            w_out = w_out_odd[i].astype(BF16)
            if ctx_update:
                hc = _normmod(ctx, norm1[l], mc[0], mc[1])
            y1 = _dft_stage1(x, norm1[l], mx[0], mx[1], dft['a_kron'])
            fx = _dft_stage2(y1, dft['m'], dft['cc'], dft['sc'])
            x = _rowmm_strided(fx, w_out, x, mx[2])
            if ctx_update:
                fc = _dft_small(hc, dft['m_ctx'], dft['cc'], dft['sc'])
                ctx = _rowmm([fc], [w_out], F32, res=ctx, gate=mc[2])
        x = _mlp(x, norm2[l], mx[3], mx[4], mx[5], w1, w2, l, final_norm if l == DEPTH - 1 else None)
        if ctx_update:
            ctx = _mlp(ctx.reshape(1, b * n_ctx, d), norm2[l], mc[3][:1], mc[4][:1], mc[5][:1],
                       w1, w2, l).reshape(b, n_ctx, d)
    return x
```

```python
import functools
import math

import jax
import jax.numpy as jnp
from jax import lax
from jax.experimental import pallas as pl
from jax.experimental.pallas import tpu as pltpu

D_MODEL = 2048
DEPTH = 4
GRID_W = 64
N_HEADS = 8
Q_RANK = 512
KV_RANK = 256
QK_NOPE = 128
QK_ROPE = 64
V_HEAD = 128
QK_HEAD = QK_NOPE + QK_ROPE
ATTN_SCALE = QK_HEAD ** -0.5
Q_SCALE = ATTN_SCALE * math.log2(math.e)
MLA_OUT = N_HEADS * V_HEAD
ROPE_AXIS = QK_ROPE // 2
ROPE_FREQS = ROPE_AXIS // 2
ROPE_THETA = 10000.0
POOL_WIDTH = D_MODEL // 2
POOL_WINDOWS = (2, 4, 8, 16)
POOL_GROUP = POOL_WIDTH // len(POOL_WINDOWS)
POOL_HALO = 8
FOURIER_GROUPS = 4
FOURIER_GROUP = D_MODEL // FOURIER_GROUPS
D_FF = 4 * D_MODEL
EPS = 1e-6

P_POOL = 0
P_QLAT = POOL_WIDTH
P_KV = POOL_WIDTH + Q_RANK
P_WIDTH = POOL_WIDTH + Q_RANK + KV_RANK + 2 * QK_ROPE

DFT_N1 = 64
DFT_N2 = 128
DFT_ROWS = 16
DFT2_K1 = 8
DFT_STRIDE_BLOCK = 8
DFT_COLS = 512

NORM_ROWS = 16
MLP_TM = 512
MLP_TF = 1024
MLP_ROWS = 256
ATTN_TQ = 1024
ATTN_TK_FIXED = 4096
ATTN_TK_ONLINE = 256
ATTN_BOUND_SLACK = 1.02
ATTN_MIN_DENOM = 2.0 ** -60

V7X_VMEM_LIMIT = 56 * 1024 * 1024

F32 = jnp.float32
BF16 = jnp.bfloat16


def _params(*sem):
    return pltpu.CompilerParams(dimension_semantics=sem, vmem_limit_bytes=V7X_VMEM_LIMIT)


def _resident(shape):
    nd = len(shape)
    return pl.BlockSpec(shape, lambda *_: (0,) * nd, pipeline_mode=pl.Buffered(1))


def _rms(x, g):
    return x * lax.rsqrt(jnp.mean(x * x, axis=-1, keepdims=True) + EPS) * g


def _split_bf16(a):
    hi = a.astype(BF16)
    return hi, (a - hi.astype(F32)).astype(BF16)


def _mod_kernel(c_ref, w_ref, b_ref, o_ref):
    c = c_ref[...]
    s = c * (1.0 / (1.0 + jnp.exp(-c)))
    s_hi, s_lo = _split_bf16(s)
    w_hi, w_lo = _split_bf16(w_ref[0])
    rows = s.shape[0]
    top = jnp.dot(jnp.concatenate([s_hi, s_lo], axis=0), w_hi, preferred_element_type=F32)
    o_ref[0] = top[:rows] + top[rows:] + jnp.dot(s_hi, w_lo, preferred_element_type=F32) + b_ref[0]


def _modulation(cc, w_mod, b_mod):
    rows = cc.shape[0]
    tn = 1536
    n6 = 6 * D_MODEL
    return pl.pallas_call(
        _mod_kernel,
        grid=(DEPTH, n6 // tn),
        in_specs=[
            pl.BlockSpec((rows, D_MODEL), lambda l, j: (0, 0)),
            pl.BlockSpec((1, D_MODEL, tn), lambda l, j: (l, 0, j)),
            pl.BlockSpec((1, 1, tn), lambda l, j: (l, 0, j)),
        ],
        out_specs=pl.BlockSpec((1, rows, tn), lambda l, j: (l, 0, j)),
        out_shape=jax.ShapeDtypeStruct((DEPTH, rows, n6), F32),
        compiler_params=_params("parallel", "parallel"),
        name="modulation",
    )(cc, w_mod, b_mod.reshape(DEPTH, 1, n6))


def _normmod_rows(x_ref, g_ref, sh_ref, sc_ref, dst_ref, start=0, stop=None):
    gs = g_ref[...] * (1.0 + sc_ref[0])
    sh = sh_ref[0]
    for r in range(start, x_ref.shape[1] if stop is None else stop, NORM_ROWS):
        x = x_ref[0, r:r + NORM_ROWS, :]
        inv = lax.rsqrt(jnp.mean(x * x, axis=-1, keepdims=True) + EPS)
        dst_ref[r:r + NORM_ROWS, :] = (x * inv * gs + sh).astype(dst_ref.dtype)


def _normmod_kernel(x_ref, g_ref, sh_ref, sc_ref, o_ref):
    _normmod_rows(x_ref, g_ref, sh_ref, sc_ref, o_ref.at[0])


def _normmod(x, g, sh, sc, out_dtype=BF16):
    b, n, d = x.shape
    tm = min(512, n)
    vec = pl.BlockSpec((1, 1, d), lambda bi, i: (bi, 0, 0))
    return pl.pallas_call(
        _normmod_kernel,
        grid=(b, n // tm),
        in_specs=[pl.BlockSpec((1, tm, d), lambda bi, i: (bi, i, 0)),
                  pl.BlockSpec((1, d), lambda bi, i: (0, 0)), vec, vec],
        out_specs=pl.BlockSpec((1, tm, d), lambda bi, i: (bi, i, 0)),
        out_shape=jax.ShapeDtypeStruct((b, n, d), out_dtype),
        compiler_params=_params("parallel", "parallel"),
        name="normmod",
    )(x, g.reshape(1, d), sh, sc)


def _rope_cols(t, cos2, sin2, rope):
    if not rope:
        return t
    return t * cos2 + pltpu.roll(t, QK_ROPE, 1) * sin2


def _evenproj_kernel(x_ref, g_ref, sh_ref, sc_ref, w_ref, qg_ref, wq_ref, kvg_ref, wk_ref, wvt_ref, c_ref, s_ref,
                     u_ref, q_ref, k_ref, vt_ref, kmax_ref, h_sc, *, rope):
    @pl.when(pl.program_id(1) == 0)
    def _():
        kmax_ref[...] = jnp.zeros(kmax_ref.shape, F32)

    width = QK_NOPE + 2 * QK_ROPE
    for r in range(0, x_ref.shape[1], MLP_ROWS):
        rows = slice(r, r + MLP_ROWS)
        _normmod_rows(x_ref, g_ref, sh_ref, sc_ref, h_sc, r, r + MLP_ROWS)
        p = jnp.dot(h_sc[rows, :], w_ref[...], preferred_element_type=F32)
        u_ref[0, rows, :] = p[:, P_POOL:P_POOL + POOL_WIDTH]
        cos2, sin2 = c_ref[rows, :], s_ref[rows, :]
        z = _rms(p[:, P_QLAT:P_QLAT + Q_RANK], qg_ref[...]).astype(BF16)
        acc = jnp.dot(z, wq_ref[...], preferred_element_type=F32)
        for h in range(N_HEADS):
            q_ref[0, h, rows, :QK_NOPE] = (acc[:, h * width:h * width + QK_NOPE] * Q_SCALE).astype(q_ref.dtype)
            t = _rope_cols(acc[:, h * width + QK_NOPE:(h + 1) * width], cos2, sin2, rope)
            q_ref[0, h, rows, QK_NOPE:] = (t[:, :QK_ROPE] * Q_SCALE).astype(q_ref.dtype)
        zk = _rms(p[:, P_KV:P_KV + KV_RANK], kvg_ref[...]).astype(BF16)
        kr = _rope_cols(p[:, P_KV + KV_RANK:], cos2, sin2, rope)[:, :QK_ROPE].astype(k_ref.dtype)
        kn = jnp.dot(zk, wk_ref[...], preferred_element_type=F32).astype(k_ref.dtype)
        vt = lax.dot_general(wvt_ref[...], zk, (((1,), (1,)), ((), ())),
                             preferred_element_type=F32).astype(vt_ref.dtype)
        kn2 = kn.astype(F32) * kn.astype(F32)
        kr2 = jnp.sum(kr.astype(F32) * kr.astype(F32), axis=1, keepdims=True)
        for h in range(N_HEADS):
            k_ref[0, h, rows, :QK_NOPE] = kn[:, h * QK_NOPE:(h + 1) * QK_NOPE]
            k_ref[0, h, rows, QK_NOPE:] = kr
            vt_ref[0, h, :, rows] = vt[h * V_HEAD:(h + 1) * V_HEAD, :]
            norm2 = jnp.sum(kn2[:, h * QK_NOPE:(h + 1) * QK_NOPE], axis=1, keepdims=True) + kr2
            kmax_ref[0, h] = jnp.maximum(kmax_ref[0, h], jnp.max(norm2, axis=0, keepdims=True))


def _evenproj(x, g, sh, sc, w_in2, q_norm, wq, kv_norm, wk, wvt, cos2, sin2, rope):
    b, n, d = x.shape
    tm = min(512, n)
    vec = pl.BlockSpec((1, 1, d), lambda bi, i: (bi, 0, 0))
    tab = pl.BlockSpec((tm, 2 * QK_ROPE), lambda bi, i: (i, 0))
    heads = pl.BlockSpec((1, N_HEADS, tm, QK_HEAD), lambda bi, i: (bi, 0, i, 0))
    return pl.pallas_call(
        functools.partial(_evenproj_kernel, rope=rope),
        grid=(b, n // tm),
        in_specs=[pl.BlockSpec((1, tm, d), lambda bi, i: (bi, i, 0)),
                  pl.BlockSpec((1, d), lambda bi, i: (0, 0)), vec, vec, _resident(w_in2.shape),
                  pl.BlockSpec((1, Q_RANK), lambda bi, i: (0, 0)), _resident(wq.shape),
                  pl.BlockSpec((1, KV_RANK), lambda bi, i: (0, 0)), _resident(wk.shape), _resident(wvt.shape),
                  tab, tab],
        out_specs=[pl.BlockSpec((1, tm, POOL_WIDTH), lambda bi, i: (bi, i, 0)), heads, heads,
                   pl.BlockSpec((1, N_HEADS, V_HEAD, tm), lambda bi, i: (bi, 0, 0, i)),
                   pl.BlockSpec((1, N_HEADS, 8, 128), lambda bi, i: (bi, 0, 0, 0))],
        out_shape=[jax.ShapeDtypeStruct((b, n, POOL_WIDTH), F32),
                   jax.ShapeDtypeStruct((b, N_HEADS, n, QK_HEAD), BF16),
                   jax.ShapeDtypeStruct((b, N_HEADS, n, QK_HEAD), BF16),
                   jax.ShapeDtypeStruct((b, N_HEADS, V_HEAD, n), BF16),
                   jax.ShapeDtypeStruct((b, N_HEADS, 8, 128), F32)],
        scratch_shapes=[pltpu.VMEM((tm, d), BF16)],
        compiler_params=_params("parallel", "arbitrary"),
        name="evenproj",
    )(x, g.reshape(1, d), sh, sc, w_in2, q_norm.reshape(1, Q_RANK), wq, kv_norm.reshape(1, KV_RANK), wk, wvt,
      cos2, sin2)


def _rowmm_kernel(*refs, n_in, has_res):
    a_refs, w_refs = refs[:n_in], refs[n_in:2 * n_in]
    o_ref = refs[-1]
    acc = None
    for a, w in zip(a_refs, w_refs):
        d = jnp.dot(a[0], w[...], preferred_element_type=F32)
        acc = d if acc is None else acc + d
    if has_res:
        res_ref, gate_ref = refs[2 * n_in], refs[2 * n_in + 1]
        acc = res_ref[0] + gate_ref[0] * acc
    o_ref[0] = acc.astype(o_ref.dtype)


def _rowmm(a_list, w_list, out_dtype, res=None, gate=None, tm=512):
    b, n, _ = a_list[0].shape
    nout = w_list[0].shape[1]
    tm = min(tm, n)
    in_specs = [pl.BlockSpec((1, tm, a.shape[2]), lambda bi, i: (bi, i, 0)) for a in a_list]
    in_specs += [_resident(w.shape) for w in w_list]
    args = list(a_list) + list(w_list)
    if res is not None:
        in_specs += [pl.BlockSpec((1, tm, nout), lambda bi, i: (bi, i, 0)),
                     pl.BlockSpec((1, 1, nout), lambda bi, i: (bi, 0, 0))]
        args += [res, gate]
    return pl.pallas_call(
        functools.partial(_rowmm_kernel, n_in=len(a_list), has_res=res is not None),
        grid=(b, n // tm),
        in_specs=in_specs,
        out_specs=pl.BlockSpec((1, tm, nout), lambda bi, i: (bi, i, 0)),
        out_shape=jax.ShapeDtypeStruct((b, n, nout), out_dtype),
        compiler_params=_params("parallel", "parallel"),
        name="rowmm",
    )(*args)


def _rowmm_strided_kernel(f_ref, w_ref, res_ref, gate_ref, o_ref):
    gate = gate_ref[0]
    d = w_ref.shape[0]
    n2 = f_ref.shape[1]
    steps = res_ref.shape[2]
    rows = jnp.concatenate([f_ref[0, :, s * d:(s + 1) * d] for s in range(steps)], axis=0)
    for c in range(0, d, DFT_COLS):
        cols = slice(c, c + DFT_COLS)
        y = jnp.dot(rows, w_ref[:, cols], preferred_element_type=F32)
        for s in range(steps):
            o_ref[0, :, s, cols] = res_ref[0, :, s, cols] + gate[:, cols] * y[s * n2:(s + 1) * n2, :]


def _rowmm_strided(f, w, res, gate):
    b, n, d = res.shape
    res4 = res.reshape(b, DFT_N2, DFT_N1, d)
    blk = pl.BlockSpec((1, DFT_N2, DFT_STRIDE_BLOCK, d), lambda bi, i: (bi, 0, i, 0))
    out = pl.pallas_call(
        _rowmm_strided_kernel,
        grid=(b, DFT_N1 // DFT_STRIDE_BLOCK),
        in_specs=[pl.BlockSpec((1, DFT_N2, DFT_STRIDE_BLOCK * d), lambda bi, i: (bi, 0, i)),
                  _resident(w.shape), blk,
                  pl.BlockSpec((1, 1, d), lambda bi, i: (bi, 0, 0))],
        out_specs=blk,
        out_shape=jax.ShapeDtypeStruct(res4.shape, F32),
        compiler_params=_params("parallel", "parallel"),
        name="rowmm_strided",
    )(f, w, res4, gate)
    return out.reshape(b, n, d)


def _attn_kernel(q_ref, *refs, n_sources):
    o_ref = refs[-1]
    sources = [refs[3 * s:3 * s + 3] for s in range(n_sources)]
    q = q_ref[0, 0]
    tq = q.shape[0]
    nt = (((1,), (1,)), ((), ()))

    def key_blocks(tk):
        blocks = []
        for k_ref, vt_ref, _ in sources:
            n_keys = k_ref.shape[2]
            size = min(tk, n_keys)
            assert n_keys % size == 0
            blocks += [(k_ref, vt_ref, s, size) for s in range(0, n_keys, size)]
        return blocks

    def scores(blk):
        k_ref, _, s, size = blk
        return lax.dot_general(k_ref[0, 0, s:s + size, :], q, nt, preferred_element_type=F32)

    def weighted_values(blk, p):
        _, vt_ref, s, size = blk
        return jnp.dot(vt_ref[0, 0, :, s:s + size], p.astype(BF16), preferred_element_type=F32)

    def store(acc, l):
        o_ref[0] = jnp.transpose(acc / l).astype(o_ref.dtype)

    qf = q.astype(F32)
    qn2 = lax.dot_general(jnp.ones((8, q.shape[1]), BF16), (qf * qf).astype(BF16), nt,
                          preferred_element_type=F32)[:1]
    kmax = functools.reduce(jnp.maximum, [kmax_ref[0, 0, :1, :1] for _, _, kmax_ref in sources])
    ref = jnp.sqrt(qn2 * kmax) * ATTN_BOUND_SLACK
    l = jnp.zeros((1, tq), F32)
    acc = jnp.zeros((V_HEAD, tq), F32)
    blocks = key_blocks(ATTN_TK_FIXED)
    st_next = scores(blocks[0])
    for j, blk in enumerate(blocks):
        st = st_next
        if j + 1 < len(blocks):
            st_next = scores(blocks[j + 1])
        p = jnp.exp2(st - ref)
        l = l + jnp.sum(p, axis=0, keepdims=True)
        acc = acc + weighted_values(blk, p)
    store(acc, l)

    @pl.when(jnp.logical_not(jnp.min(l) >= ATTN_MIN_DENOM))
    def _():
        blocks = key_blocks(ATTN_TK_ONLINE)
        m = jnp.full((1, tq), -jnp.inf, F32)
        l = jnp.zeros((1, tq), F32)
        acc = jnp.zeros((V_HEAD, tq), F32)
        st_next = scores(blocks[0])
        for j, blk in enumerate(blocks):
            st = st_next
            if j + 1 < len(blocks):
                st_next = scores(blocks[j + 1])
            m_new = jnp.maximum(m, jnp.max(st, axis=0, keepdims=True))
            alpha = jnp.exp2(m - m_new)
            p = jnp.exp2(st - m_new)
            l = alpha * l + jnp.sum(p, axis=0, keepdims=True)
            acc = alpha * acc + weighted_values(blk, p)
            m = m_new
        store(acc, l)


def _attention(q, sources):
    b, h, n, _ = q.shape
    tq = min(ATTN_TQ, n)
    in_specs = [pl.BlockSpec((1, 1, tq, QK_HEAD), lambda bi, hi, i: (bi, hi, i, 0))]
    args = [q]
    for k, vt, kmax in sources:
        m = k.shape[2]
        in_specs += [pl.BlockSpec((1, 1, m, QK_HEAD), lambda bi, hi, i: (bi, hi, 0, 0)),
                     pl.BlockSpec((1, 1, V_HEAD, m), lambda bi, hi, i: (bi, hi, 0, 0)),
                     pl.BlockSpec((1, 1, 8, 128), lambda bi, hi, i: (bi, hi, 0, 0))]
        args += [k, vt, kmax]
    return pl.pallas_call(
        functools.partial(_attn_kernel, n_sources=len(sources)),
        grid=(b, h, n // tq),
        in_specs=in_specs,
        out_specs=pl.BlockSpec((1, tq, V_HEAD), lambda bi, hi, i: (bi, i, hi)),
        out_shape=jax.ShapeDtypeStruct((b, n, h * V_HEAD), BF16),
        compiler_params=_params("parallel", "parallel", "parallel"),
        name="attention",
    )(*args)


def _pool_kernel(u_ref, prev_ref, next_ref, w_ref, s_ref, o_ref, ext_sc, *, n, tm):
    i = pl.program_id(1)
    keep_prev = jnp.where(i > 0, 1.0, 0.0)
    keep_next = jnp.where(i < pl.num_programs(1) - 1, 1.0, 0.0)
    ext_sc[0:POOL_HALO] = prev_ref[0] * keep_prev
    ext_sc[POOL_HALO:POOL_HALO + tm] = u_ref[0]
    ext_sc[POOL_HALO + tm:] = next_ref[0] * keep_next
    t = i * tm + lax.broadcasted_iota(jnp.int32, (tm, 1), 0)
    rows = tm + 2 * POOL_HALO
    for g, win in enumerate(POOL_WINDOWS):
        half = win // 2
        cols = slice(g * POOL_GROUP, (g + 1) * POOL_GROUP)
        a = ext_sc[:, cols]
        span = 1
        while span < win:
            a = a + pltpu.roll(a, rows - span, 0)
            span *= 2
        tot = a[POOL_HALO - half:POOL_HALO - half + tm]
        cnt = (jnp.minimum(t + half, n) - jnp.maximum(t - half, 0)).astype(F32)
        pooled = (tot / cnt - ext_sc[POOL_HALO:POOL_HALO + tm, cols]).astype(BF16)
        y = jnp.dot(pooled, w_ref[g], preferred_element_type=F32) * s_ref[:, cols]
        o_ref[0, :, cols] = y.astype(o_ref.dtype)


def _pool(p, w_pool, pool_scale):
    b, n, _ = p.shape
    tm = min(512, n)
    hb = tm // POOL_HALO
    last_hb = n // POOL_HALO - 1
    return pl.pallas_call(
        functools.partial(_pool_kernel, n=n, tm=tm),
        grid=(b, n // tm),
        in_specs=[pl.BlockSpec((1, tm, POOL_WIDTH), lambda bi, i: (bi, i, 0)),
                  pl.BlockSpec((1, POOL_HALO, POOL_WIDTH), lambda bi, i: (bi, jnp.maximum(i * hb - 1, 0), 0)),
                  pl.BlockSpec((1, POOL_HALO, POOL_WIDTH),
                               lambda bi, i: (bi, jnp.minimum((i + 1) * hb, last_hb), 0)),
                  _resident(w_pool.shape),
                  pl.BlockSpec((1, POOL_WIDTH), lambda bi, i: (0, 0))],
        out_specs=pl.BlockSpec((1, tm, POOL_WIDTH), lambda bi, i: (bi, i, 0)),
        out_shape=jax.ShapeDtypeStruct((b, n, POOL_WIDTH), BF16),
        scratch_shapes=[pltpu.VMEM((tm + 2 * POOL_HALO, POOL_WIDTH), F32)],
        compiler_params=_params("parallel", "parallel"),
        name="pool",
    )(p, p, p, w_pool, pool_scale.reshape(1, POOL_WIDTH))


def _dft1_kernel(x_ref, g_ref, sh_ref, sc_ref, a_ref, o_ref, h_sc):
    gs = g_ref[...] * (1.0 + sc_ref[0])
    sh = sh_ref[0]
    for t1 in range(DFT_N1):
        x = x_ref[0, t1]
        inv = lax.rsqrt(jnp.mean(x * x, axis=-1, keepdims=True) + EPS)
        h_sc[t1 * DFT_ROWS:(t1 + 1) * DFT_ROWS, :] = (x * inv * gs + sh).astype(h_sc.dtype)
    for c in range(0, D_MODEL, DFT_COLS):
        y = jnp.dot(a_ref[...], h_sc[:, c:c + DFT_COLS], preferred_element_type=F32)
        o_ref[0, :, :, :, c:c + DFT_COLS] = y.astype(o_ref.dtype).reshape(2, DFT_N1, DFT_ROWS, DFT_COLS)


def _dft_stage1(x, g, sh, sc, a_kron):
    b, n, d = x.shape
    vec = pl.BlockSpec((1, 1, d), lambda bi, r: (bi, 0, 0))
    return pl.pallas_call(
        _dft1_kernel,
        grid=(b, DFT_N2 // DFT_ROWS),
        in_specs=[pl.BlockSpec((1, DFT_N1, DFT_ROWS, d), lambda bi, r: (bi, 0, r, 0)),
                  pl.BlockSpec((1, d), lambda bi, r: (0, 0)), vec, vec,
                  _resident(a_kron.shape)],
        out_specs=pl.BlockSpec((1, 2, DFT_N1, DFT_ROWS, d), lambda bi, r: (bi, 0, 0, r, 0)),
        out_shape=jax.ShapeDtypeStruct((b, 2, DFT_N1, DFT_N2, d), BF16),
        scratch_shapes=[pltpu.VMEM((DFT_N1 * DFT_ROWS, d), BF16)],
        compiler_params=_params("parallel", "parallel"),
        name="dft_stage1",
    )(x.reshape(b, DFT_N1, DFT_N2, d), g.reshape(1, d), sh, sc, a_kron)


def _dft2_rows(y, m, cc_ref, sc_ref, rows_out, seq_len, store):
    pq = jnp.dot(m, y, preferred_element_type=F32).astype(BF16)
    ortho = 1.0 / math.sqrt(seq_len * FOURIER_GROUP)
    for g in range(FOURIER_GROUPS):
        cols = slice(g * FOURIER_GROUP, (g + 1) * FOURIER_GROUP)
        f = (jnp.dot(pq[:rows_out, cols], cc_ref[...], preferred_element_type=F32)
             - jnp.dot(pq[rows_out:, cols], sc_ref[...], preferred_element_type=F32))
        store(cols, f * ortho)


def _dft2_kernel(y_ref, m_ref, cc_ref, sc_ref, o_ref):
    for s in range(y_ref.shape[2]):
        y = jnp.concatenate([y_ref[0, 0, s], y_ref[0, 1, s]], axis=0)

        def store(cols, f, s=s):
            o_ref[0, :, s * D_MODEL + cols.start:s * D_MODEL + cols.stop] = f.astype(o_ref.dtype)

        _dft2_rows(y, m_ref[s], cc_ref, sc_ref, DFT_N2, DFT_N1 * DFT_N2, store)


def _dft_small_kernel(h_ref, m_ref, cc_ref, sc_ref, o_ref):
    def store(cols, f):
        o_ref[0, :, cols] = f.astype(o_ref.dtype)

    n = h_ref.shape[1]
    _dft2_rows(h_ref[0], m_ref[0], cc_ref, sc_ref, n, n, store)


def _dft_stage2(y, m, cc, sc):
    b = y.shape[0]
    return pl.pallas_call(
        _dft2_kernel,
        grid=(b, DFT_N1 // DFT2_K1),
        in_specs=[pl.BlockSpec((1, 2, DFT2_K1, DFT_N2, D_MODEL), lambda bi, k: (bi, 0, k, 0, 0)),
                  pl.BlockSpec((DFT2_K1, 2 * DFT_N2, 2 * DFT_N2), lambda bi, k: (k, 0, 0)),
                  _resident(cc.shape), _resident(sc.shape)],
        out_specs=pl.BlockSpec((1, DFT_N2, DFT2_K1 * D_MODEL), lambda bi, k: (bi, 0, k)),
        out_shape=jax.ShapeDtypeStruct((b, DFT_N2, DFT_N1 * D_MODEL), BF16),
        compiler_params=_params("parallel", "parallel"),
        name="dft_stage2",
    )(y, m, cc, sc)


def _dft_small(h, m, cc, sc):
    b, n, d = h.shape
    return pl.pallas_call(
        _dft_small_kernel,
        grid=(b,),
        in_specs=[pl.BlockSpec((1, n, d), lambda bi: (bi, 0, 0)),
                  _resident(m.shape), _resident(cc.shape), _resident(sc.shape)],
        out_specs=pl.BlockSpec((1, n, d), lambda bi: (bi, 0, 0)),
        out_shape=jax.ShapeDtypeStruct((b, n, d), BF16),
        compiler_params=_params("parallel"),
        name="dft_small",
    )(h, m, cc, sc)


def _cos_sin(num, den):
    ang = (num % den).astype(F32) * (2.0 * math.pi / den)
    return jnp.cos(ang), jnp.sin(ang)


def _dft_tables(n_ctx):
    i1 = jnp.arange(DFT_N1, dtype=jnp.int32)
    c1, s1 = _cos_sin(i1[:, None] * i1[None, :], DFT_N1)
    a = jnp.concatenate([c1, -s1], axis=0)
    eye = jnp.eye(DFT_ROWS, dtype=F32)
    a_kron = jnp.einsum('pt,jk->pjtk', a, eye).reshape(2 * DFT_N1 * DFT_ROWS, DFT_N1 * DFT_ROWS)
    n = DFT_N1 * DFT_N2
    t2 = jnp.arange(DFT_N2, dtype=jnp.int32)
    k = i1[:, None, None] + DFT_N1 * t2[None, :, None]
    ck, sk = _cos_sin(k * t2[None, None, :], n)
    m = jnp.concatenate([jnp.concatenate([ck, sk], axis=2),
                         jnp.concatenate([sk, -ck], axis=2)], axis=1)
    ic = jnp.arange(FOURIER_GROUP, dtype=jnp.int32)
    cc, sc = _cos_sin(ic[:, None] * ic[None, :], FOURIER_GROUP)
    il = jnp.arange(n_ctx, dtype=jnp.int32)
    cl, sl = _cos_sin(il[:, None] * il[None, :], n_ctx)
    m_ctx = jnp.concatenate([cl, sl], axis=0)[None]

    return dict(a_kron=a_kron.astype(BF16), m=m.astype(BF16), m_ctx=m_ctx.astype(BF16),
                cc=cc.astype(BF16), sc=sc.astype(BF16))


def _mlp_kernel(x_ref, g_ref, sh_ref, sc_ref, gate_ref, w1_ref, w2_ref, *rest, final):
    o_ref, h_sc, acc_sc = rest[-3:]
    j = pl.program_id(2)
    last = pl.num_programs(2) - 1
    tm = x_ref.shape[1]
    chunks = [slice(r, r + MLP_ROWS) for r in range(0, tm, MLP_ROWS)]

    def hidden(rows):
        a = jnp.maximum(jnp.dot(h_sc[rows, :], w1_ref[...], preferred_element_type=F32), 0.0)
        return (a * a).astype(BF16)

    @pl.when(j == 0)
    def _():
        parts = []
        for rows in chunks:
            _normmod_rows(x_ref, g_ref, sh_ref, sc_ref, h_sc, rows.start, rows.stop)
            parts.append(hidden(rows))
        acc_sc[...] = jnp.dot(jnp.concatenate(parts, axis=0), w2_ref[...], preferred_element_type=F32)

    @pl.when(jnp.logical_and(j > 0, j < last))
    def _():
        acc_sc[...] += jnp.dot(hidden(slice(None)), w2_ref[...], preferred_element_type=F32)

    @pl.when(j == last)
    def _():
        gate = gate_ref[0]
        a = hidden(slice(None))
        for rows in chunks:
            upd = acc_sc[rows, :] + jnp.dot(a[rows, :], w2_ref[...], preferred_element_type=F32)
            for r in range(0, MLP_ROWS, NORM_ROWS):
                y = x_ref[0, rows.start + r:rows.start + r + NORM_ROWS, :] + gate * upd[r:r + NORM_ROWS, :]
                if final:
                    y = _rms(y, rest[0][...])
                o_ref[0, rows.start + r:rows.start + r + NORM_ROWS, :] = y


def _mlp(x, g, sh, sc, gate, w1, w2, layer, final_g=None):
    b, n, d = x.shape
    tm = min(MLP_TM, n)
    tf = MLP_TF
    vec = pl.BlockSpec((1, 1, d), lambda bi, i, j: (bi, 0, 0))
    row = pl.BlockSpec((1, d), lambda bi, i, j: (0, 0))
    in_specs = [pl.BlockSpec((1, tm, d), lambda bi, i, j: (bi, i, 0)), row, vec, vec, vec,
                pl.BlockSpec((None, d, tf), lambda bi, i, j: (layer, 0, j)),
                pl.BlockSpec((None, tf, d), lambda bi, i, j: (layer, j, 0))]
    args = [x, g.reshape(1, d), sh, sc, gate, w1, w2]
    if final_g is not None:
        in_specs.append(row)
        args.append(final_g.reshape(1, d))
    return pl.pallas_call(
        functools.partial(_mlp_kernel, final=final_g is not None),
        grid=(b, n // tm, D_FF // tf),
        in_specs=in_specs,
        out_specs=pl.BlockSpec((1, tm, d), lambda bi, i, j: (bi, i, 0)),
        out_shape=jax.ShapeDtypeStruct((b, n, d), F32),
        scratch_shapes=[pltpu.VMEM((tm, d), BF16), pltpu.VMEM((tm, d), F32)],
        compiler_params=_params("parallel", "parallel", "arbitrary"),
        name="mlp",
    )(*args)


def _rotate_half_axial(x):
    xr = x.reshape(x.shape[:-1] + (2, 2, ROPE_FREQS))
    return jnp.concatenate([-xr[..., 1:, :], xr[..., :1, :]], axis=-2).reshape(x.shape)


def _rope_tables(n):
    rows = n // GRID_W
    r = jnp.broadcast_to(jnp.arange(rows, dtype=F32)[:, None], (rows, GRID_W)).reshape(n)
    col = jnp.broadcast_to(jnp.arange(GRID_W, dtype=F32)[None, :], (rows, GRID_W)).reshape(n)
    inv = ROPE_THETA ** (-2.0 * jnp.arange(ROPE_FREQS, dtype=F32) / ROPE_AXIS)
    ang = jnp.stack([r[:, None] * inv, col[:, None] * inv], axis=1)
    ang = jnp.broadcast_to(ang[:, :, None, :], (n, 2, 2, ROPE_FREQS)).reshape(n, QK_ROPE)
    zeros = jnp.zeros((n, QK_ROPE), F32)
    return (jnp.concatenate([jnp.cos(ang), zeros], axis=1), jnp.concatenate([jnp.sin(ang), zeros], axis=1))


def _even_weights(w_in, w_uq, w_ukv):
    w_kr = w_in[:, Q_RANK + KV_RANK:Q_RANK + KV_RANK + QK_ROPE]
    w_in2 = jnp.concatenate([w_in[:, Q_RANK + KV_RANK + QK_ROPE:], w_in[:, :Q_RANK],
                             w_in[:, Q_RANK:Q_RANK + KV_RANK], w_kr, _rotate_half_axial(w_kr)], axis=1)
    wq = w_uq.reshape(Q_RANK, N_HEADS, QK_HEAD)
    wq = jnp.concatenate([wq, _rotate_half_axial(wq[..., QK_NOPE:])], axis=-1)
    wq = wq.reshape(Q_RANK, N_HEADS * (QK_HEAD + QK_ROPE))
    wkv = w_ukv.reshape(KV_RANK, N_HEADS, QK_NOPE + V_HEAD)
    wk = wkv[..., :QK_NOPE].reshape(KV_RANK, N_HEADS * QK_NOPE)
    wvt = jnp.transpose(wkv[..., QK_NOPE:], (1, 2, 0)).reshape(N_HEADS * V_HEAD, KV_RANK)
    return w_in2.astype(BF16), wq.astype(BF16), wk.astype(BF16), wvt.astype(BF16)


def kernel(x, c, ctx, c_ctx, w_mod, b_mod, norm1, norm2, w_in, q_norm, w_uq, kv_norm, w_ukv, w_pool,
           pool_scale, w_out_even, w_out_odd, w_mlp1, w_mlp2, final_norm):
    b, n, d = x.shape
    n_ctx = ctx.shape[1]
    assert n == DFT_N1 * DFT_N2 and d == D_MODEL and n % GRID_W == 0
    assert n % 512 == 0 and n % n_ctx == 0

    cos2, sin2 = _rope_tables(n)
    dft = _dft_tables(n_ctx)
    w1 = w_mlp1.astype(BF16)
    w2 = w_mlp2.astype(BF16)

    cc = jnp.concatenate([c, c_ctx[None], jnp.zeros((8 - b - 1, d), F32)], axis=0)
    mods = _modulation(cc, w_mod, b_mod)

    updates_ctx = [False] * DEPTH
    for l in reversed(range(DEPTH - 1)):
        updates_ctx[l] = (l + 1) % 2 == 0 or updates_ctx[l + 1]

    for l in range(DEPTH):
        even = l % 2 == 0
        i = l // 2
        mx = [mods[l, :b, k * d:(k + 1) * d].reshape(b, 1, d) for k in range(6)]
        mc = [jnp.broadcast_to(mods[l, b, k * d:(k + 1) * d].reshape(1, 1, d), (b, 1, d)) for k in range(6)]
        ctx_update = updates_ctx[l]

        if even:
            w_in2, wq, wk, wvt = _even_weights(w_in[i], w_uq[i], w_ukv[i])
            w_out = w_out_even[i].astype(BF16)
            wp = w_pool[i].astype(BF16)
            ux, q, *keys_x = _evenproj(x, norm1[l], mx[0], mx[1], w_in2, q_norm[i], wq, kv_norm[i], wk, wvt,
                                       cos2, sin2, rope=True)
            uc, cq, *keys_c = _evenproj(ctx, norm1[l], mc[0], mc[1], w_in2, q_norm[i], wq, kv_norm[i], wk, wvt,
                                        cos2, sin2, rope=False)
            attn_x = _attention(q, [keys_x, keys_c])
            pool_x = _pool(ux, wp, pool_scale[i])
            x = _rowmm([attn_x, pool_x], [w_out[:MLA_OUT], w_out[MLA_OUT:]], F32, res=x, gate=mx[2])
            if ctx_update:
                attn_c = _attention(cq, [keys_c])
                pool_c = _pool(uc, wp, pool_scale[i])
                ctx = _rowmm([attn_c, pool_c], [w_out[:MLA_OUT], w_out[MLA_OUT:]], F32, res=ctx, gate=mc[2])
        else:
            w_out = w_out_odd[i].astype(BF16)
            if ctx_update:
                hc = _normmod(ctx, norm1[l], mc[0], mc[1])
            y1 = _dft_stage1(x, norm1[l], mx[0], mx[1], dft['a_kron'])
            fx = _dft_stage2(y1, dft['m'], dft['cc'], dft['sc'])
            x = _rowmm_strided(fx, w_out, x, mx[2])
            if ctx_update:
                fc = _dft_small(hc, dft['m_ctx'], dft['cc'], dft['sc'])
                ctx = _rowmm([fc], [w_out], F32, res=ctx, gate=mc[2])
        x = _mlp(x, norm2[l], mx[3], mx[4], mx[5], w1, w2, l, final_norm if l == DEPTH - 1 else None)
        if ctx_update:
            ctx = _mlp(ctx.reshape(1, b * n_ctx, d), norm2[l], mc[3][:1], mc[4][:1], mc[5][:1],
                       w1, w2, l).reshape(b, n_ctx, d)
    return x
```
